```python
import jax, jax.numpy as jnp
from jax import lax
import numpy as np

D_MODEL = 1024
BATCH = 2
SEQ = 8192
DEPTH = 1

D_CONV = D_MODEL
CONV_WIDTH = 3
HEAD_DIM = 64
N_HEADS = D_MODEL // HEAD_DIM
N_KV_HEADS = N_HEADS // 8
GROUP = N_HEADS // N_KV_HEADS
WINDOW = 128
BLOCK = 128
N_BUCKETS = 32
MAX_DISTANCE = 128
D_FF = 4 * D_MODEL
EPS = 1e-6
NEG_INF = -1e30

Q_COLS = N_HEADS * HEAD_DIM
KV_COLS = N_KV_HEADS * HEAD_DIM
IN_COLS = 3 * D_CONV + Q_COLS + 2 * KV_COLS + 2 * D_MODEL
SPLITS = (D_CONV, 2 * D_CONV, 3 * D_CONV,
          3 * D_CONV + Q_COLS,
          3 * D_CONV + Q_COLS + KV_COLS,
          3 * D_CONV + Q_COLS + 2 * KV_COLS,
          3 * D_CONV + Q_COLS + 2 * KV_COLS + D_MODEL)

kernel_name = "hybrid_conv_swa_gated_block"


def rmsnorm(x, g):
    xf = x.astype(jnp.float32)
    y = xf * lax.rsqrt(jnp.mean(xf * xf, axis=-1, keepdims=True) + EPS)
    return (y * g.astype(jnp.float32)).astype(x.dtype)


def t5_causal_bucket(dist):
    n = np.maximum(dist, 0)
    max_exact = N_BUCKETS // 2
    ratio = np.log(np.maximum(n, max_exact).astype(np.float32) / max_exact) / np.log(MAX_DISTANCE / max_exact)
    large = max_exact + (ratio * (N_BUCKETS - max_exact)).astype(np.int32)
    large = np.minimum(large, N_BUCKETS - 1)
    return np.where(n < max_exact, n, large).astype(np.int32)


def short_conv(u, w):
    c = u.shape[-1]
    return lax.conv_general_dilated(
        u, w[:, None, :].astype(u.dtype), window_strides=(1,),
        padding=[(CONV_WIDTH - 1, 0)],
        dimension_numbers=("NWC", "WIO", "NWC"),
        feature_group_count=c)


def sliding_window_attention(q, k, v, sinks, rel_bias):
    b, t = q.shape[0], q.shape[1]
    nb = t // BLOCK
    qb = q.reshape(b, nb, BLOCK, N_KV_HEADS, GROUP, HEAD_DIM)
    pad = ((0, 0), (BLOCK, 0), (0, 0), (0, 0))
    kp = jnp.pad(k, pad).reshape(b, nb + 1, BLOCK, N_KV_HEADS, HEAD_DIM)
    vp = jnp.pad(v, pad).reshape(b, nb + 1, BLOCK, N_KV_HEADS, HEAD_DIM)
    kb = jnp.concatenate([kp[:, :-1], kp[:, 1:]], axis=2)
    vb = jnp.concatenate([vp[:, :-1], vp[:, 1:]], axis=2)

    s = jnp.einsum("bnqkgd,bnskd->bnkgqs", qb, kb).astype(jnp.float32) * (HEAD_DIM ** -0.5)

    qi = np.arange(BLOCK)[:, None]
    kj = np.arange(2 * BLOCK)[None, :]
    dist = qi + BLOCK - kj
    in_window = (dist >= 0) & (dist < WINDOW)
    bias = rel_bias.astype(jnp.float32)[t5_causal_bucket(dist)]
    bias = jnp.transpose(bias, (2, 0, 1)).reshape(N_KV_HEADS, GROUP, BLOCK, 2 * BLOCK)
    key_pos = np.arange(nb)[:, None] * BLOCK + kj - BLOCK
    mask = in_window[None] & (key_pos >= 0)[:, None, :]
    s = jnp.where(mask[:, None, None], s + bias, NEG_INF)

    sink = jnp.broadcast_to(sinks.astype(jnp.float32).reshape(N_KV_HEADS, GROUP, 1, 1),
                            s.shape[:-1] + (1,))
    p = jax.nn.softmax(jnp.concatenate([s, sink], axis=-1), axis=-1)[..., :-1]
    o = jnp.einsum("bnkgqs,bnskd->bnqkgd", p.astype(vb.dtype), vb)
    return o.reshape(b, t, N_HEADS * HEAD_DIM)


def setup_inputs(seed: int = 0) -> dict:
    key = jax.random.key(seed)
    ks = jax.random.split(key, 14)
    f32 = jnp.float32
    nrm = lambda k, shape, scale: jax.random.normal(k, shape, f32) * scale
    return {
        "x": nrm(ks[0], (BATCH, SEQ, D_MODEL), 1.0),
        "attn_norm_g": 1.0 + nrm(ks[1], (DEPTH, D_MODEL), 0.05),
        "w_in": nrm(ks[2], (DEPTH, D_MODEL, IN_COLS), D_MODEL ** -0.5),
        "conv_w": nrm(ks[3], (DEPTH, CONV_WIDTH, D_CONV), CONV_WIDTH ** -0.5),
        "w_conv_out": nrm(ks[4], (DEPTH, D_CONV, D_MODEL), D_CONV ** -0.5),
        "attn_sinks": nrm(ks[5], (DEPTH, N_HEADS), 0.5),
        "rel_bias": nrm(ks[6], (N_BUCKETS, N_HEADS), 0.5),
        "w_attn_out": nrm(ks[7], (DEPTH, Q_COLS, D_MODEL), Q_COLS ** -0.5),
        "w_o": nrm(ks[8], (DEPTH, D_MODEL, D_MODEL), D_MODEL ** -0.5),
        "mlp_norm_g": 1.0 + nrm(ks[9], (DEPTH, D_MODEL), 0.05),
        "w_up": nrm(ks[10], (DEPTH, D_MODEL, D_FF), D_MODEL ** -0.5),
        "w_down": nrm(ks[11], (DEPTH, D_FF, D_MODEL), D_FF ** -0.5),
        "final_norm_g": 1.0 + nrm(ks[12], (D_MODEL,), 0.05),
    }


def reference(x, attn_norm_g, w_in, conv_w, w_conv_out, attn_sinks, rel_bias, w_attn_out, w_o,
              mlp_norm_g, w_up, w_down, final_norm_g):
    b, t, _ = x.shape
    for l in range(DEPTH):
        h = rmsnorm(x, attn_norm_g[l])
        proj = h @ w_in[l]
        u_b, u_c, u_x, q, k, v, g_conv, g_attn = jnp.split(proj, SPLITS, axis=-1)

        y_conv = (u_b * short_conv(u_c * u_x, conv_w[l])) @ w_conv_out[l]

        attn = sliding_window_attention(
            q.reshape(b, t, N_HEADS, HEAD_DIM),
            k.reshape(b, t, N_KV_HEADS, HEAD_DIM),
            v.reshape(b, t, N_KV_HEADS, HEAD_DIM),
            attn_sinks[l], rel_bias)
        y_attn = attn @ w_attn_out[l]

        mixed = jax.nn.sigmoid(g_conv) * y_conv + jax.nn.sigmoid(g_attn) * y_attn
        x = x + mixed @ w_o[l]

        h2 = rmsnorm(x, mlp_norm_g[l])
        x = x + jnp.square(jax.nn.relu(h2 @ w_up[l])) @ w_down[l]
    return rmsnorm(x, final_norm_g)
```

```python
import functools

import jax
import jax.numpy as jnp
import numpy as np
from jax import lax
from jax.experimental import pallas as pl
from jax.experimental.pallas import tpu as pltpu

D_MODEL = 1024
HEAD_DIM = 64
N_HEADS = 16
N_KV_HEADS = 2
GROUP = N_HEADS // N_KV_HEADS
BLOCK = 128
N_BUCKETS = 32
MAX_DISTANCE = 128
CONV_WIDTH = 3
D_FF = 4 * D_MODEL
EPS = 1e-6
NEG_INF = -1e30

LANES = 128
SUBLANES = 8
PAIRS = N_HEADS // 2
KV_COLS = N_KV_HEADS * HEAD_DIM
ROWS = N_HEADS * BLOCK

OFF_B, OFF_C, OFF_X, OFF_Q = 0, D_MODEL, 2 * D_MODEL, 3 * D_MODEL
OFF_K = 4 * D_MODEL
OFF_V = OFF_K + KV_COLS
OFF_GC = OFF_V + KV_COLS
OFF_GA = OFF_GC + D_MODEL
IN_COLS = OFF_GA + D_MODEL

TILE_MIX = 512
TILE_MLP = 512
VMEM_LIMIT = 56 * 1024 * 1024

BF16 = jnp.bfloat16
F32 = jnp.float32


def _head_permutation():
    perm = np.empty(N_HEADS * HEAD_DIM, np.int32)
    for p in range(PAIRS):
        for e in range(2):
            head = e * GROUP + p
            dst = p * LANES + e * HEAD_DIM
            perm[dst:dst + HEAD_DIM] = np.arange(head * HEAD_DIM, (head + 1) * HEAD_DIM)
    return perm


def _bucket_table():
    qi = np.arange(BLOCK)[:, None]
    kj = np.arange(2 * BLOCK)[None, :]
    dist = qi + BLOCK - kj
    n = np.maximum(dist, 0)
    max_exact = N_BUCKETS // 2
    ratio = np.log(np.maximum(n, max_exact).astype(np.float32) / max_exact) / np.log(MAX_DISTANCE / max_exact)
    large = np.minimum(max_exact + (ratio * (N_BUCKETS - max_exact)).astype(np.int32), N_BUCKETS - 1)
    bucket = np.where(n < max_exact, n, large).astype(np.int32)
    in_window = (dist >= 0) & (dist < BLOCK)
    later = np.where(in_window, bucket, -1)
    first = np.where(in_window & (kj >= BLOCK), bucket, -1)
    return np.stack([later, first]).astype(np.int32)


def _bias_kernel(bucket_ref, rel_ref, out_ref):
    r = pl.program_id(0)
    head = (r % 2) * GROUP + r // 2
    for v in range(2):
        bucket = bucket_ref[v]
        acc = jnp.full(bucket.shape, NEG_INF, F32)
        for b in range(N_BUCKETS):
            acc = jnp.where(bucket == b, rel_ref[b, head], acc)
        out_ref[v, 0] = acc


def _bias_tables(rel_bias):
    bucket = jnp.asarray(_bucket_table())
    return pl.pallas_call(
        _bias_kernel,
        grid=(N_HEADS,),
        in_specs=[
            pl.BlockSpec((2, BLOCK, 2 * BLOCK), lambda r: (0, 0, 0)),
            pl.BlockSpec(memory_space=pltpu.SMEM),
        ],
        out_specs=pl.BlockSpec((2, 1, BLOCK, 2 * BLOCK), lambda r: (0, r, 0, 0)),
        out_shape=jax.ShapeDtypeStruct((2, N_HEADS, BLOCK, 2 * BLOCK), F32),
        name="bias_tables",
    )(bucket, rel_bias.astype(F32))


def _rmsnorm(x, g):
    y = x * lax.rsqrt(jnp.mean(x * x, axis=-1, keepdims=True) + EPS)
    return y * g


def _mixer_kernel(x_ref, g_ref, win_ref, convw_ref, wco_ref, sinks_ref, bias_ref, wao_ref, wo_ref,
                  out_ref, cx_buf, kv_buf, attn_buf, *, tile):
    t = pl.program_id(1)

    @pl.when(t == 0)
    def _():
        cx_buf[0:SUBLANES, :] = jnp.zeros((SUBLANES, D_MODEL), F32)
        kv_buf[0:BLOCK, :] = jnp.zeros((BLOCK, 2 * KV_COLS), BF16)

    x = x_ref[0]
    h = _rmsnorm(x, g_ref[...]).astype(BF16)

    def proj(lo, width):
        return jnp.dot(h, win_ref[:, lo:lo + width], preferred_element_type=F32)

    cx = proj(OFF_C, D_MODEL) * proj(OFF_X, D_MODEL)
    cx_buf[SUBLANES:SUBLANES + tile, :] = cx
    conv = (convw_ref[0:1, :] * cx_buf[SUBLANES - 2:SUBLANES - 2 + tile, :]
            + convw_ref[1:2, :] * cx_buf[SUBLANES - 1:SUBLANES - 1 + tile, :]
            + convw_ref[2:3, :] * cx)
    cx_buf[0:SUBLANES, :] = cx_buf[tile:tile + SUBLANES, :]
    u = (proj(OFF_B, D_MODEL) * conv).astype(BF16)
    y_conv = jnp.dot(u, wco_ref[...], preferred_element_type=F32)
    mixed = jax.nn.sigmoid(proj(OFF_GC, D_MODEL)) * y_conv

    q = (proj(OFF_Q, D_MODEL) * (HEAD_DIM ** -0.5)).astype(BF16)
    kv_buf[BLOCK:BLOCK + tile, :] = proj(OFF_K, 2 * KV_COLS).astype(BF16)
    low_half = lax.broadcasted_iota(jnp.int32, (BLOCK, LANES), 1) < HEAD_DIM
    zero = jnp.zeros((BLOCK, LANES), BF16)
    for j in range(tile // BLOCK):
        qj = q[j * BLOCK:(j + 1) * BLOCK, :]
        keys = kv_buf[j * BLOCK:(j + 2) * BLOCK, 0:KV_COLS]
        vals = kv_buf[j * BLOCK:(j + 2) * BLOCK, KV_COLS:2 * KV_COLS]
        parts = []
        for p in range(PAIRS):
            qp = qj[:, p * LANES:(p + 1) * LANES]
            parts.append(jnp.where(low_half, qp, zero))
            parts.append(jnp.where(low_half, zero, qp))
        lhs = jnp.concatenate(parts, axis=0)
        s = lax.dot_general(lhs, keys, (((1,), (1,)), ((), ())), preferred_element_type=F32)
        first = (t == 0).astype(jnp.int32) if j == 0 else 0
        probs, inv = [], []
        for r in range(N_HEADS):
            head = (r % 2) * GROUP + r // 2
            sr = s[r * BLOCK:(r + 1) * BLOCK, :] + bias_ref[first, r]
            sink = sinks_ref[head]
            m = jnp.maximum(jnp.max(sr, axis=-1, keepdims=True), sink)
            e = jnp.exp(sr - m)
            inv.append(1.0 / (jnp.sum(e, axis=-1, keepdims=True) + jnp.exp(sink - m)))
            probs.append(e.astype(BF16))
        o = jnp.dot(jnp.concatenate(probs, axis=0), vals, preferred_element_type=F32)
        for p in range(PAIRS):
            lo = o[(2 * p) * BLOCK:(2 * p + 1) * BLOCK, :] * inv[2 * p]
            hi = o[(2 * p + 1) * BLOCK:(2 * p + 2) * BLOCK, :] * inv[2 * p + 1]
            attn_buf[j * BLOCK:(j + 1) * BLOCK, p * LANES:(p + 1) * LANES] = jnp.where(low_half, lo, hi).astype(BF16)
    kv_buf[0:BLOCK, :] = kv_buf[tile:tile + BLOCK, :]
    y_attn = jnp.dot(attn_buf[...], wao_ref[...], preferred_element_type=F32)
    mixed = mixed + jax.nn.sigmoid(proj(OFF_GA, D_MODEL)) * y_attn

    out_ref[0] = x + jnp.dot(mixed.astype(BF16), wo_ref[...], preferred_element_type=F32)


def _mlp_kernel(x_ref, g_ref, wup_ref, wdown_ref, gf_ref, out_ref):
    x = x_ref[...]
    h = _rmsnorm(x, g_ref[...]).astype(BF16)
    up = jnp.dot(h, wup_ref[...], preferred_element_type=F32)
    act = jnp.square(jnp.maximum(up, 0.0)).astype(BF16)
    y = x + jnp.dot(act, wdown_ref[...], preferred_element_type=F32)
    out_ref[...] = _rmsnorm(y, gf_ref[...])


def _resident(shape):
    return pl.BlockSpec(shape, lambda *_: (0,) * len(shape), pipeline_mode=pl.Buffered(1))


def _mixer(x, g, w_in, conv_w, w_conv_out, sinks, bias, w_attn_out, w_o):
    batch, seq, _ = x.shape
    tile = TILE_MIX
    assert seq % tile == 0 and tile % BLOCK == 0
    return pl.pallas_call(
        functools.partial(_mixer_kernel, tile=tile),
        grid=(batch, seq // tile),
        in_specs=[
            pl.BlockSpec((1, tile, D_MODEL), lambda b, t: (b, t, 0)),
            _resident((1, D_MODEL)),
            _resident((D_MODEL, IN_COLS)),
            _resident((CONV_WIDTH, D_MODEL)),
            _resident((D_MODEL, D_MODEL)),
            pl.BlockSpec(memory_space=pltpu.SMEM),
            _resident((2, N_HEADS, BLOCK, 2 * BLOCK)),
            _resident((D_MODEL, D_MODEL)),
            _resident((D_MODEL, D_MODEL)),
        ],
        out_specs=pl.BlockSpec((1, tile, D_MODEL), lambda b, t: (b, t, 0)),
        out_shape=jax.ShapeDtypeStruct(x.shape, F32),
        scratch_shapes=[
            pltpu.VMEM((tile + 2 * SUBLANES, D_MODEL), F32),
            pltpu.VMEM((tile + BLOCK, 2 * KV_COLS), BF16),
            pltpu.VMEM((tile, D_MODEL), BF16),
        ],
        compiler_params=pltpu.CompilerParams(
            dimension_semantics=("arbitrary", "arbitrary"), vmem_limit_bytes=VMEM_LIMIT),
        name="token_mixer",
    )(x, g, w_in, conv_w, w_conv_out, sinks, bias, w_attn_out, w_o)


def _mlp(x, g, w_up, w_down, g_final):
    rows = x.shape[0]
    tile = TILE_MLP
    assert rows % tile == 0
    return pl.pallas_call(
        _mlp_kernel,
        grid=(rows // tile,),
        in_specs=[
            pl.BlockSpec((tile, D_MODEL), lambda i: (i, 0)),
            _resident((1, D_MODEL)),
            _resident((D_MODEL, D_FF)),
            _resident((D_FF, D_MODEL)),
            _resident((1, D_MODEL)),
        ],
        out_specs=pl.BlockSpec((tile, D_MODEL), lambda i: (i, 0)),
        out_shape=jax.ShapeDtypeStruct(x.shape, F32),
        compiler_params=pltpu.CompilerParams(
            dimension_semantics=("arbitrary",), vmem_limit_bytes=VMEM_LIMIT),
        name="channel_mixer",
    )(x, g, w_up, w_down, g_final)


def kernel(x, attn_norm_g, w_in, conv_w, w_conv_out, attn_sinks, rel_bias, w_attn_out, w_o, mlp_norm_g, w_up,
           w_down, final_norm_g):
    batch, seq, _ = x.shape
    depth = w_in.shape[0]
    perm = _head_permutation()
    bias = _bias_tables(rel_bias)
    for l in range(depth):
        w = w_in[l]
        w_perm = jnp.concatenate(
            [w[:, :OFF_Q], w[:, OFF_Q:OFF_K][:, perm], w[:, OFF_K:]], axis=1).astype(BF16)
        x = _mixer(x, attn_norm_g[l][None, :], w_perm, conv_w[l], w_conv_out[l].astype(BF16),
                   attn_sinks[l].astype(F32), bias, w_attn_out[l][perm, :].astype(BF16), w_o[l].astype(BF16))
        last = l == depth - 1
        assert last, "only DEPTH == 1 fuses the final norm"
        x = _mlp(x.reshape(batch * seq, D_MODEL), mlp_norm_g[l][None, :], w_up[l].astype(BF16),
                 w_down[l].astype(BF16), final_norm_g[None, :]).reshape(batch, seq, D_MODEL)
    return x
```

```python
import functools

import jax
import jax.numpy as jnp
import numpy as np
from jax import lax
from jax.experimental import pallas as pl
from jax.experimental.pallas import tpu as pltpu

D_MODEL = 1024
HEAD_DIM = 64
N_HEADS = 16
N_KV_HEADS = 2
GROUP = N_HEADS // N_KV_HEADS
BLOCK = 128
N_BUCKETS = 32
MAX_DISTANCE = 128
CONV_WIDTH = 3
D_FF = 4 * D_MODEL
EPS = 1e-6
NEG_INF = -1e30

LANES = 128
SUBLANES = 8
KV_COLS = N_KV_HEADS * HEAD_DIM
PAIRS_PER_GROUP = GROUP // 2

OFF_B, OFF_C, OFF_X, OFF_Q = 0, D_MODEL, 2 * D_MODEL, 3 * D_MODEL
OFF_K = 4 * D_MODEL
OFF_V = OFF_K + KV_COLS
OFF_GC = OFF_V + KV_COLS
OFF_GA = OFF_GC + D_MODEL
IN_COLS = OFF_GA + D_MODEL

TILE_MIX = 512
TILE_MLP = 512
VMEM_LIMIT = 56 * 1024 * 1024

BF16 = jnp.bfloat16
F32 = jnp.float32

NT_DIMS = (((1,), (1,)), ((), ()))
TN_DIMS = (((0,), (0,)), ((), ()))


def _bucket_table():
    kj = np.arange(2 * BLOCK)[:, None]
    qi = np.arange(BLOCK)[None, :]
    dist = qi + BLOCK - kj
    n = np.maximum(dist, 0)
    max_exact = N_BUCKETS // 2
    ratio = np.log(np.maximum(n, max_exact).astype(np.float32) / max_exact) / np.log(MAX_DISTANCE / max_exact)
    large = np.minimum(max_exact + (ratio * (N_BUCKETS - max_exact)).astype(np.int32), N_BUCKETS - 1)
    bucket = np.where(n < max_exact, n, large).astype(np.int32)
    in_window = (dist >= 0) & (dist < BLOCK)
    return np.where(in_window, bucket, -1).astype(np.int32)


def _bias_kernel(bucket_ref, rel_ref, out_ref):
    head = pl.program_id(0)
    bucket = bucket_ref[...]
    acc = jnp.full(bucket.shape, NEG_INF, F32)
    for b in range(N_BUCKETS):
        acc = jnp.where(bucket == b, rel_ref[b, head], acc)
    out_ref[0, 0] = acc
    key = lax.broadcasted_iota(jnp.int32, bucket.shape, 0)
    out_ref[1, 0] = jnp.where(key >= BLOCK, acc, NEG_INF)


def _bias_tables(rel_bias):
    bucket = jnp.asarray(_bucket_table())
    return pl.pallas_call(
        _bias_kernel,
        grid=(N_HEADS,),
        in_specs=[
            pl.BlockSpec((2 * BLOCK, BLOCK), lambda h: (0, 0)),
            pl.BlockSpec(memory_space=pltpu.SMEM),
        ],
        out_specs=pl.BlockSpec((2, 1, 2 * BLOCK, BLOCK), lambda h: (0, h, 0, 0)),
        out_shape=jax.ShapeDtypeStruct((2, N_HEADS, 2 * BLOCK, BLOCK), F32),
        name="bias_tables",
    )(bucket, rel_bias.astype(F32))


def _rmsnorm(x, g):
    y = x * lax.rsqrt(jnp.mean(x * x, axis=-1, keepdims=True) + EPS)
    return y * g


def _mixer_kernel(x_ref, g_ref, win_ref, convw_ref, wco_ref, sinks_ref, bias_ref, wao_ref, wo_ref,
                  out_ref, cx_buf, kv_buf, attn_buf, *, tile):
    t = pl.program_id(1)

    @pl.when(t == 0)
    def _():
        cx_buf[0:SUBLANES, :] = jnp.zeros((SUBLANES, D_MODEL), F32)
        kv_buf[0:BLOCK, :] = jnp.zeros((BLOCK, 3 * KV_COLS), BF16)

    x = x_ref[0]
    h = _rmsnorm(x, g_ref[...]).astype(BF16)

    def proj(lo, width):
        return jnp.dot(h, win_ref[:, lo:lo + width], preferred_element_type=F32)

    q = (proj(OFF_Q, D_MODEL) * (HEAD_DIM ** -0.5)).astype(BF16)
    kv = proj(OFF_K, 2 * KV_COLS)
    k = kv[:, 0:KV_COLS]
    k_swapped = pltpu.roll(k, HEAD_DIM, axis=1)
    low_half_tile = lax.broadcasted_iota(jnp.int32, (tile, LANES), 1) < HEAD_DIM
    kv_buf[BLOCK:BLOCK + tile, 0:KV_COLS] = jnp.where(low_half_tile, k, k_swapped).astype(BF16)
    kv_buf[BLOCK:BLOCK + tile, KV_COLS:2 * KV_COLS] = jnp.where(low_half_tile, k_swapped, k).astype(BF16)
    kv_buf[BLOCK:BLOCK + tile, 2 * KV_COLS:3 * KV_COLS] = kv[:, KV_COLS:2 * KV_COLS].astype(BF16)

    low_half = lax.broadcasted_iota(jnp.int32, (BLOCK, LANES), 1) < HEAD_DIM
    zero = jnp.zeros((BLOCK, LANES), BF16)

    def attend(j, fillers=()):
        fillers = list(fillers)
        rows = slice(j * BLOCK, (j + 2) * BLOCK)
        qj = q[j * BLOCK:(j + 1) * BLOCK, :]
        first = (t == 0).astype(jnp.int32) if j == 0 else 0
        scores = []
        for g in range(N_KV_HEADS):
            parts = []
            for p in range(g * PAIRS_PER_GROUP, (g + 1) * PAIRS_PER_GROUP):
                qp = qj[:, p * LANES:(p + 1) * LANES]
                parts.append(jnp.where(low_half, qp, zero))
                parts.append(jnp.where(low_half, zero, qp))
            q_heads = jnp.concatenate(parts, axis=0)
            keys = kv_buf[rows, g * KV_COLS:(g + 1) * KV_COLS]
            scores.append(lax.dot_general(keys, q_heads, NT_DIMS, preferred_element_type=F32))
        probs, inv = [], []
        for g in range(N_KV_HEADS):
            if fillers:
                fillers.pop(0)()
            s = scores[g]
            for i in range(GROUP):
                head = g * GROUP + i
                sh = s[:, i * BLOCK:(i + 1) * BLOCK] + bias_ref[first, head]
                sink = sinks_ref[head]
                m = jnp.maximum(jnp.max(sh, axis=0, keepdims=True), sink)
                e = jnp.exp(sh - m)
                inv.append(1.0 / (jnp.sum(e, axis=0, keepdims=True) + jnp.exp(sink - m)))
                probs.append(e.astype(BF16))
        for f in fillers:
            f()
        vals = kv_buf[rows, 2 * KV_COLS:3 * KV_COLS]
        o = lax.dot_general(vals, jnp.concatenate(probs, axis=1), TN_DIMS,
                            preferred_element_type=F32)
        for head in range(N_HEADS):
            g = head // GROUP
            oh = o[g * HEAD_DIM:(g + 1) * HEAD_DIM, head * BLOCK:(head + 1) * BLOCK] * inv[head]
            attn_buf[head * HEAD_DIM:(head + 1) * HEAD_DIM, j * BLOCK:(j + 1) * BLOCK] = oh.astype(BF16)

    v = {}

    def t_c():
        v["c"] = proj(OFF_C, D_MODEL)

    def t_x():
        v["cx"] = v["c"] * proj(OFF_X, D_MODEL)

    def t_conv_b():
        gate_b = proj(OFF_B, D_MODEL)
        cx = v["cx"]
        cx_buf[SUBLANES:SUBLANES + tile, :] = cx
        conv = (convw_ref[0:1, :] * cx_buf[SUBLANES - 2:SUBLANES - 2 + tile, :]
                + convw_ref[1:2, :] * cx_buf[SUBLANES - 1:SUBLANES - 1 + tile, :]
                + convw_ref[2:3, :] * cx)
        cx_buf[0:SUBLANES, :] = cx_buf[tile:tile + SUBLANES, :]
        v["u"] = (gate_b * conv).astype(BF16)

    def t_yconv():
        v["y_conv"] = jnp.dot(v["u"], wco_ref[...], preferred_element_type=F32)

    def t_gc():
        v["mixed"] = jax.nn.sigmoid(proj(OFF_GC, D_MODEL)) * v["y_conv"]

    def t_ga():
        v["gate_attn"] = jax.nn.sigmoid(proj(OFF_GA, D_MODEL))

    half = tile // 2

    def t_yattn(r):
        def task():
            rows = slice(r * half, (r + 1) * half)
            y_attn = lax.dot_general(attn_buf[:, rows], wao_ref[...], TN_DIMS, preferred_element_type=F32)
            v["mixed", r] = (v["mixed"][rows] + v["gate_attn"][rows] * y_attn).astype(BF16)
        return task

    def t_out(r):
        def task():
            rows = slice(r * half, (r + 1) * half)
            out_ref[0, rows, :] = x[rows] + jnp.dot(v["mixed", r], wo_ref[...], preferred_element_type=F32)
        return task

    attend(0, [t_c, t_x])
    attend(1, [t_conv_b, t_yconv])
    attend(2, [t_gc, t_ga])
    attend(3, [t_yattn(0), t_out(0)])
    kv_buf[0:BLOCK, :] = kv_buf[tile:tile + BLOCK, :]
    t_yattn(1)()
    t_out(1)()


def _mlp_kernel(x_ref, g_ref, wup_ref, wdown_ref, gf_ref, out_ref):
    x = x_ref[...]
    h = _rmsnorm(x, g_ref[...]).astype(BF16)
    up = jnp.dot(h, wup_ref[...], preferred_element_type=F32)
    act = jnp.square(jnp.maximum(up, 0.0)).astype(BF16)
    y = x + jnp.dot(act, wdown_ref[...], preferred_element_type=F32)
    out_ref[...] = _rmsnorm(y, gf_ref[...])


def _resident(shape):
    return pl.BlockSpec(shape, lambda *_: (0,) * len(shape), pipeline_mode=pl.Buffered(1))


def _mixer(x, g, w_in, conv_w, w_conv_out, sinks, bias, w_attn_out, w_o):
    batch, seq, _ = x.shape
    tile = TILE_MIX
    assert seq % tile == 0 and tile % BLOCK == 0
    return pl.pallas_call(
        functools.partial(_mixer_kernel, tile=tile),
        grid=(batch, seq // tile),
        in_specs=[
            pl.BlockSpec((1, tile, D_MODEL), lambda b, t: (b, t, 0)),
            _resident((1, D_MODEL)),
            _resident((D_MODEL, IN_COLS)),
            _resident((CONV_WIDTH, D_MODEL)),
            _resident((D_MODEL, D_MODEL)),
            pl.BlockSpec(memory_space=pltpu.SMEM),
            _resident((2, N_HEADS, 2 * BLOCK, BLOCK)),
            _resident((D_MODEL, D_MODEL)),
            _resident((D_MODEL, D_MODEL)),
        ],
        out_specs=pl.BlockSpec((1, tile, D_MODEL), lambda b, t: (b, t, 0)),
        out_shape=jax.ShapeDtypeStruct(x.shape, F32),
        scratch_shapes=[
            pltpu.VMEM((tile + 2 * SUBLANES, D_MODEL), F32),
            pltpu.VMEM((tile + BLOCK, 3 * KV_COLS), BF16),
            pltpu.VMEM((D_MODEL, tile), BF16),
        ],
        compiler_params=pltpu.CompilerParams(
            dimension_semantics=("arbitrary", "arbitrary"), vmem_limit_bytes=VMEM_LIMIT),
        name="token_mixer",
    )(x, g, w_in, conv_w, w_conv_out, sinks, bias, w_attn_out, w_o)


def _mlp(x, g, w_up, w_down, g_final):
    rows = x.shape[0]
    tile = TILE_MLP
    assert rows % tile == 0
    return pl.pallas_call(
        _mlp_kernel,
        grid=(rows // tile,),
        in_specs=[
            pl.BlockSpec((tile, D_MODEL), lambda i: (i, 0)),
            _resident((1, D_MODEL)),
            _resident((D_MODEL, D_FF)),
            _resident((D_FF, D_MODEL)),
            _resident((1, D_MODEL)),
        ],
        out_specs=pl.BlockSpec((tile, D_MODEL), lambda i: (i, 0)),
        out_shape=jax.ShapeDtypeStruct(x.shape, F32),
        compiler_params=pltpu.CompilerParams(
            dimension_semantics=("arbitrary",), vmem_limit_bytes=VMEM_LIMIT),
        name="channel_mixer",
    )(x, g, w_up, w_down, g_final)


def kernel(x, attn_norm_g, w_in, conv_w, w_conv_out, attn_sinks, rel_bias, w_attn_out, w_o, mlp_norm_g, w_up,
           w_down, final_norm_g):
    batch, seq, _ = x.shape
    depth = w_in.shape[0]
    assert depth == 1, "the final norm is fused into the (only) layer's channel mixer"
    bias = _bias_tables(rel_bias)
    x = _mixer(x, attn_norm_g[0][None, :], w_in[0].astype(BF16), conv_w[0], w_conv_out[0].astype(BF16),
               attn_sinks[0].astype(F32), bias, w_attn_out[0].astype(BF16), w_o[0].astype(BF16))
    x = _mlp(x.reshape(batch * seq, D_MODEL), mlp_norm_g[0][None, :], w_up[0].astype(BF16),
             w_down[0].astype(BF16), final_norm_g[None, :])
    return x.reshape(batch, seq, D_MODEL)
```

```python
import functools
import math

import jax
import jax.numpy as jnp
import numpy as np
from jax import lax
from jax.experimental import pallas as pl
from jax.experimental.pallas import tpu as pltpu

D_MODEL = 1024
HEAD_DIM = 64
N_HEADS = 16
N_KV_HEADS = 2
GROUP = N_HEADS // N_KV_HEADS
BLOCK = 128
N_BUCKETS = 32
MAX_DISTANCE = 128
CONV_WIDTH = 3
D_FF = 4 * D_MODEL
EPS = 1e-6
NEG_INF = -1e30
LOG2E = math.log2(math.e)

LANES = 128
SUBLANES = 8
KV_COLS = N_KV_HEADS * HEAD_DIM
PAIRS_PER_GROUP = GROUP // 2

OFF_B, OFF_C, OFF_X, OFF_Q = 0, D_MODEL, 2 * D_MODEL, 3 * D_MODEL
OFF_K = 4 * D_MODEL
OFF_V = OFF_K + KV_COLS
OFF_GC = OFF_V + KV_COLS
OFF_GA = OFF_GC + D_MODEL
IN_COLS = OFF_GA + D_MODEL

TILE_MIX = 512
TILE_MLP = 512
VMEM_LIMIT = 56 * 1024 * 1024

BF16 = jnp.bfloat16
F32 = jnp.float32

NT_DIMS = (((1,), (1,)), ((), ()))
TN_DIMS = (((0,), (0,)), ((), ()))


def _bucket_table():
    kj = np.arange(2 * BLOCK)[:, None]
    qi = np.arange(BLOCK)[None, :]
    dist = qi + BLOCK - kj
    n = np.maximum(dist, 0)
    max_exact = N_BUCKETS // 2
    ratio = np.log(np.maximum(n, max_exact).astype(np.float32) / max_exact) / np.log(MAX_DISTANCE / max_exact)
    large = np.minimum(max_exact + (ratio * (N_BUCKETS - max_exact)).astype(np.int32), N_BUCKETS - 1)
    bucket = np.where(n < max_exact, n, large).astype(np.int32)
    in_window = (dist >= 0) & (dist < BLOCK)
    return np.where(in_window, bucket, -1).astype(np.int32)


def _bias_kernel(bucket_ref, rel_ref, out_ref):
    head = pl.program_id(0)
    bucket = bucket_ref[...]
    acc = jnp.full(bucket.shape, NEG_INF, F32)
    for b in range(N_BUCKETS):
        acc = jnp.where(bucket == b, rel_ref[b, head], acc)
    acc = acc * LOG2E
    out_ref[0, 0] = acc
    key = lax.broadcasted_iota(jnp.int32, bucket.shape, 0)
    out_ref[1, 0] = jnp.where(key >= BLOCK, acc, NEG_INF)


def _bias_tables(rel_bias):
    bucket = jnp.asarray(_bucket_table())
    return pl.pallas_call(
        _bias_kernel,
        grid=(N_HEADS,),
        in_specs=[
            pl.BlockSpec((2 * BLOCK, BLOCK), lambda h: (0, 0)),
            pl.BlockSpec(memory_space=pltpu.SMEM),
        ],
        out_specs=pl.BlockSpec((2, 1, 2 * BLOCK, BLOCK), lambda h: (0, h, 0, 0)),
        out_shape=jax.ShapeDtypeStruct((2, N_HEADS, 2 * BLOCK, BLOCK), F32),
        name="bias_tables",
    )(bucket, rel_bias.astype(F32))


def _rmsnorm(x, g):
    y = x * lax.rsqrt(jnp.mean(x * x, axis=-1, keepdims=True) + EPS)
    return y * g


def _mixer_kernel(x_ref, g_ref, win_ref, convw_ref, wco_ref, sinks_ref, bias_ref, wao_ref, wo_ref,
                  out_ref, cx_buf, kv_buf, attn_buf, *, tile):
    t = pl.program_id(1)

    @pl.when(t == 0)
    def _():
        cx_buf[0:SUBLANES, :] = jnp.zeros((SUBLANES, D_MODEL), F32)
        kv_buf[0:BLOCK, :] = jnp.zeros((BLOCK, 4 * KV_COLS), BF16)

    x = x_ref[0]
    h = _rmsnorm(x, g_ref[...]).astype(BF16)

    def proj(lo, width):
        return jnp.dot(h, win_ref[:, lo:lo + width], preferred_element_type=F32)

    q = (proj(OFF_Q, D_MODEL) * (HEAD_DIM ** -0.5 * LOG2E)).astype(BF16)
    half = tile // 2
    w_kv = win_ref[:, OFF_K:OFF_K + 2 * KV_COLS]
    kv = jnp.concatenate([jnp.dot(h[0:half], w_kv, preferred_element_type=F32),
                          jnp.dot(h[half:tile], w_kv, preferred_element_type=F32)], axis=0)
    k = kv[:, 0:KV_COLS]
    val = kv[:, KV_COLS:2 * KV_COLS]
    k_swapped = pltpu.roll(k, HEAD_DIM, axis=1)
    low_half_tile = lax.broadcasted_iota(jnp.int32, (tile, LANES), 1) < HEAD_DIM
    new_rows = slice(BLOCK, BLOCK + tile)
    kv_buf[new_rows, 0:KV_COLS] = jnp.where(low_half_tile, k, k_swapped).astype(BF16)
    kv_buf[new_rows, KV_COLS:2 * KV_COLS] = jnp.where(low_half_tile, k_swapped, k).astype(BF16)
    kv_buf[new_rows, 2 * KV_COLS:3 * KV_COLS] = jnp.where(low_half_tile, val, 1.0).astype(BF16)
    kv_buf[new_rows, 3 * KV_COLS:4 * KV_COLS] = jnp.where(low_half_tile, 1.0, val).astype(BF16)

    low_half = lax.broadcasted_iota(jnp.int32, (BLOCK, LANES), 1) < HEAD_DIM
    zero = jnp.zeros((BLOCK, LANES), BF16)

    scores, probs, sink_terms = {}, {}, {}

    def qk(j):
        rows = slice(j * BLOCK, (j + 2) * BLOCK)
        qj = q[j * BLOCK:(j + 1) * BLOCK, :]
        for g in range(N_KV_HEADS):
            parts = []
            for p in range(g * PAIRS_PER_GROUP, (g + 1) * PAIRS_PER_GROUP):
                qp = qj[:, p * LANES:(p + 1) * LANES]
                parts.append(jnp.where(low_half, qp, zero))
                parts.append(jnp.where(low_half, zero, qp))
            q_heads = jnp.concatenate(parts, axis=0)
            keys = kv_buf[rows, g * KV_COLS:(g + 1) * KV_COLS]
            scores[j, g] = lax.dot_general(keys, q_heads, NT_DIMS, preferred_element_type=F32)

    def softmax(j, g):
        first = (t == 0).astype(jnp.int32) if j == 0 else 0
        s = scores.pop((j, g))
        for i in range(GROUP):
            head = g * GROUP + i
            sh = s[:, i * BLOCK:(i + 1) * BLOCK] + bias_ref[first, head]
            sink = jnp.full((1, BLOCK), sinks_ref[head], F32) * LOG2E
            m = jnp.maximum(jnp.max(sh, axis=0, keepdims=True), sink)
            probs[j, head] = jnp.exp2(sh - m).astype(BF16)
            sink_terms[j, head] = jnp.exp2(sink - m)

    def pv(j):
        rows = slice(j * BLOCK, (j + 2) * BLOCK)
        for g in range(N_KV_HEADS):
            vals = kv_buf[rows, (2 + g) * KV_COLS:(3 + g) * KV_COLS]
            p = jnp.concatenate([probs.pop((j, g * GROUP + i)) for i in range(GROUP)], axis=1)
            o = lax.dot_general(vals, p, TN_DIMS, preferred_element_type=F32)
            out_rows = slice(g * HEAD_DIM, (g + 1) * HEAD_DIM)
            sum_row = (1 - g) * HEAD_DIM
            for i in range(GROUP):
                head = g * GROUP + i
                cols = slice(i * BLOCK, (i + 1) * BLOCK)
                denom = o[sum_row:sum_row + 1, cols] + sink_terms.pop((j, head))
                oh = o[out_rows, cols] * (1.0 / denom)
                attn_buf[head * HEAD_DIM:(head + 1) * HEAD_DIM, j * BLOCK:(j + 1) * BLOCK] = oh.astype(BF16)

    v = {}

    def t_c():
        v["c"] = proj(OFF_C, D_MODEL)

    def t_x():
        cx = v.pop("c") * proj(OFF_X, D_MODEL)
        cx_buf[SUBLANES:SUBLANES + tile, :] = cx
        v["cx"] = cx

    def t_conv():
        v["conv"] = (convw_ref[0:1, :] * cx_buf[SUBLANES - 2:SUBLANES - 2 + tile, :]
                     + convw_ref[1:2, :] * cx_buf[SUBLANES - 1:SUBLANES - 1 + tile, :]
                     + convw_ref[2:3, :] * v.pop("cx"))
        cx_buf[0:SUBLANES, :] = cx_buf[tile:tile + SUBLANES, :]

    def t_b():
        v["u"] = (proj(OFF_B, D_MODEL) * v.pop("conv")).astype(BF16)

    def t_gc():
        v["gate_conv"] = jax.nn.sigmoid(proj(OFF_GC, D_MODEL))

    def t_ga():
        v["gate_attn"] = jax.nn.sigmoid(proj(OFF_GA, D_MODEL))

    def t_yconv():
        v["mixed"] = v.pop("gate_conv") * jnp.dot(v.pop("u"), wco_ref[...], preferred_element_type=F32)

    def t_yattn(r):
        rows = slice(r * half, (r + 1) * half)
        y_attn = lax.dot_general(attn_buf[:, rows], wao_ref[...], TN_DIMS, preferred_element_type=F32)
        v["mixed", r] = (v["mixed"][rows] + v["gate_attn"][rows] * y_attn).astype(BF16)

    def t_out(r):
        rows = slice(r * half, (r + 1) * half)
        out_ref[0, rows, :] = x[rows] + jnp.dot(v.pop(("mixed", r)), wo_ref[...], preferred_element_type=F32)

    qk(0)
    t_c()
    softmax(0, 0)
    t_x()
    softmax(0, 1)
    t_gc()
    t_conv()
    pv(0)
    qk(1)
    t_b()
    softmax(1, 0)
    t_ga()
    softmax(1, 1)
    pv(1)
    qk(2)
    t_yconv()
    softmax(2, 0)
    t_yattn(0)
    softmax(2, 1)
    pv(2)
    qk(3)
    t_out(0)
    softmax(3, 0)
    softmax(3, 1)
    pv(3)
    kv_buf[0:BLOCK, :] = kv_buf[tile:tile + BLOCK, :]
    t_yattn(1)
    t_out(1)


def _mlp_kernel(x_ref, g_ref, wup_ref, wdown_ref, gf_ref, out_ref):
    x = x_ref[...]
    h = _rmsnorm(x, g_ref[...]).astype(BF16)
    up = jnp.dot(h, wup_ref[...], preferred_element_type=F32)
    act = jnp.square(jnp.maximum(up, 0.0)).astype(BF16)
    y = x + jnp.dot(act, wdown_ref[...], preferred_element_type=F32)
    out_ref[...] = _rmsnorm(y, gf_ref[...])


def _resident(shape):
    return pl.BlockSpec(shape, lambda *_: (0,) * len(shape), pipeline_mode=pl.Buffered(1))


def _mixer(x, g, w_in, conv_w, w_conv_out, sinks, bias, w_attn_out, w_o):
    batch, seq, _ = x.shape
    tile = TILE_MIX
    assert seq % tile == 0 and tile % BLOCK == 0
    return pl.pallas_call(
        functools.partial(_mixer_kernel, tile=tile),
        grid=(batch, seq // tile),
        in_specs=[
            pl.BlockSpec((1, tile, D_MODEL), lambda b, t: (b, t, 0)),
            _resident((1, D_MODEL)),
            _resident((D_MODEL, IN_COLS)),
            _resident((CONV_WIDTH, D_MODEL)),
            _resident((D_MODEL, D_MODEL)),
            pl.BlockSpec(memory_space=pltpu.SMEM),
            _resident((2, N_HEADS, 2 * BLOCK, BLOCK)),
            _resident((D_MODEL, D_MODEL)),
            _resident((D_MODEL, D_MODEL)),
        ],
        out_specs=pl.BlockSpec((1, tile, D_MODEL), lambda b, t: (b, t, 0)),
        out_shape=jax.ShapeDtypeStruct(x.shape, F32),
        scratch_shapes=[
            pltpu.VMEM((tile + 2 * SUBLANES, D_MODEL), F32),
            pltpu.VMEM((tile + BLOCK, 4 * KV_COLS), BF16),
            pltpu.VMEM((D_MODEL, tile), BF16),
        ],
        compiler_params=pltpu.CompilerParams(
            dimension_semantics=("arbitrary", "arbitrary"), vmem_limit_bytes=VMEM_LIMIT),
        name="token_mixer",
    )(x, g, w_in, conv_w, w_conv_out, sinks, bias, w_attn_out, w_o)


def _mlp(x, g, w_up, w_down, g_final):
    rows = x.shape[0]
    tile = TILE_MLP
    assert rows % tile == 0
    return pl.pallas_call(
        _mlp_kernel,
        grid=(rows // tile,),
        in_specs=[
            pl.BlockSpec((tile, D_MODEL), lambda i: (i, 0)),
            _resident((1, D_MODEL)),
            _resident((D_MODEL, D_FF)),
            _resident((D_FF, D_MODEL)),
            _resident((1, D_MODEL)),
        ],
        out_specs=pl.BlockSpec((tile, D_MODEL), lambda i: (i, 0)),
        out_shape=jax.ShapeDtypeStruct(x.shape, F32),
        compiler_params=pltpu.CompilerParams(
            dimension_semantics=("arbitrary",), vmem_limit_bytes=VMEM_LIMIT),
        name="channel_mixer",
    )(x, g, w_up, w_down, g_final)


def kernel(x, attn_norm_g, w_in, conv_w, w_conv_out, attn_sinks, rel_bias, w_attn_out, w_o, mlp_norm_g, w_up,
           w_down, final_norm_g):
    batch, seq, _ = x.shape
    depth = w_in.shape[0]
    assert depth == 1, "the final norm is fused into the (only) layer's channel mixer"
    bias = _bias_tables(rel_bias)
    x = _mixer(x, attn_norm_g[0][None, :], w_in[0].astype(BF16), conv_w[0], w_conv_out[0].astype(BF16),
               attn_sinks[0].astype(F32), bias, w_attn_out[0].astype(BF16), w_o[0].astype(BF16))
    x = _mlp(x.reshape(batch * seq, D_MODEL), mlp_norm_g[0][None, :], w_up[0].astype(BF16),
             w_down[0].astype(BF16), final_norm_g[None, :])
    return x.reshape(batch, seq, D_MODEL)
```

```python
import functools
import math

import jax
import jax.numpy as jnp
import numpy as np
from jax import lax
from jax.experimental import pallas as pl
from jax.experimental.pallas import tpu as pltpu

D_MODEL = 1024
HEAD_DIM = 64
N_HEADS = 16
N_KV_HEADS = 2
GROUP = N_HEADS // N_KV_HEADS
BLOCK = 128
N_BUCKETS = 32
MAX_DISTANCE = 128
CONV_WIDTH = 3
D_FF = 4 * D_MODEL
EPS = 1e-6
NEG_INF = -1e30
LOG2E = math.log2(math.e)

LANES = 128
SUBLANES = 8
BF16_SUBLANES = 16
KV_COLS = N_KV_HEADS * HEAD_DIM
PAIRS_PER_GROUP = GROUP // 2

OFF_B, OFF_C, OFF_X, OFF_Q = 0, D_MODEL, 2 * D_MODEL, 3 * D_MODEL
OFF_K = 4 * D_MODEL
OFF_V = OFF_K + KV_COLS
OFF_GC = OFF_V + KV_COLS
OFF_GA = OFF_GC + D_MODEL
IN_COLS = OFF_GA + D_MODEL

TILE_MIX = 512
TILE_MLP = 1024
SUBTILE_MLP = 256
VMEM_LIMIT = 56 * 1024 * 1024

BF16 = jnp.bfloat16
F32 = jnp.float32

NT_DIMS = (((1,), (1,)), ((), ()))
TN_DIMS = (((0,), (0,)), ((), ()))


def _bucket_table():
    kj = np.arange(2 * BLOCK)[:, None]
    qi = np.arange(BLOCK)[None, :]
    dist = qi + BLOCK - kj
    n = np.maximum(dist, 0)
    max_exact = N_BUCKETS // 2
    ratio = np.log(np.maximum(n, max_exact).astype(np.float32) / max_exact) / np.log(MAX_DISTANCE / max_exact)
    large = np.minimum(max_exact + (ratio * (N_BUCKETS - max_exact)).astype(np.int32), N_BUCKETS - 1)
    bucket = np.where(n < max_exact, n, large).astype(np.int32)
    in_window = (dist >= 0) & (dist < BLOCK)
    return np.where(in_window, bucket, -1).astype(np.int32)


def _bias_kernel(bucket_ref, rel_ref, out_ref):
    head = pl.program_id(0)
    bucket = bucket_ref[...]
    acc = jnp.full(bucket.shape, NEG_INF, F32)
    for b in range(N_BUCKETS):
        acc = jnp.where(bucket == b, rel_ref[b, head], acc)
    acc = acc * LOG2E
    out_ref[0, 0] = acc
    key = lax.broadcasted_iota(jnp.int32, bucket.shape, 0)
    out_ref[1, 0] = jnp.where(key >= BLOCK, acc, NEG_INF)


def _bias_tables(rel_bias):
    bucket = jnp.asarray(_bucket_table())
    return pl.pallas_call(
        _bias_kernel,
        grid=(N_HEADS,),
        in_specs=[
            pl.BlockSpec((2 * BLOCK, BLOCK), lambda h: (0, 0)),
            pl.BlockSpec(memory_space=pltpu.SMEM),
        ],
        out_specs=pl.BlockSpec((2, 1, 2 * BLOCK, BLOCK), lambda h: (0, h, 0, 0)),
        out_shape=jax.ShapeDtypeStruct((2, N_HEADS, 2 * BLOCK, BLOCK), F32),
        name="bias_tables",
    )(bucket, rel_bias.astype(F32))


def _rmsnorm(x, g):
    y = x * lax.rsqrt(jnp.mean(x * x, axis=-1, keepdims=True) + EPS)
    return y * g


def _mixer_kernel(x_ref, g_ref, win_ref, convw_ref, wco_ref, sinks_ref, bias_ref, wao_ref, wo_ref,
                  wup_ref, wdown_ref, out_ref, wup_out_ref, wdown_out_ref, cx_buf, kv_buf, attn_buf, *, tile):
    t = pl.program_id(1)

    @pl.when(t == 0)
    def _():
        cx_buf[0:SUBLANES, :] = jnp.zeros((SUBLANES, D_MODEL), F32)
        kv_buf[0:BLOCK, :] = jnp.zeros((BLOCK, 4 * KV_COLS), BF16)

    x = x_ref[0]
    h = _rmsnorm(x, g_ref[...]).astype(BF16)

    def proj(lo, width):
        return jnp.dot(h, win_ref[:, lo:lo + width], preferred_element_type=F32)

    q = (proj(OFF_Q, D_MODEL) * (HEAD_DIM ** -0.5 * LOG2E)).astype(BF16)
    half = tile // 2
    w_kv = win_ref[:, OFF_K:OFF_K + 2 * KV_COLS]
    kv = jnp.concatenate([jnp.dot(h[0:half], w_kv, preferred_element_type=F32),
                          jnp.dot(h[half:tile], w_kv, preferred_element_type=F32)], axis=0)
    k = kv[:, 0:KV_COLS]
    val = kv[:, KV_COLS:2 * KV_COLS]
    k_swapped = pltpu.roll(k, HEAD_DIM, axis=1)
    low_half_tile = lax.broadcasted_iota(jnp.int32, (tile, LANES), 1) < HEAD_DIM
    new_rows = slice(BLOCK, BLOCK + tile)
    kv_buf[new_rows, 0:KV_COLS] = jnp.where(low_half_tile, k, k_swapped).astype(BF16)
    kv_buf[new_rows, KV_COLS:2 * KV_COLS] = jnp.where(low_half_tile, k_swapped, k).astype(BF16)
    kv_buf[new_rows, 2 * KV_COLS:3 * KV_COLS] = jnp.where(low_half_tile, val, 1.0).astype(BF16)
    kv_buf[new_rows, 3 * KV_COLS:4 * KV_COLS] = jnp.where(low_half_tile, 1.0, val).astype(BF16)

    low_half = lax.broadcasted_iota(jnp.int32, (BLOCK, LANES), 1) < HEAD_DIM
    zero = jnp.zeros((BLOCK, LANES), BF16)

    scores, probs, sink_terms = {}, {}, {}

    def qk(j):
        rows = slice(j * BLOCK, (j + 2) * BLOCK)
        qj = q[j * BLOCK:(j + 1) * BLOCK, :]
        for g in range(N_KV_HEADS):
            parts = []
            for p in range(g * PAIRS_PER_GROUP, (g + 1) * PAIRS_PER_GROUP):
                qp = qj[:, p * LANES:(p + 1) * LANES]
                parts.append(jnp.where(low_half, qp, zero))
                parts.append(jnp.where(low_half, zero, qp))
            q_heads = jnp.concatenate(parts, axis=0)
            keys = kv_buf[rows, g * KV_COLS:(g + 1) * KV_COLS]
            scores[j, g] = lax.dot_general(keys, q_heads, NT_DIMS, preferred_element_type=F32)

    def softmax(j, g):
        first = (t == 0).astype(jnp.int32) if j == 0 else 0
        s = scores.pop((j, g))
        for i in range(GROUP):
            head = g * GROUP + i
            sh = s[:, i * BLOCK:(i + 1) * BLOCK] + bias_ref[first, head]
            sink = jnp.full((1, BLOCK), sinks_ref[head], F32) * LOG2E
            m = jnp.maximum(jnp.max(sh, axis=0, keepdims=True), sink)
            probs[j, head] = jnp.exp2(sh - m).astype(BF16)
            sink_terms[j, head] = jnp.exp2(sink - m)

    def pv(j):
        rows = slice(j * BLOCK, (j + 2) * BLOCK)
        for g in range(N_KV_HEADS):
            vals = kv_buf[rows, (2 + g) * KV_COLS:(3 + g) * KV_COLS]
            p = jnp.concatenate([probs.pop((j, g * GROUP + i)) for i in range(GROUP)], axis=1)
            o = lax.dot_general(vals, p, TN_DIMS, preferred_element_type=F32)
            out_rows = slice(g * HEAD_DIM, (g + 1) * HEAD_DIM)
            sum_row = (1 - g) * HEAD_DIM
            for i in range(GROUP):
                head = g * GROUP + i
                cols = slice(i * BLOCK, (i + 1) * BLOCK)
                denom = o[sum_row:sum_row + 1, cols] + sink_terms.pop((j, head))
                oh = o[out_rows, cols] * (1.0 / denom)
                attn_buf[head * HEAD_DIM:(head + 1) * HEAD_DIM, j * BLOCK:(j + 1) * BLOCK] = oh.astype(BF16)

    v = {}

    def t_c():
        v["c"] = proj(OFF_C, D_MODEL)

    def t_x():
        cx = v.pop("c") * proj(OFF_X, D_MODEL)
        cx_buf[SUBLANES:SUBLANES + tile, :] = cx
        v["cx"] = cx

    def t_conv():
        v["conv"] = (convw_ref[0:1, :] * cx_buf[SUBLANES - 2:SUBLANES - 2 + tile, :]
                     + convw_ref[1:2, :] * cx_buf[SUBLANES - 1:SUBLANES - 1 + tile, :]
                     + convw_ref[2:3, :] * v.pop("cx"))
        cx_buf[0:SUBLANES, :] = cx_buf[tile:tile + SUBLANES, :]

    def t_b():
        v["u"] = (proj(OFF_B, D_MODEL) * v.pop("conv")).astype(BF16)

    def t_gc():
        v["gate_conv"] = jax.nn.sigmoid(proj(OFF_GC, D_MODEL))

    def t_ga():
        v["gate_attn"] = jax.nn.sigmoid(proj(OFF_GA, D_MODEL))

    def t_yconv():
        v["mixed"] = v.pop("gate_conv") * jnp.dot(v.pop("u"), wco_ref[...], preferred_element_type=F32)

    def t_yattn(r):
        rows = slice(r * half, (r + 1) * half)
        y_attn = lax.dot_general(attn_buf[:, rows], wao_ref[...], TN_DIMS, preferred_element_type=F32)
        v["mixed", r] = (v["mixed"][rows] + v["gate_attn"][rows] * y_attn).astype(BF16)

    def t_out(r):
        rows = slice(r * half, (r + 1) * half)
        out_ref[0, rows, :] = x[rows] + jnp.dot(v.pop(("mixed", r)), wo_ref[...], preferred_element_type=F32)

    qk(0)
    t_c()
    wup_out_ref[...] = wup_ref[...].astype(BF16)
    wdown_out_ref[...] = wdown_ref[...].astype(BF16)
    softmax(0, 0)
    t_x()
    softmax(0, 1)
    t_gc()
    t_conv()
    pv(0)
    qk(1)
    t_b()
    softmax(1, 0)
    t_ga()
    softmax(1, 1)
    pv(1)
    qk(2)
    t_yconv()
    softmax(2, 0)
    t_yattn(0)
    softmax(2, 1)
    pv(2)
    qk(3)
    t_out(0)
    softmax(3, 0)
    softmax(3, 1)
    pv(3)
    kv_buf[0:BLOCK, :] = kv_buf[tile:tile + BLOCK, :]
    t_yattn(1)
    t_out(1)


def _mlp_kernel(x_ref, g_ref, wup_ref, wdown_ref, gf_ref, out_ref, *, tile, sub):
    normed, act, mixed = {}, {}, {}

    def rows(i):
        return slice(i * sub, (i + 1) * sub)

    def norm(i):
        normed[i] = _rmsnorm(x_ref[rows(i), :], g_ref[...]).astype(BF16)

    def up(i):
        u = jnp.dot(normed.pop(i), wup_ref[...], preferred_element_type=F32)
        act[i] = jnp.square(jnp.maximum(u, 0.0)).astype(BF16)

    def down(i):
        mixed[i] = x_ref[rows(i), :] + jnp.dot(act.pop(i), wdown_ref[...], preferred_element_type=F32)

    def final(i):
        out_ref[rows(i), :] = _rmsnorm(mixed.pop(i), gf_ref[...])

    n = tile // sub
    norm(0)
    up(0)
    for i in range(n):
        if i + 1 < n:
            norm(i + 1)
        down(i)
        if i + 1 < n:
            up(i + 1)
        final(i)


def _resident(shape):
    return pl.BlockSpec(shape, lambda *_: (0,) * len(shape), pipeline_mode=pl.Buffered(1))


def _mixer(x, g, w_in, conv_w, w_conv_out, sinks, bias, w_attn_out, w_o, w_up, w_down):
    batch, seq, _ = x.shape
    tile = TILE_MIX
    assert seq % tile == 0 and tile % BLOCK == 0
    tiles = seq // tile
    steps = batch * tiles
    up_rows, down_rows = D_MODEL // steps, D_FF // steps
    assert up_rows * steps == D_MODEL and up_rows % BF16_SUBLANES == 0
    assert down_rows * steps == D_FF and down_rows % BF16_SUBLANES == 0
    return pl.pallas_call(
        functools.partial(_mixer_kernel, tile=tile),
        grid=(batch, seq // tile),
        in_specs=[
            pl.BlockSpec((1, tile, D_MODEL), lambda b, t: (b, t, 0)),
            _resident((1, D_MODEL)),
            _resident((D_MODEL, IN_COLS)),
            _resident((CONV_WIDTH, D_MODEL)),
            _resident((D_MODEL, D_MODEL)),
            pl.BlockSpec(memory_space=pltpu.SMEM),
            _resident((2, N_HEADS, 2 * BLOCK, BLOCK)),
            _resident((D_MODEL, D_MODEL)),
            _resident((D_MODEL, D_MODEL)),
            pl.BlockSpec((up_rows, D_FF), lambda b, t: (b * tiles + t, 0)),
            pl.BlockSpec((down_rows, D_MODEL), lambda b, t: (b * tiles + t, 0)),
        ],
        out_specs=[
            pl.BlockSpec((1, tile, D_MODEL), lambda b, t: (b, t, 0)),
            pl.BlockSpec((up_rows, D_FF), lambda b, t: (b * tiles + t, 0)),
            pl.BlockSpec((down_rows, D_MODEL), lambda b, t: (b * tiles + t, 0)),
        ],
        out_shape=[
            jax.ShapeDtypeStruct(x.shape, F32),
            jax.ShapeDtypeStruct(w_up.shape, BF16),
            jax.ShapeDtypeStruct(w_down.shape, BF16),
        ],
        scratch_shapes=[
            pltpu.VMEM((tile + 2 * SUBLANES, D_MODEL), F32),
            pltpu.VMEM((tile + BLOCK, 4 * KV_COLS), BF16),
            pltpu.VMEM((D_MODEL, tile), BF16),
        ],
        compiler_params=pltpu.CompilerParams(
            dimension_semantics=("arbitrary", "arbitrary"), vmem_limit_bytes=VMEM_LIMIT),
        name="token_mixer",
    )(x, g, w_in, conv_w, w_conv_out, sinks, bias, w_attn_out, w_o, w_up, w_down)


def _mlp(x, g, w_up, w_down, g_final):
    rows = x.shape[0]
    tile, sub = TILE_MLP, SUBTILE_MLP
    assert rows % tile == 0 and tile % sub == 0
    return pl.pallas_call(
        functools.partial(_mlp_kernel, tile=tile, sub=sub),
        grid=(rows // tile,),
        in_specs=[
            pl.BlockSpec((tile, D_MODEL), lambda i: (i, 0)),
            _resident((1, D_MODEL)),
            _resident((D_MODEL, D_FF)),
            _resident((D_FF, D_MODEL)),
            _resident((1, D_MODEL)),
        ],
        out_specs=pl.BlockSpec((tile, D_MODEL), lambda i: (i, 0)),
        out_shape=jax.ShapeDtypeStruct(x.shape, F32),
        compiler_params=pltpu.CompilerParams(
            dimension_semantics=("arbitrary",), vmem_limit_bytes=VMEM_LIMIT),
        name="channel_mixer",
    )(x, g, w_up, w_down, g_final)


def kernel(x, attn_norm_g, w_in, conv_w, w_conv_out, attn_sinks, rel_bias, w_attn_out, w_o, mlp_norm_g, w_up,
           w_down, final_norm_g):
    batch, seq, _ = x.shape
    depth = w_in.shape[0]
    assert depth == 1, "the final norm is fused into the (only) layer's channel mixer"
    bias = _bias_tables(rel_bias)
    x, w_up_bf16, w_down_bf16 = _mixer(
        x, attn_norm_g[0][None, :], w_in[0].astype(BF16), conv_w[0], w_conv_out[0].astype(BF16),
        attn_sinks[0].astype(F32), bias, w_attn_out[0].astype(BF16), w_o[0].astype(BF16), w_up[0], w_down[0])
    x = _mlp(x.reshape(batch * seq, D_MODEL), mlp_norm_g[0][None, :], w_up_bf16, w_down_bf16,
             final_norm_g[None, :])
    return x.reshape(batch, seq, D_MODEL)
```

```python
import functools
import math

import jax
import jax.numpy as jnp
import numpy as np
from jax import lax
from jax.experimental import pallas as pl
from jax.experimental.pallas import tpu as pltpu

D_MODEL = 1024
HEAD_DIM = 64
N_HEADS = 16
N_KV_HEADS = 2
GROUP = N_HEADS // N_KV_HEADS
BLOCK = 128
N_BUCKETS = 32
MAX_DISTANCE = 128
CONV_WIDTH = 3
D_FF = 4 * D_MODEL
EPS = 1e-6
NEG_INF = -1e30
LOG2E = math.log2(math.e)

LANES = 128
SUBLANES = 8
BF16_SUBLANES = 16
KV_COLS = N_KV_HEADS * HEAD_DIM
PAIRS_PER_GROUP = GROUP // 2

OFF_B, OFF_C, OFF_X, OFF_Q = 0, D_MODEL, 2 * D_MODEL, 3 * D_MODEL
OFF_K = 4 * D_MODEL
OFF_V = OFF_K + KV_COLS
OFF_GC = OFF_V + KV_COLS
OFF_GA = OFF_GC + D_MODEL
IN_COLS = OFF_GA + D_MODEL

TILE_MIX = 512
TILE_MLP = 1024
SUBTILE_MLP = 256
VMEM_LIMIT = 56 * 1024 * 1024
STAGE_COLS = 256
STAGE_SLOTS = 4

BF16 = jnp.bfloat16
F32 = jnp.float32

NT_DIMS = (((1,), (1,)), ((), ()))
TN_DIMS = (((0,), (0,)), ((), ()))


def _bucket_table():
    kj = np.arange(2 * BLOCK)[:, None]
    qi = np.arange(BLOCK)[None, :]
    dist = qi + BLOCK - kj
    n = np.maximum(dist, 0)
    max_exact = N_BUCKETS // 2
    ratio = np.log(np.maximum(n, max_exact).astype(np.float32) / max_exact) / np.log(MAX_DISTANCE / max_exact)
    large = np.minimum(max_exact + (ratio * (N_BUCKETS - max_exact)).astype(np.int32), N_BUCKETS - 1)
    bucket = np.where(n < max_exact, n, large).astype(np.int32)
    in_window = (dist >= 0) & (dist < BLOCK)
    return np.where(in_window, bucket, -1).astype(np.int32)


def _bias_kernel(bucket_ref, rel_ref, out_ref):
    head = pl.program_id(0)
    bucket = bucket_ref[...]
    acc = jnp.full(bucket.shape, NEG_INF, F32)
    for b in range(N_BUCKETS):
        acc = jnp.where(bucket == b, rel_ref[b, head], acc)
    acc = acc * LOG2E
    out_ref[0, 0] = acc
    key = lax.broadcasted_iota(jnp.int32, bucket.shape, 0)
    out_ref[1, 0] = jnp.where(key >= BLOCK, acc, NEG_INF)


def _bias_tables(rel_bias):
    bucket = jnp.asarray(_bucket_table())
    return pl.pallas_call(
        _bias_kernel,
        grid=(N_HEADS,),
        in_specs=[
            pl.BlockSpec((2 * BLOCK, BLOCK), lambda h: (0, 0)),
            pl.BlockSpec(memory_space=pltpu.SMEM),
        ],
        out_specs=pl.BlockSpec((2, 1, 2 * BLOCK, BLOCK), lambda h: (0, h, 0, 0)),
        out_shape=jax.ShapeDtypeStruct((2, N_HEADS, 2 * BLOCK, BLOCK), F32),
        name="bias_tables",
    )(bucket, rel_bias.astype(F32))


def _rmsnorm(x, g):
    y = x * lax.rsqrt(jnp.mean(x * x, axis=-1, keepdims=True) + EPS)
    return y * g


def _stage_weights(pairs, stage, sems):
    chunks = [(src, dst, c) for src, dst in pairs for c in range(0, src.shape[1], STAGE_COLS)]
    slots = stage.shape[0]

    def copy(i):
        src, _, c = chunks[i]
        return pltpu.make_async_copy(src.at[:, c:c + STAGE_COLS], stage.at[i % slots], sems.at[i % slots])

    for i in range(min(slots, len(chunks))):
        copy(i).start()
    for i, (_, dst, c) in enumerate(chunks):
        copy(i).wait()
        dst[:, c:c + STAGE_COLS] = stage[i % slots].astype(BF16)
        if i + slots < len(chunks):
            copy(i + slots).start()


def _mixer_kernel(x_ref, g_ref, win_hbm, convw_ref, wco_hbm, sinks_ref, bias_ref, wao_hbm, wo_hbm,
                  wup_ref, wdown_ref, out_ref, wup_out_ref, wdown_out_ref,
                  win_buf, wco_buf, wao_buf, wo_buf, stage, stage_sems, cx_buf, kv_buf, attn_buf, *, tile, tiles):
    s = pl.program_id(0)

    @pl.when(s == 0)
    def _():
        _stage_weights([(win_hbm, win_buf), (wco_hbm, wco_buf), (wao_hbm, wao_buf), (wo_hbm, wo_buf)],
                       stage, stage_sems)

    @pl.when(s > 0)
    def _():
        _mixer_step((s - 1) % tiles, x_ref, g_ref, win_buf, convw_ref, wco_buf, sinks_ref, bias_ref, wao_buf,
                    wo_buf, wup_ref, wdown_ref, out_ref, wup_out_ref, wdown_out_ref, cx_buf, kv_buf, attn_buf,
                    tile=tile)


def _mixer_step(t, x_ref, g_ref, win_ref, convw_ref, wco_ref, sinks_ref, bias_ref, wao_ref, wo_ref,
                wup_ref, wdown_ref, out_ref, wup_out_ref, wdown_out_ref, cx_buf, kv_buf, attn_buf, *, tile):
    @pl.when(t == 0)
    def _():
        cx_buf[0:SUBLANES, :] = jnp.zeros((SUBLANES, D_MODEL), F32)
        kv_buf[0:BLOCK, :] = jnp.zeros((BLOCK, 4 * KV_COLS), BF16)

    x = x_ref[0]
    h = _rmsnorm(x, g_ref[...]).astype(BF16)

    def proj(lo, width):
        return jnp.dot(h, win_ref[:, lo:lo + width], preferred_element_type=F32)

    q = (proj(OFF_Q, D_MODEL) * (HEAD_DIM ** -0.5 * LOG2E)).astype(BF16)
    half = tile // 2
    w_kv = win_ref[:, OFF_K:OFF_K + 2 * KV_COLS]
    kv = jnp.concatenate([jnp.dot(h[0:half], w_kv, preferred_element_type=F32),
                          jnp.dot(h[half:tile], w_kv, preferred_element_type=F32)], axis=0)
    k = kv[:, 0:KV_COLS]
    val = kv[:, KV_COLS:2 * KV_COLS]
    k_swapped = pltpu.roll(k, HEAD_DIM, axis=1)
    low_half_tile = lax.broadcasted_iota(jnp.int32, (tile, LANES), 1) < HEAD_DIM
    new_rows = slice(BLOCK, BLOCK + tile)
    kv_buf[new_rows, 0:KV_COLS] = jnp.where(low_half_tile, k, k_swapped).astype(BF16)
    kv_buf[new_rows, KV_COLS:2 * KV_COLS] = jnp.where(low_half_tile, k_swapped, k).astype(BF16)
    kv_buf[new_rows, 2 * KV_COLS:3 * KV_COLS] = jnp.where(low_half_tile, val, 1.0).astype(BF16)
    kv_buf[new_rows, 3 * KV_COLS:4 * KV_COLS] = jnp.where(low_half_tile, 1.0, val).astype(BF16)

    low_half = lax.broadcasted_iota(jnp.int32, (BLOCK, LANES), 1) < HEAD_DIM
    zero = jnp.zeros((BLOCK, LANES), BF16)

    scores, probs, sink_terms = {}, {}, {}

    def qk(j):
        rows = slice(j * BLOCK, (j + 2) * BLOCK)
        qj = q[j * BLOCK:(j + 1) * BLOCK, :]
        for g in range(N_KV_HEADS):
            parts = []
            for p in range(g * PAIRS_PER_GROUP, (g + 1) * PAIRS_PER_GROUP):
                qp = qj[:, p * LANES:(p + 1) * LANES]
                parts.append(jnp.where(low_half, qp, zero))
                parts.append(jnp.where(low_half, zero, qp))
            q_heads = jnp.concatenate(parts, axis=0)
            keys = kv_buf[rows, g * KV_COLS:(g + 1) * KV_COLS]
            scores[j, g] = lax.dot_general(keys, q_heads, NT_DIMS, preferred_element_type=F32)

    def softmax(j, g):
        first = (t == 0).astype(jnp.int32) if j == 0 else 0
        s = scores.pop((j, g))
        for i in range(GROUP):
            head = g * GROUP + i
            sh = s[:, i * BLOCK:(i + 1) * BLOCK] + bias_ref[first, head]
            sink = jnp.full((1, BLOCK), sinks_ref[head], F32) * LOG2E
            m = jnp.maximum(jnp.max(sh, axis=0, keepdims=True), sink)
            probs[j, head] = jnp.exp2(sh - m).astype(BF16)
            sink_terms[j, head] = jnp.exp2(sink - m)

    def pv(j):
        rows = slice(j * BLOCK, (j + 2) * BLOCK)
        for g in range(N_KV_HEADS):
            vals = kv_buf[rows, (2 + g) * KV_COLS:(3 + g) * KV_COLS]
            p = jnp.concatenate([probs.pop((j, g * GROUP + i)) for i in range(GROUP)], axis=1)
            o = lax.dot_general(vals, p, TN_DIMS, preferred_element_type=F32)
            out_rows = slice(g * HEAD_DIM, (g + 1) * HEAD_DIM)
            sum_row = (1 - g) * HEAD_DIM
            for i in range(GROUP):
                head = g * GROUP + i
                cols = slice(i * BLOCK, (i + 1) * BLOCK)
                denom = o[sum_row:sum_row + 1, cols] + sink_terms.pop((j, head))
                oh = o[out_rows, cols] * (1.0 / denom)
                attn_buf[head * HEAD_DIM:(head + 1) * HEAD_DIM, j * BLOCK:(j + 1) * BLOCK] = oh.astype(BF16)

    v = {}

    def t_c():
        v["c"] = proj(OFF_C, D_MODEL)

    def t_x():
        cx = v.pop("c") * proj(OFF_X, D_MODEL)
        cx_buf[SUBLANES:SUBLANES + tile, :] = cx
        v["cx"] = cx

    def t_conv():
        v["conv"] = (convw_ref[0:1, :] * cx_buf[SUBLANES - 2:SUBLANES - 2 + tile, :]
                     + convw_ref[1:2, :] * cx_buf[SUBLANES - 1:SUBLANES - 1 + tile, :]
                     + convw_ref[2:3, :] * v.pop("cx"))
        cx_buf[0:SUBLANES, :] = cx_buf[tile:tile + SUBLANES, :]

    def t_b():
        v["u"] = (proj(OFF_B, D_MODEL) * v.pop("conv")).astype(BF16)

    def t_gc():
        v["gate_conv"] = jax.nn.sigmoid(proj(OFF_GC, D_MODEL))

    def t_ga():
        v["gate_attn"] = jax.nn.sigmoid(proj(OFF_GA, D_MODEL))

    def t_yconv():
        v["mixed"] = v.pop("gate_conv") * jnp.dot(v.pop("u"), wco_ref[...], preferred_element_type=F32)

    def t_yattn(r):
        rows = slice(r * half, (r + 1) * half)
        y_attn = lax.dot_general(attn_buf[:, rows], wao_ref[...], TN_DIMS, preferred_element_type=F32)
        v["mixed", r] = (v["mixed"][rows] + v["gate_attn"][rows] * y_attn).astype(BF16)

    def t_out(r):
        rows = slice(r * half, (r + 1) * half)
        out_ref[0, rows, :] = x[rows] + jnp.dot(v.pop(("mixed", r)), wo_ref[...], preferred_element_type=F32)

    qk(0)
    t_c()
    wup_out_ref[...] = wup_ref[...].astype(BF16)
    wdown_out_ref[...] = wdown_ref[...].astype(BF16)
    softmax(0, 0)
    t_x()
    softmax(0, 1)
    t_gc()
    t_conv()
    pv(0)
    qk(1)
    t_b()
    softmax(1, 0)
    t_ga()
    softmax(1, 1)
    pv(1)
    qk(2)
    t_yconv()
    softmax(2, 0)
    t_yattn(0)
    softmax(2, 1)
    pv(2)
    qk(3)
    t_out(0)
    softmax(3, 0)
    softmax(3, 1)
    pv(3)
    kv_buf[0:BLOCK, :] = kv_buf[tile:tile + BLOCK, :]
    t_yattn(1)
    t_out(1)


def _mlp_kernel(x_ref, g_ref, wup_ref, wdown_ref, gf_ref, out_ref, *, tile, sub):
    normed, act, mixed = {}, {}, {}

    def rows(i):
        return slice(i * sub, (i + 1) * sub)

    def norm(i):
        normed[i] = _rmsnorm(x_ref[rows(i), :], g_ref[...]).astype(BF16)

    def up(i):
        u = jnp.dot(normed.pop(i), wup_ref[...], preferred_element_type=F32)
        act[i] = jnp.square(jnp.maximum(u, 0.0)).astype(BF16)

    def down(i):
        mixed[i] = x_ref[rows(i), :] + jnp.dot(act.pop(i), wdown_ref[...], preferred_element_type=F32)

    def final(i):
        out_ref[rows(i), :] = _rmsnorm(mixed.pop(i), gf_ref[...])

    n = tile // sub
    norm(0)
    up(0)
    for i in range(n):
        if i + 1 < n:
            norm(i + 1)
        down(i)
        if i + 1 < n:
            up(i + 1)
        final(i)


def _resident(shape):
    return pl.BlockSpec(shape, lambda *_: (0,) * len(shape), pipeline_mode=pl.Buffered(1))


def _mixer(x, g, w_in, conv_w, w_conv_out, sinks, bias, w_attn_out, w_o, w_up, w_down):
    batch, seq, _ = x.shape
    tile = TILE_MIX
    assert seq % tile == 0 and tile % BLOCK == 0
    tiles = seq // tile
    steps = batch * tiles
    up_rows, down_rows = D_MODEL // steps, D_FF // steps
    assert up_rows * steps == D_MODEL and up_rows % BF16_SUBLANES == 0
    assert down_rows * steps == D_FF and down_rows % BF16_SUBLANES == 0
    assert IN_COLS % STAGE_COLS == 0 and D_MODEL % STAGE_COLS == 0

    def tile_of(s):
        return jnp.maximum(s - 1, 0)

    def x_map(s):
        return (tile_of(s) // tiles, tile_of(s) % tiles, 0)

    def slice_map(s):
        return (tile_of(s), 0)

    hbm = pl.BlockSpec(memory_space=pl.ANY)
    return pl.pallas_call(
        functools.partial(_mixer_kernel, tile=tile, tiles=tiles),
        grid=(steps + 1,),
        in_specs=[
            pl.BlockSpec((1, tile, D_MODEL), x_map),
            _resident((1, D_MODEL)),
            hbm,
            _resident((CONV_WIDTH, D_MODEL)),
            hbm,
            pl.BlockSpec(memory_space=pltpu.SMEM),
            _resident((2, N_HEADS, 2 * BLOCK, BLOCK)),
            hbm,
            hbm,
            pl.BlockSpec((up_rows, D_FF), slice_map),
            pl.BlockSpec((down_rows, D_MODEL), slice_map),
        ],
        out_specs=[
            pl.BlockSpec((1, tile, D_MODEL), x_map),
            pl.BlockSpec((up_rows, D_FF), slice_map),
            pl.BlockSpec((down_rows, D_MODEL), slice_map),
        ],
        out_shape=[
            jax.ShapeDtypeStruct(x.shape, F32),
            jax.ShapeDtypeStruct(w_up.shape, BF16),
            jax.ShapeDtypeStruct(w_down.shape, BF16),
        ],
        scratch_shapes=[
            pltpu.VMEM((D_MODEL, IN_COLS), BF16),
            pltpu.VMEM((D_MODEL, D_MODEL), BF16),
            pltpu.VMEM((D_MODEL, D_MODEL), BF16),
            pltpu.VMEM((D_MODEL, D_MODEL), BF16),
            pltpu.VMEM((STAGE_SLOTS, D_MODEL, STAGE_COLS), F32),
            pltpu.SemaphoreType.DMA((STAGE_SLOTS,)),
            pltpu.VMEM((tile + 2 * SUBLANES, D_MODEL), F32),
            pltpu.VMEM((tile + BLOCK, 4 * KV_COLS), BF16),
            pltpu.VMEM((D_MODEL, tile), BF16),
        ],
        compiler_params=pltpu.CompilerParams(
            dimension_semantics=("arbitrary",), vmem_limit_bytes=VMEM_LIMIT),
        name="token_mixer",
    )(x, g, w_in, conv_w, w_conv_out, sinks, bias, w_attn_out, w_o, w_up, w_down)


def _mlp(x, g, w_up, w_down, g_final):
    rows = x.shape[0]
    tile, sub = TILE_MLP, SUBTILE_MLP
    assert rows % tile == 0 and tile % sub == 0
    return pl.pallas_call(
        functools.partial(_mlp_kernel, tile=tile, sub=sub),
        grid=(rows // tile,),
        in_specs=[
            pl.BlockSpec((tile, D_MODEL), lambda i: (i, 0)),
            _resident((1, D_MODEL)),
            _resident((D_MODEL, D_FF)),
            _resident((D_FF, D_MODEL)),
            _resident((1, D_MODEL)),
        ],
        out_specs=pl.BlockSpec((tile, D_MODEL), lambda i: (i, 0)),
        out_shape=jax.ShapeDtypeStruct(x.shape, F32),
        compiler_params=pltpu.CompilerParams(
            dimension_semantics=("arbitrary",), vmem_limit_bytes=VMEM_LIMIT),
        name="channel_mixer",
    )(x, g, w_up, w_down, g_final)


def kernel(x, attn_norm_g, w_in, conv_w, w_conv_out, attn_sinks, rel_bias, w_attn_out, w_o, mlp_norm_g, w_up,
           w_down, final_norm_g):
    batch, seq, _ = x.shape
    depth = w_in.shape[0]
    assert depth == 1, "the final norm is fused into the (only) layer's channel mixer"
    bias = _bias_tables(rel_bias)
    x, w_up_bf16, w_down_bf16 = _mixer(
        x, attn_norm_g[0][None, :], w_in[0], conv_w[0], w_conv_out[0], attn_sinks[0].astype(F32), bias,
        w_attn_out[0], w_o[0], w_up[0], w_down[0])
    x = _mlp(x.reshape(batch * seq, D_MODEL), mlp_norm_g[0][None, :], w_up_bf16, w_down_bf16,
             final_norm_g[None, :])
    return x.reshape(batch, seq, D_MODEL)
```

```python
import functools
import math

import jax
import jax.numpy as jnp
import numpy as np
from jax import lax
from jax.experimental import pallas as pl
from jax.experimental.pallas import tpu as pltpu

D_MODEL = 1024
HEAD_DIM = 64
N_HEADS = 16
N_KV_HEADS = 2
GROUP = N_HEADS // N_KV_HEADS
BLOCK = 128
N_BUCKETS = 32
MAX_DISTANCE = 128
CONV_WIDTH = 3
D_FF = 4 * D_MODEL
EPS = 1e-6
NEG_INF = -1e30
LOG2E = math.log2(math.e)

LANES = 128
SUBLANES = 8
BF16_SUBLANES = 16
KV_COLS = N_KV_HEADS * HEAD_DIM
PAIRS_PER_GROUP = GROUP // 2

OFF_B, OFF_C, OFF_X, OFF_Q = 0, D_MODEL, 2 * D_MODEL, 3 * D_MODEL
OFF_K = 4 * D_MODEL
OFF_V = OFF_K + KV_COLS
OFF_GC = OFF_V + KV_COLS
OFF_GA = OFF_GC + D_MODEL
IN_COLS = OFF_GA + D_MODEL

TILE_MIX = 512
TILE_MLP = 1024
SUBTILE_MLP = 256
VMEM_LIMIT = 56 * 1024 * 1024
STAGE_COLS = 256
STAGE_SLOTS = 4

BF16 = jnp.bfloat16
F32 = jnp.float32

NT_DIMS = (((1,), (1,)), ((), ()))
TN_DIMS = (((0,), (0,)), ((), ()))


def _bucket_table():
    kj = np.arange(2 * BLOCK)[:, None]
    qi = np.arange(BLOCK)[None, :]
    dist = qi + BLOCK - kj
    n = np.maximum(dist, 0)
    max_exact = N_BUCKETS // 2
    ratio = np.log(np.maximum(n, max_exact).astype(np.float32) / max_exact) / np.log(MAX_DISTANCE / max_exact)
    large = np.minimum(max_exact + (ratio * (N_BUCKETS - max_exact)).astype(np.int32), N_BUCKETS - 1)
    bucket = np.where(n < max_exact, n, large).astype(np.int32)
    in_window = (dist >= 0) & (dist < BLOCK)
    return np.where(in_window, bucket, -1).astype(np.int32)


def _bias_table(head, bucket_ref, rel_ref, bias_buf):
    bucket = bucket_ref[...]
    acc = jnp.full(bucket.shape, NEG_INF, F32)
    for b in range(N_BUCKETS):
        acc = jnp.where(bucket == b, rel_ref[b, head], acc)
    acc = acc * LOG2E
    bias_buf[0, head] = acc
    key = lax.broadcasted_iota(jnp.int32, bucket.shape, 0)
    bias_buf[1, head] = jnp.where(key >= BLOCK, acc, NEG_INF)


def _rmsnorm(x, g):
    y = x * lax.rsqrt(jnp.mean(x * x, axis=-1, keepdims=True) + EPS)
    return y * g


def _stage_weights(pairs, stage, sems, fillers):
    chunks = [(src, dst, c) for src, dst in pairs for c in range(0, src.shape[1], STAGE_COLS)]
    slots = stage.shape[0]

    def copy(i):
        src, _, c = chunks[i]
        return pltpu.make_async_copy(src.at[:, c:c + STAGE_COLS], stage.at[i % slots], sems.at[i % slots])

    for i in range(min(slots, len(chunks))):
        copy(i).start()
    fillers = list(fillers)
    assert len(fillers) <= len(chunks)
    for i, (_, dst, c) in enumerate(chunks):
        if fillers:
            fillers.pop(0)()
        copy(i).wait()
        dst[:, c:c + STAGE_COLS] = stage[i % slots].astype(BF16)
        if i + slots < len(chunks):
            copy(i + slots).start()


def _mixer_kernel(x_ref, g_ref, win_hbm, convw_ref, wco_hbm, sinks_ref, bucket_ref, rel_ref, wao_hbm, wo_hbm,
                  wup_ref, wdown_ref, out_ref, wup_out_ref, wdown_out_ref,
                  win_buf, wco_buf, wao_buf, wo_buf, stage, stage_sems, bias_buf, cx_buf, kv_buf, attn_buf,
                  *, tile, tiles):
    s = pl.program_id(0)

    @pl.when(s == 0)
    def _():
        bias_tasks = [functools.partial(_bias_table, head, bucket_ref, rel_ref, bias_buf)
                      for head in range(N_HEADS)]
        _stage_weights([(win_hbm, win_buf), (wco_hbm, wco_buf), (wao_hbm, wao_buf), (wo_hbm, wo_buf)],
                       stage, stage_sems, bias_tasks)

    @pl.when(s > 0)
    def _():
        _mixer_step((s - 1) % tiles, x_ref, g_ref, win_buf, convw_ref, wco_buf, sinks_ref, bias_buf, wao_buf,
                    wo_buf, wup_ref, wdown_ref, out_ref, wup_out_ref, wdown_out_ref, cx_buf, kv_buf, attn_buf,
                    tile=tile)


def _mixer_step(t, x_ref, g_ref, win_ref, convw_ref, wco_ref, sinks_ref, bias_ref, wao_ref, wo_ref,
                wup_ref, wdown_ref, out_ref, wup_out_ref, wdown_out_ref, cx_buf, kv_buf, attn_buf, *, tile):
    @pl.when(t == 0)
    def _():
        cx_buf[0:SUBLANES, :] = jnp.zeros((SUBLANES, D_MODEL), F32)
        kv_buf[0:BLOCK, :] = jnp.zeros((BLOCK, 4 * KV_COLS), BF16)

    x = x_ref[0]
    h = _rmsnorm(x, g_ref[...]).astype(BF16)

    def proj(lo, width):
        return jnp.dot(h, win_ref[:, lo:lo + width], preferred_element_type=F32)

    q = (proj(OFF_Q, D_MODEL) * (HEAD_DIM ** -0.5 * LOG2E)).astype(BF16)
    half = tile // 2
    w_kv = win_ref[:, OFF_K:OFF_K + 2 * KV_COLS]
    kv = jnp.concatenate([jnp.dot(h[0:half], w_kv, preferred_element_type=F32),
                          jnp.dot(h[half:tile], w_kv, preferred_element_type=F32)], axis=0)
    k = kv[:, 0:KV_COLS]
    val = kv[:, KV_COLS:2 * KV_COLS]
    k_swapped = pltpu.roll(k, HEAD_DIM, axis=1)
    low_half_tile = lax.broadcasted_iota(jnp.int32, (tile, LANES), 1) < HEAD_DIM
    new_rows = slice(BLOCK, BLOCK + tile)
    kv_buf[new_rows, 0:KV_COLS] = jnp.where(low_half_tile, k, k_swapped).astype(BF16)
    kv_buf[new_rows, KV_COLS:2 * KV_COLS] = jnp.where(low_half_tile, k_swapped, k).astype(BF16)
    kv_buf[new_rows, 2 * KV_COLS:3 * KV_COLS] = jnp.where(low_half_tile, val, 1.0).astype(BF16)
    kv_buf[new_rows, 3 * KV_COLS:4 * KV_COLS] = jnp.where(low_half_tile, 1.0, val).astype(BF16)

    low_half = lax.broadcasted_iota(jnp.int32, (BLOCK, LANES), 1) < HEAD_DIM
    zero = jnp.zeros((BLOCK, LANES), BF16)

    scores, probs, sink_terms = {}, {}, {}

    def qk(j):
        rows = slice(j * BLOCK, (j + 2) * BLOCK)
        qj = q[j * BLOCK:(j + 1) * BLOCK, :]
        for g in range(N_KV_HEADS):
            parts = []
            for p in range(g * PAIRS_PER_GROUP, (g + 1) * PAIRS_PER_GROUP):
                qp = qj[:, p * LANES:(p + 1) * LANES]
                parts.append(jnp.where(low_half, qp, zero))
                parts.append(jnp.where(low_half, zero, qp))
            q_heads = jnp.concatenate(parts, axis=0)
            keys = kv_buf[rows, g * KV_COLS:(g + 1) * KV_COLS]
            scores[j, g] = lax.dot_general(keys, q_heads, NT_DIMS, preferred_element_type=F32)

    def softmax(j, g):
        first = (t == 0).astype(jnp.int32) if j == 0 else 0
        s = scores.pop((j, g))
        for i in range(GROUP):
            head = g * GROUP + i
            sh = s[:, i * BLOCK:(i + 1) * BLOCK] + bias_ref[first, head]
            sink = jnp.full((1, BLOCK), sinks_ref[head], F32) * LOG2E
            m = jnp.maximum(jnp.max(sh, axis=0, keepdims=True), sink)
            probs[j, head] = jnp.exp2(sh - m).astype(BF16)
            sink_terms[j, head] = jnp.exp2(sink - m)

    def pv(j):
        rows = slice(j * BLOCK, (j + 2) * BLOCK)
        for g in range(N_KV_HEADS):
            vals = kv_buf[rows, (2 + g) * KV_COLS:(3 + g) * KV_COLS]
            p = jnp.concatenate([probs.pop((j, g * GROUP + i)) for i in range(GROUP)], axis=1)
            o = lax.dot_general(vals, p, TN_DIMS, preferred_element_type=F32)
            out_rows = slice(g * HEAD_DIM, (g + 1) * HEAD_DIM)
            sum_row = (1 - g) * HEAD_DIM
            for i in range(GROUP):
                head = g * GROUP + i
                cols = slice(i * BLOCK, (i + 1) * BLOCK)
                denom = o[sum_row:sum_row + 1, cols] + sink_terms.pop((j, head))
                oh = o[out_rows, cols] * (1.0 / denom)
                attn_buf[head * HEAD_DIM:(head + 1) * HEAD_DIM, j * BLOCK:(j + 1) * BLOCK] = oh.astype(BF16)

    v = {}

    def t_c():
        v["c"] = proj(OFF_C, D_MODEL)

    def t_x():
        cx = v.pop("c") * proj(OFF_X, D_MODEL)
        cx_buf[SUBLANES:SUBLANES + tile, :] = cx
        v["cx"] = cx

    def t_conv():
        v["conv"] = (convw_ref[0:1, :] * cx_buf[SUBLANES - 2:SUBLANES - 2 + tile, :]
                     + convw_ref[1:2, :] * cx_buf[SUBLANES - 1:SUBLANES - 1 + tile, :]
                     + convw_ref[2:3, :] * v.pop("cx"))
        cx_buf[0:SUBLANES, :] = cx_buf[tile:tile + SUBLANES, :]

    def t_b():
        v["u"] = (proj(OFF_B, D_MODEL) * v.pop("conv")).astype(BF16)

    def t_gc():
        v["gate_conv"] = jax.nn.sigmoid(proj(OFF_GC, D_MODEL))

    def t_ga():
        v["gate_attn"] = jax.nn.sigmoid(proj(OFF_GA, D_MODEL))

    def t_yconv():
        v["mixed"] = v.pop("gate_conv") * jnp.dot(v.pop("u"), wco_ref[...], preferred_element_type=F32)

    def t_yattn(r):
        rows = slice(r * half, (r + 1) * half)
        y_attn = lax.dot_general(attn_buf[:, rows], wao_ref[...], TN_DIMS, preferred_element_type=F32)
        v["mixed", r] = (v["mixed"][rows] + v["gate_attn"][rows] * y_attn).astype(BF16)

    def t_out(r):
        rows = slice(r * half, (r + 1) * half)
        out_ref[0, rows, :] = x[rows] + jnp.dot(v.pop(("mixed", r)), wo_ref[...], preferred_element_type=F32)

    qk(0)
    t_c()
    wup_out_ref[...] = wup_ref[...].astype(BF16)
    wdown_out_ref[...] = wdown_ref[...].astype(BF16)
    softmax(0, 0)
    t_x()
    softmax(0, 1)
    t_gc()
    t_conv()
    pv(0)
    qk(1)
    t_b()
    softmax(1, 0)
    t_ga()
    softmax(1, 1)
    pv(1)
    qk(2)
    t_yconv()
    softmax(2, 0)
    t_yattn(0)
    softmax(2, 1)
    pv(2)
    qk(3)
    t_out(0)
    softmax(3, 0)
    softmax(3, 1)
    pv(3)
    kv_buf[0:BLOCK, :] = kv_buf[tile:tile + BLOCK, :]
    t_yattn(1)
    t_out(1)


def _mlp_kernel(x_ref, g_ref, wup_ref, wdown_ref, gf_ref, out_ref, *, tile, sub):
    normed, act, mixed = {}, {}, {}

    def rows(i):
        return slice(i * sub, (i + 1) * sub)

    def norm(i):
        normed[i] = _rmsnorm(x_ref[rows(i), :], g_ref[...]).astype(BF16)

    def up(i):
        u = jnp.dot(normed.pop(i), wup_ref[...], preferred_element_type=F32)
        act[i] = jnp.square(jnp.maximum(u, 0.0)).astype(BF16)

    def down(i):
        mixed[i] = x_ref[rows(i), :] + jnp.dot(act.pop(i), wdown_ref[...], preferred_element_type=F32)

    def final(i):
        out_ref[rows(i), :] = _rmsnorm(mixed.pop(i), gf_ref[...])

    n = tile // sub
    norm(0)
    up(0)
    for i in range(n):
        if i + 1 < n:
            norm(i + 1)
        down(i)
        if i + 1 < n:
            up(i + 1)
        final(i)


def _resident(shape):
    return pl.BlockSpec(shape, lambda *_: (0,) * len(shape), pipeline_mode=pl.Buffered(1))


def _mixer(x, g, w_in, conv_w, w_conv_out, sinks, rel_bias, w_attn_out, w_o, w_up, w_down):
    batch, seq, _ = x.shape
    tile = TILE_MIX
    assert seq % tile == 0 and tile % BLOCK == 0
    tiles = seq // tile
    steps = batch * tiles
    up_rows, down_rows = D_MODEL // steps, D_FF // steps
    assert up_rows * steps == D_MODEL and up_rows % BF16_SUBLANES == 0
    assert down_rows * steps == D_FF and down_rows % BF16_SUBLANES == 0
    assert IN_COLS % STAGE_COLS == 0 and D_MODEL % STAGE_COLS == 0

    def tile_of(s):
        return jnp.maximum(s - 1, 0)

    def x_map(s):
        return (tile_of(s) // tiles, tile_of(s) % tiles, 0)

    def slice_map(s):
        return (tile_of(s), 0)

    hbm = pl.BlockSpec(memory_space=pl.ANY)
    return pl.pallas_call(
        functools.partial(_mixer_kernel, tile=tile, tiles=tiles),
        grid=(steps + 1,),
        in_specs=[
            pl.BlockSpec((1, tile, D_MODEL), x_map),
            _resident((1, D_MODEL)),
            hbm,
            _resident((CONV_WIDTH, D_MODEL)),
            hbm,
            pl.BlockSpec(memory_space=pltpu.SMEM),
            _resident((2 * BLOCK, BLOCK)),
            pl.BlockSpec(memory_space=pltpu.SMEM),
            hbm,
            hbm,
            pl.BlockSpec((up_rows, D_FF), slice_map),
            pl.BlockSpec((down_rows, D_MODEL), slice_map),
        ],
        out_specs=[
            pl.BlockSpec((1, tile, D_MODEL), x_map),
            pl.BlockSpec((up_rows, D_FF), slice_map),
            pl.BlockSpec((down_rows, D_MODEL), slice_map),
        ],
        out_shape=[
            jax.ShapeDtypeStruct(x.shape, F32),
            jax.ShapeDtypeStruct(w_up.shape, BF16),
            jax.ShapeDtypeStruct(w_down.shape, BF16),
        ],
        scratch_shapes=[
            pltpu.VMEM((D_MODEL, IN_COLS), BF16),
            pltpu.VMEM((D_MODEL, D_MODEL), BF16),
            pltpu.VMEM((D_MODEL, D_MODEL), BF16),
            pltpu.VMEM((D_MODEL, D_MODEL), BF16),
            pltpu.VMEM((STAGE_SLOTS, D_MODEL, STAGE_COLS), F32),
            pltpu.SemaphoreType.DMA((STAGE_SLOTS,)),
            pltpu.VMEM((2, N_HEADS, 2 * BLOCK, BLOCK), F32),
            pltpu.VMEM((tile + 2 * SUBLANES, D_MODEL), F32),
            pltpu.VMEM((tile + BLOCK, 4 * KV_COLS), BF16),
            pltpu.VMEM((D_MODEL, tile), BF16),
        ],
        compiler_params=pltpu.CompilerParams(
            dimension_semantics=("arbitrary",), vmem_limit_bytes=VMEM_LIMIT),
        name="token_mixer",
    )(x, g, w_in, conv_w, w_conv_out, sinks, jnp.asarray(_bucket_table()), rel_bias, w_attn_out, w_o, w_up, w_down)


def _mlp(x, g, w_up, w_down, g_final):
    rows = x.shape[0]
    tile, sub = TILE_MLP, SUBTILE_MLP
    assert rows % tile == 0 and tile % sub == 0
    return pl.pallas_call(
        functools.partial(_mlp_kernel, tile=tile, sub=sub),
        grid=(rows // tile,),
        in_specs=[
            pl.BlockSpec((tile, D_MODEL), lambda i: (i, 0)),
            _resident((1, D_MODEL)),
            _resident((D_MODEL, D_FF)),
            _resident((D_FF, D_MODEL)),
            _resident((1, D_MODEL)),
        ],
        out_specs=pl.BlockSpec((tile, D_MODEL), lambda i: (i, 0)),
        out_shape=jax.ShapeDtypeStruct(x.shape, F32),
        compiler_params=pltpu.CompilerParams(
            dimension_semantics=("arbitrary",), vmem_limit_bytes=VMEM_LIMIT),
        name="channel_mixer",
    )(x, g, w_up, w_down, g_final)


def kernel(x, attn_norm_g, w_in, conv_w, w_conv_out, attn_sinks, rel_bias, w_attn_out, w_o, mlp_norm_g, w_up,
           w_down, final_norm_g):
    batch, seq, _ = x.shape
    depth = w_in.shape[0]
    assert depth == 1, "the final norm is fused into the (only) layer's channel mixer"
    x, w_up_bf16, w_down_bf16 = _mixer(
        x, attn_norm_g[0][None, :], w_in[0], conv_w[0], w_conv_out[0], attn_sinks[0].astype(F32),
        rel_bias.astype(F32), w_attn_out[0], w_o[0], w_up[0], w_down[0])
    x = _mlp(x.reshape(batch * seq, D_MODEL), mlp_norm_g[0][None, :], w_up_bf16, w_down_bf16,
             final_norm_g[None, :])
    return x.reshape(batch, seq, D_MODEL)
```

```python
import functools
import math

import jax
import jax.numpy as jnp
import numpy as np
from jax import lax
from jax.experimental import pallas as pl
from jax.experimental.pallas import tpu as pltpu

D_MODEL = 1024
HEAD_DIM = 64
N_HEADS = 16
N_KV_HEADS = 2
GROUP = N_HEADS // N_KV_HEADS
BLOCK = 128
N_BUCKETS = 32
MAX_DISTANCE = 128
CONV_WIDTH = 3
D_FF = 4 * D_MODEL
EPS = 1e-6
NEG_INF = -1e30
LOG2E = math.log2(math.e)

LANES = 128
SUBLANES = 8
BF16_SUBLANES = 16
KV_COLS = N_KV_HEADS * HEAD_DIM
PAIRS_PER_GROUP = GROUP // 2

OFF_B, OFF_C, OFF_X, OFF_Q = 0, D_MODEL, 2 * D_MODEL, 3 * D_MODEL
OFF_K = 4 * D_MODEL
OFF_V = OFF_K + KV_COLS
OFF_GC = OFF_V + KV_COLS
OFF_GA = OFF_GC + D_MODEL
IN_COLS = OFF_GA + D_MODEL

TILE_MIX = 512
SUBTILES_MLP = (256, 256, 256, 256)
VMEM_LIMIT = 56 * 1024 * 1024
STAGE_COLS = 256
STAGE_SLOTS = 4

BF16 = jnp.bfloat16
F32 = jnp.float32

NT_DIMS = (((1,), (1,)), ((), ()))
TN_DIMS = (((0,), (0,)), ((), ()))


def _bucket_table():
    kj = np.arange(2 * BLOCK)[:, None]
    qi = np.arange(BLOCK)[None, :]
    dist = qi + BLOCK - kj
    n = np.maximum(dist, 0)
    max_exact = N_BUCKETS // 2
    ratio = np.log(np.maximum(n, max_exact).astype(np.float32) / max_exact) / np.log(MAX_DISTANCE / max_exact)
    large = np.minimum(max_exact + (ratio * (N_BUCKETS - max_exact)).astype(np.int32), N_BUCKETS - 1)
    bucket = np.where(n < max_exact, n, large).astype(np.int32)
    in_window = (dist >= 0) & (dist < BLOCK)
    return np.where(in_window, bucket, -1).astype(np.int32)


def _bias_table(head, bucket_ref, rel_ref, bias_buf):
    bucket = bucket_ref[...]
    acc = jnp.full(bucket.shape, NEG_INF, F32)
    for b in range(N_BUCKETS):
        acc = jnp.where(bucket == b, rel_ref[b, head], acc)
    acc = acc * LOG2E
    bias_buf[0, head] = acc
    key = lax.broadcasted_iota(jnp.int32, bucket.shape, 0)
    bias_buf[1, head] = jnp.where(key >= BLOCK, acc, NEG_INF)


def _rmsnorm(x, g):
    y = x * lax.rsqrt(jnp.mean(x * x, axis=-1, keepdims=True) + EPS)
    return y * g


def _stage_weights(pairs, stage, sems, fillers):
    chunks = [(src, dst, c) for src, dst in pairs for c in range(0, src.shape[1], STAGE_COLS)]
    slots = stage.shape[0]

    def copy(i):
        src, _, c = chunks[i]
        return pltpu.make_async_copy(src.at[:, c:c + STAGE_COLS], stage.at[i % slots], sems.at[i % slots])

    for i in range(min(slots, len(chunks))):
        copy(i).start()
    fillers = list(fillers)
    assert len(fillers) <= len(chunks)
    for i, (_, dst, c) in enumerate(chunks):
        if fillers:
            fillers.pop(0)()
        copy(i).wait()
        dst[:, c:c + STAGE_COLS] = stage[i % slots].astype(BF16)
        if i + slots < len(chunks):
            copy(i + slots).start()


def _mixer_kernel(x_ref, g_ref, win_hbm, convw_ref, wco_hbm, sinks_ref, bucket_ref, rel_ref, wao_hbm, wo_hbm,
                  wup_ref, wdown_ref, out_ref, wup_out_ref, wdown_out_ref,
                  win_buf, wco_buf, wao_buf, wo_buf, stage, stage_sems, bias_buf, cx_buf, kv_buf, attn_buf,
                  *, tile, tiles):
    s = pl.program_id(0)

    @pl.when(s == 0)
    def _():
        bias_tasks = [functools.partial(_bias_table, head, bucket_ref, rel_ref, bias_buf)
                      for head in range(N_HEADS)]
        _stage_weights([(win_hbm, win_buf), (wco_hbm, wco_buf), (wao_hbm, wao_buf), (wo_hbm, wo_buf)],
                       stage, stage_sems, bias_tasks)

    @pl.when(s > 0)
    def _():
        _mixer_step((s - 1) % tiles, x_ref, g_ref, win_buf, convw_ref, wco_buf, sinks_ref, bias_buf, wao_buf,
                    wo_buf, wup_ref, wdown_ref, out_ref, wup_out_ref, wdown_out_ref, cx_buf, kv_buf, attn_buf,
                    tile=tile)


def _mixer_step(t, x_ref, g_ref, win_ref, convw_ref, wco_ref, sinks_ref, bias_ref, wao_ref, wo_ref,
                wup_ref, wdown_ref, out_ref, wup_out_ref, wdown_out_ref, cx_buf, kv_buf, attn_buf, *, tile):
    @pl.when(t == 0)
    def _():
        cx_buf[0:SUBLANES, :] = jnp.zeros((SUBLANES, D_MODEL), F32)
        kv_buf[0:BLOCK, :] = jnp.zeros((BLOCK, 4 * KV_COLS), BF16)

    x = x_ref[0]
    half = tile // 2
    h_halves = [_rmsnorm(x[r * half:(r + 1) * half], g_ref[...]).astype(BF16) for r in range(2)]
    h = jnp.concatenate(h_halves, axis=0)

    def proj(lo, width):
        return jnp.dot(h, win_ref[:, lo:lo + width], preferred_element_type=F32)

    def proj_halves(lo, width):
        w = win_ref[:, lo:lo + width]
        return jnp.concatenate([jnp.dot(hr, w, preferred_element_type=F32) for hr in h_halves], axis=0)

    q = (proj_halves(OFF_Q, D_MODEL) * (HEAD_DIM ** -0.5 * LOG2E)).astype(BF16)
    kv = proj_halves(OFF_K, 2 * KV_COLS)
    k = kv[:, 0:KV_COLS]
    val = kv[:, KV_COLS:2 * KV_COLS]
    k_swapped = pltpu.roll(k, HEAD_DIM, axis=1)
    low_half_tile = lax.broadcasted_iota(jnp.int32, (tile, LANES), 1) < HEAD_DIM
    new_rows = slice(BLOCK, BLOCK + tile)
    kv_buf[new_rows, 0:KV_COLS] = jnp.where(low_half_tile, k, k_swapped).astype(BF16)
    kv_buf[new_rows, KV_COLS:2 * KV_COLS] = jnp.where(low_half_tile, k_swapped, k).astype(BF16)
    kv_buf[new_rows, 2 * KV_COLS:3 * KV_COLS] = jnp.where(low_half_tile, val, 1.0).astype(BF16)
    kv_buf[new_rows, 3 * KV_COLS:4 * KV_COLS] = jnp.where(low_half_tile, 1.0, val).astype(BF16)

    low_half = lax.broadcasted_iota(jnp.int32, (BLOCK, LANES), 1) < HEAD_DIM
    zero = jnp.zeros((BLOCK, LANES), BF16)

    scores, probs, sink_terms = {}, {}, {}

    def qk(j):
        rows = slice(j * BLOCK, (j + 2) * BLOCK)
        qj = q[j * BLOCK:(j + 1) * BLOCK, :]
        for g in range(N_KV_HEADS):
            parts = []
            for p in range(g * PAIRS_PER_GROUP, (g + 1) * PAIRS_PER_GROUP):
                qp = qj[:, p * LANES:(p + 1) * LANES]
                parts.append(jnp.where(low_half, qp, zero))
                parts.append(jnp.where(low_half, zero, qp))
            q_heads = jnp.concatenate(parts, axis=0)
            keys = kv_buf[rows, g * KV_COLS:(g + 1) * KV_COLS]
            scores[j, g] = lax.dot_general(keys, q_heads, NT_DIMS, preferred_element_type=F32)

    def softmax(j, g):
        first = (t == 0).astype(jnp.int32) if j == 0 else 0
        s = scores.pop((j, g))
        for i in range(GROUP):
            head = g * GROUP + i
            sh = s[:, i * BLOCK:(i + 1) * BLOCK] + bias_ref[first, head]
            sink = jnp.full((1, BLOCK), sinks_ref[head], F32) * LOG2E
            m = jnp.maximum(jnp.max(sh, axis=0, keepdims=True), sink)
            probs[j, head] = jnp.exp2(sh - m).astype(BF16)
            sink_terms[j, head] = jnp.exp2(sink - m)

    def pv(j):
        rows = slice(j * BLOCK, (j + 2) * BLOCK)
        for g in range(N_KV_HEADS):
            vals = kv_buf[rows, (2 + g) * KV_COLS:(3 + g) * KV_COLS]
            p = jnp.concatenate([probs.pop((j, g * GROUP + i)) for i in range(GROUP)], axis=1)
            o = lax.dot_general(vals, p, TN_DIMS, preferred_element_type=F32)
            out_rows = slice(g * HEAD_DIM, (g + 1) * HEAD_DIM)
            sum_row = (1 - g) * HEAD_DIM
            for i in range(GROUP):
                head = g * GROUP + i
                cols = slice(i * BLOCK, (i + 1) * BLOCK)
                denom = o[sum_row:sum_row + 1, cols] + sink_terms.pop((j, head))
                oh = o[out_rows, cols] * (1.0 / denom)
                attn_buf[head * HEAD_DIM:(head + 1) * HEAD_DIM, j * BLOCK:(j + 1) * BLOCK] = oh.astype(BF16)

    v = {}

    def t_c():
        v["c"] = proj(OFF_C, D_MODEL)

    def t_x():
        cx = v.pop("c") * proj(OFF_X, D_MODEL)
        cx_buf[SUBLANES:SUBLANES + tile, :] = cx
        v["cx"] = cx

    def t_conv():
        v["conv"] = (convw_ref[0:1, :] * cx_buf[SUBLANES - 2:SUBLANES - 2 + tile, :]
                     + convw_ref[1:2, :] * cx_buf[SUBLANES - 1:SUBLANES - 1 + tile, :]
                     + convw_ref[2:3, :] * v.pop("cx"))
        cx_buf[0:SUBLANES, :] = cx_buf[tile:tile + SUBLANES, :]

    def t_b():
        v["u"] = (proj(OFF_B, D_MODEL) * v.pop("conv")).astype(BF16)

    def t_gc():
        v["gate_conv"] = jax.nn.sigmoid(proj(OFF_GC, D_MODEL))

    def t_ga():
        v["gate_attn"] = jax.nn.sigmoid(proj(OFF_GA, D_MODEL))

    def t_yconv():
        v["mixed"] = v.pop("gate_conv") * jnp.dot(v.pop("u"), wco_ref[...], preferred_element_type=F32)

    def t_yattn(r):
        rows = slice(r * half, (r + 1) * half)
        y_attn = lax.dot_general(attn_buf[:, rows], wao_ref[...], TN_DIMS, preferred_element_type=F32)
        v["mixed", r] = (v["mixed"][rows] + v["gate_attn"][rows] * y_attn).astype(BF16)

    def t_out(r):
        rows = slice(r * half, (r + 1) * half)
        out_ref[0, rows, :] = x[rows] + jnp.dot(v.pop(("mixed", r)), wo_ref[...], preferred_element_type=F32)

    qk(0)
    t_c()
    wup_out_ref[...] = wup_ref[...].astype(BF16)
    wdown_out_ref[...] = wdown_ref[...].astype(BF16)
    softmax(0, 0)
    t_x()
    softmax(0, 1)
    t_gc()
    t_conv()
    pv(0)
    qk(1)
    t_b()
    softmax(1, 0)
    t_ga()
    softmax(1, 1)
    pv(1)
    qk(2)
    t_yconv()
    softmax(2, 0)
    t_yattn(0)
    softmax(2, 1)
    pv(2)
    qk(3)
    t_out(0)
    softmax(3, 0)
    softmax(3, 1)
    pv(3)
    kv_buf[0:BLOCK, :] = kv_buf[tile:tile + BLOCK, :]
    t_yattn(1)
    t_out(1)


def _mlp_kernel(x_ref, g_ref, wup_ref, wdown_ref, gf_ref, out_ref, *, subtiles):
    normed, act, mixed = {}, {}, {}
    starts = [sum(subtiles[:i]) for i in range(len(subtiles))]

    def rows(i):
        return slice(starts[i], starts[i] + subtiles[i])

    def norm(i):
        normed[i] = _rmsnorm(x_ref[rows(i), :], g_ref[...]).astype(BF16)

    def up(i):
        u = jnp.dot(normed.pop(i), wup_ref[...], preferred_element_type=F32)
        act[i] = jnp.square(jnp.maximum(u, 0.0)).astype(BF16)

    def down(i):
        mixed[i] = x_ref[rows(i), :] + jnp.dot(act.pop(i), wdown_ref[...], preferred_element_type=F32)

    def final(i):
        out_ref[rows(i), :] = _rmsnorm(mixed.pop(i), gf_ref[...])

    n = len(subtiles)
    norm(0)
    up(0)
    for i in range(n):
        if i + 1 < n:
            norm(i + 1)
        down(i)
        if i + 1 < n:
            up(i + 1)
        final(i)


def _resident(shape):
    return pl.BlockSpec(shape, lambda *_: (0,) * len(shape), pipeline_mode=pl.Buffered(1))


def _mixer(x, g, w_in, conv_w, w_conv_out, sinks, rel_bias, w_attn_out, w_o, w_up, w_down):
    batch, seq, _ = x.shape
    tile = TILE_MIX
    assert seq % tile == 0 and tile % BLOCK == 0
    tiles = seq // tile
    steps = batch * tiles
    up_rows, down_rows = D_MODEL // steps, D_FF // steps
    assert up_rows * steps == D_MODEL and up_rows % BF16_SUBLANES == 0
    assert down_rows * steps == D_FF and down_rows % BF16_SUBLANES == 0
    assert IN_COLS % STAGE_COLS == 0 and D_MODEL % STAGE_COLS == 0

    def tile_of(s):
        return jnp.maximum(s - 1, 0)

    def x_map(s):
        return (tile_of(s) // tiles, tile_of(s) % tiles, 0)

    def slice_map(s):
        return (tile_of(s), 0)

    hbm = pl.BlockSpec(memory_space=pl.ANY)
    return pl.pallas_call(
        functools.partial(_mixer_kernel, tile=tile, tiles=tiles),
        grid=(steps + 1,),
        in_specs=[
            pl.BlockSpec((1, tile, D_MODEL), x_map),
            _resident((1, D_MODEL)),
            hbm,
            _resident((CONV_WIDTH, D_MODEL)),
            hbm,
            pl.BlockSpec(memory_space=pltpu.SMEM),
            _resident((2 * BLOCK, BLOCK)),
            pl.BlockSpec(memory_space=pltpu.SMEM),
            hbm,
            hbm,
            pl.BlockSpec((up_rows, D_FF), slice_map),
            pl.BlockSpec((down_rows, D_MODEL), slice_map),
        ],
        out_specs=[
            pl.BlockSpec((1, tile, D_MODEL), x_map),
            pl.BlockSpec((up_rows, D_FF), slice_map),
            pl.BlockSpec((down_rows, D_MODEL), slice_map),
        ],
        out_shape=[
            jax.ShapeDtypeStruct(x.shape, F32),
            jax.ShapeDtypeStruct(w_up.shape, BF16),
            jax.ShapeDtypeStruct(w_down.shape, BF16),
        ],
        scratch_shapes=[
            pltpu.VMEM((D_MODEL, IN_COLS), BF16),
            pltpu.VMEM((D_MODEL, D_MODEL), BF16),
            pltpu.VMEM((D_MODEL, D_MODEL), BF16),
            pltpu.VMEM((D_MODEL, D_MODEL), BF16),
            pltpu.VMEM((STAGE_SLOTS, D_MODEL, STAGE_COLS), F32),
            pltpu.SemaphoreType.DMA((STAGE_SLOTS,)),
            pltpu.VMEM((2, N_HEADS, 2 * BLOCK, BLOCK), F32),
            pltpu.VMEM((tile + 2 * SUBLANES, D_MODEL), F32),
            pltpu.VMEM((tile + BLOCK, 4 * KV_COLS), BF16),
            pltpu.VMEM((D_MODEL, tile), BF16),
        ],
        compiler_params=pltpu.CompilerParams(
            dimension_semantics=("arbitrary",), vmem_limit_bytes=VMEM_LIMIT),
        name="token_mixer",
    )(x, g, w_in, conv_w, w_conv_out, sinks, jnp.asarray(_bucket_table()), rel_bias, w_attn_out, w_o, w_up, w_down)


def _mlp(x, g, w_up, w_down, g_final):
    rows = x.shape[0]
    tile = sum(SUBTILES_MLP)
    assert rows % tile == 0 and all(s % BF16_SUBLANES == 0 for s in SUBTILES_MLP)
    return pl.pallas_call(
        functools.partial(_mlp_kernel, subtiles=SUBTILES_MLP),
        grid=(rows // tile,),
        in_specs=[
            pl.BlockSpec((tile, D_MODEL), lambda i: (i, 0)),
            _resident((1, D_MODEL)),
            _resident((D_MODEL, D_FF)),
            _resident((D_FF, D_MODEL)),
            _resident((1, D_MODEL)),
        ],
        out_specs=pl.BlockSpec((tile, D_MODEL), lambda i: (i, 0)),
        out_shape=jax.ShapeDtypeStruct(x.shape, F32),
        compiler_params=pltpu.CompilerParams(
            dimension_semantics=("arbitrary",), vmem_limit_bytes=VMEM_LIMIT),
        name="channel_mixer",
    )(x, g, w_up, w_down, g_final)


def kernel(x, attn_norm_g, w_in, conv_w, w_conv_out, attn_sinks, rel_bias, w_attn_out, w_o, mlp_norm_g, w_up,
           w_down, final_norm_g):
    batch, seq, _ = x.shape
    depth = w_in.shape[0]
    assert depth == 1, "the final norm is fused into the (only) layer's channel mixer"
    x, w_up_bf16, w_down_bf16 = _mixer(
        x, attn_norm_g.reshape(1, D_MODEL), w_in.reshape(D_MODEL, IN_COLS), conv_w.reshape(CONV_WIDTH, D_MODEL),
        w_conv_out.reshape(D_MODEL, D_MODEL), attn_sinks.reshape(N_HEADS).astype(F32), rel_bias.astype(F32),
        w_attn_out.reshape(D_MODEL, D_MODEL), w_o.reshape(D_MODEL, D_MODEL), w_up.reshape(D_MODEL, D_FF),
        w_down.reshape(D_FF, D_MODEL))
    x = _mlp(x.reshape(batch * seq, D_MODEL), mlp_norm_g.reshape(1, D_MODEL), w_up_bf16, w_down_bf16,
             final_norm_g.reshape(1, D_MODEL))
    return x.reshape(batch, seq, D_MODEL)
```

```python
import functools
import math

import jax
import jax.numpy as jnp
import numpy as np
from jax import lax
from jax.experimental import pallas as pl
from jax.experimental.pallas import tpu as pltpu

D_MODEL = 1024
HEAD_DIM = 64
N_HEADS = 16
N_KV_HEADS = 2
GROUP = N_HEADS // N_KV_HEADS
BLOCK = 128
N_BUCKETS = 32
MAX_DISTANCE = 128
CONV_WIDTH = 3
D_FF = 4 * D_MODEL
EPS = 1e-6
NEG_INF = -1e30
LOG2E = math.log2(math.e)

LANES = 128
SUBLANES = 8
BF16_SUBLANES = 16
KV_COLS = N_KV_HEADS * HEAD_DIM
PAIRS_PER_GROUP = GROUP // 2

OFF_B, OFF_C, OFF_X, OFF_Q = 0, D_MODEL, 2 * D_MODEL, 3 * D_MODEL
OFF_K = 4 * D_MODEL
OFF_V = OFF_K + KV_COLS
OFF_GC = OFF_V + KV_COLS
OFF_GA = OFF_GC + D_MODEL
IN_COLS = OFF_GA + D_MODEL

TILE_MIX = 512
SUBTILES_MLP = (512, 512)
VMEM_LIMIT = 56 * 1024 * 1024
STAGE_COLS = 256
STAGE_SLOTS = 4

BF16 = jnp.bfloat16
F32 = jnp.float32

NT_DIMS = (((1,), (1,)), ((), ()))
TN_DIMS = (((0,), (0,)), ((), ()))


def _bucket_table():
    kj = np.arange(2 * BLOCK)[:, None]
    qi = np.arange(BLOCK)[None, :]
    dist = qi + BLOCK - kj
    n = np.maximum(dist, 0)
    max_exact = N_BUCKETS // 2
    ratio = np.log(np.maximum(n, max_exact).astype(np.float32) / max_exact) / np.log(MAX_DISTANCE / max_exact)
    large = np.minimum(max_exact + (ratio * (N_BUCKETS - max_exact)).astype(np.int32), N_BUCKETS - 1)
    bucket = np.where(n < max_exact, n, large).astype(np.int32)
    in_window = (dist >= 0) & (dist < BLOCK)
    return np.where(in_window, bucket, -1).astype(np.int32)


def _bias_table(head, bucket_ref, rel_ref, bias_buf):
    bucket = bucket_ref[...]
    acc = jnp.full(bucket.shape, NEG_INF, F32)
    for b in range(N_BUCKETS):
        acc = jnp.where(bucket == b, rel_ref[b, head], acc)
    acc = acc * LOG2E
    bias_buf[0, head] = acc
    key = lax.broadcasted_iota(jnp.int32, bucket.shape, 0)
    bias_buf[1, head] = jnp.where(key >= BLOCK, acc, NEG_INF)


def _rmsnorm(x, g):
    y = x * lax.rsqrt(jnp.mean(x * x, axis=-1, keepdims=True) + EPS)
    return y * g


def _stage_weights(pairs, stage, sems, fillers):
    chunks = [(src, dst, c) for src, dst in pairs for c in range(0, src.shape[1], STAGE_COLS)]
    slots = stage.shape[0]

    def copy(i):
        src, _, c = chunks[i]
        return pltpu.make_async_copy(src.at[:, c:c + STAGE_COLS], stage.at[i % slots], sems.at[i % slots])

    for i in range(min(slots, len(chunks))):
        copy(i).start()
    fillers = list(fillers)
    assert len(fillers) <= len(chunks)
    for i, (_, dst, c) in enumerate(chunks):
        if fillers:
            fillers.pop(0)()
        copy(i).wait()
        dst[:, c:c + STAGE_COLS] = stage[i % slots].astype(BF16)
        if i + slots < len(chunks):
            copy(i + slots).start()


def _mixer_kernel(x_ref, g_ref, win_hbm, convw_ref, wco_hbm, sinks_ref, bucket_ref, rel_ref, wao_hbm, wo_hbm,
                  wup_ref, wdown_ref, out_ref, wup_out_ref, wdown_out_ref,
                  win_buf, wco_buf, wao_buf, wo_buf, stage, stage_sems, bias_buf, cx_buf, kv_buf, attn_buf,
                  *, tile, tiles):
    s = pl.program_id(0)

    @pl.when(s == 0)
    def _():
        bias_tasks = [functools.partial(_bias_table, head, bucket_ref, rel_ref, bias_buf)
                      for head in range(N_HEADS)]
        _stage_weights([(win_hbm, win_buf), (wco_hbm, wco_buf), (wao_hbm, wao_buf), (wo_hbm, wo_buf)],
                       stage, stage_sems, bias_tasks)

    @pl.when(s > 0)
    def _():
        _mixer_step((s - 1) % tiles, x_ref, g_ref, win_buf, convw_ref, wco_buf, sinks_ref, bias_buf, wao_buf,
                    wo_buf, wup_ref, wdown_ref, out_ref, wup_out_ref, wdown_out_ref, cx_buf, kv_buf, attn_buf,
                    tile=tile)


def _mixer_step(t, x_ref, g_ref, win_ref, convw_ref, wco_ref, sinks_ref, bias_ref, wao_ref, wo_ref,
                wup_ref, wdown_ref, out_ref, wup_out_ref, wdown_out_ref, cx_buf, kv_buf, attn_buf, *, tile):
    @pl.when(t == 0)
    def _():
        cx_buf[0:SUBLANES, :] = jnp.zeros((SUBLANES, D_MODEL), F32)
        kv_buf[0:BLOCK, :] = jnp.zeros((BLOCK, 4 * KV_COLS), BF16)

    x = x_ref[0]
    half = tile // 2
    h_halves = [_rmsnorm(x[r * half:(r + 1) * half], g_ref[...]).astype(BF16) for r in range(2)]
    h = jnp.concatenate(h_halves, axis=0)

    def proj(lo, width):
        return jnp.dot(h, win_ref[:, lo:lo + width], preferred_element_type=F32)

    def proj_halves(lo, width):
        w = win_ref[:, lo:lo + width]
        return jnp.concatenate([jnp.dot(hr, w, preferred_element_type=F32) for hr in h_halves], axis=0)

    q = (proj_halves(OFF_Q, D_MODEL) * (HEAD_DIM ** -0.5 * LOG2E)).astype(BF16)
    kv = proj_halves(OFF_K, 2 * KV_COLS)
    k = kv[:, 0:KV_COLS]
    val = kv[:, KV_COLS:2 * KV_COLS]
    k_swapped = pltpu.roll(k, HEAD_DIM, axis=1)
    low_half_tile = lax.broadcasted_iota(jnp.int32, (tile, LANES), 1) < HEAD_DIM
    new_rows = slice(BLOCK, BLOCK + tile)
    kv_buf[new_rows, 0:KV_COLS] = jnp.where(low_half_tile, k, k_swapped).astype(BF16)
    kv_buf[new_rows, KV_COLS:2 * KV_COLS] = jnp.where(low_half_tile, k_swapped, k).astype(BF16)
    kv_buf[new_rows, 2 * KV_COLS:3 * KV_COLS] = jnp.where(low_half_tile, val, 1.0).astype(BF16)
    kv_buf[new_rows, 3 * KV_COLS:4 * KV_COLS] = jnp.where(low_half_tile, 1.0, val).astype(BF16)

    low_half = lax.broadcasted_iota(jnp.int32, (BLOCK, LANES), 1) < HEAD_DIM
    zero = jnp.zeros((BLOCK, LANES), BF16)

    scores, probs, sink_terms = {}, {}, {}

    def qk(j):
        rows = slice(j * BLOCK, (j + 2) * BLOCK)
        qj = q[j * BLOCK:(j + 1) * BLOCK, :]
        for g in range(N_KV_HEADS):
            parts = []
            for p in range(g * PAIRS_PER_GROUP, (g + 1) * PAIRS_PER_GROUP):
                qp = qj[:, p * LANES:(p + 1) * LANES]
                parts.append(jnp.where(low_half, qp, zero))
                parts.append(jnp.where(low_half, zero, qp))
            q_heads = jnp.concatenate(parts, axis=0)
            keys = kv_buf[rows, g * KV_COLS:(g + 1) * KV_COLS]
            scores[j, g] = lax.dot_general(keys, q_heads, NT_DIMS, preferred_element_type=F32)

    def softmax(j, g):
        first = (t == 0).astype(jnp.int32) if j == 0 else 0
        s = scores.pop((j, g))
        for i in range(GROUP):
            head = g * GROUP + i
            sh = s[:, i * BLOCK:(i + 1) * BLOCK] + bias_ref[first, head]
            sink = jnp.full((1, BLOCK), sinks_ref[head], F32) * LOG2E
            m = jnp.maximum(jnp.max(sh, axis=0, keepdims=True), sink)
            probs[j, head] = jnp.exp2(sh - m).astype(BF16)
            sink_terms[j, head] = jnp.exp2(sink - m)

    def pv(j):
        rows = slice(j * BLOCK, (j + 2) * BLOCK)
        for g in range(N_KV_HEADS):
            vals = kv_buf[rows, (2 + g) * KV_COLS:(3 + g) * KV_COLS]
            p = jnp.concatenate([probs.pop((j, g * GROUP + i)) for i in range(GROUP)], axis=1)
            o = lax.dot_general(vals, p, TN_DIMS, preferred_element_type=F32)
            out_rows = slice(g * HEAD_DIM, (g + 1) * HEAD_DIM)
            sum_row = (1 - g) * HEAD_DIM
            for i in range(GROUP):
                head = g * GROUP + i
                cols = slice(i * BLOCK, (i + 1) * BLOCK)
                denom = o[sum_row:sum_row + 1, cols] + sink_terms.pop((j, head))
                oh = o[out_rows, cols] * (1.0 / denom)
                attn_buf[head * HEAD_DIM:(head + 1) * HEAD_DIM, j * BLOCK:(j + 1) * BLOCK] = oh.astype(BF16)

    v = {}

    def t_c():
        v["c"] = proj(OFF_C, D_MODEL)

    def t_x():
        cx = v.pop("c") * proj(OFF_X, D_MODEL)
        cx_buf[SUBLANES:SUBLANES + tile, :] = cx
        v["cx"] = cx

    def t_conv():
        v["conv"] = (convw_ref[0:1, :] * cx_buf[SUBLANES - 2:SUBLANES - 2 + tile, :]
                     + convw_ref[1:2, :] * cx_buf[SUBLANES - 1:SUBLANES - 1 + tile, :]
                     + convw_ref[2:3, :] * v.pop("cx"))
        cx_buf[0:SUBLANES, :] = cx_buf[tile:tile + SUBLANES, :]

    def t_b():
        v["u"] = (proj(OFF_B, D_MODEL) * v.pop("conv")).astype(BF16)

    def t_gc():
        v["gate_conv"] = jax.nn.sigmoid(proj(OFF_GC, D_MODEL))

    def t_ga():
        v["gate_attn"] = jax.nn.sigmoid(proj(OFF_GA, D_MODEL))

    def t_yconv():
        v["mixed"] = v.pop("gate_conv") * jnp.dot(v.pop("u"), wco_ref[...], preferred_element_type=F32)

    def t_yattn(r):
        rows = slice(r * half, (r + 1) * half)
        y_attn = lax.dot_general(attn_buf[:, rows], wao_ref[...], TN_DIMS, preferred_element_type=F32)
        v["mixed", r] = (v["mixed"][rows] + v["gate_attn"][rows] * y_attn).astype(BF16)

    def t_out(r):
        rows = slice(r * half, (r + 1) * half)
        out_ref[0, rows, :] = x[rows] + jnp.dot(v.pop(("mixed", r)), wo_ref[...], preferred_element_type=F32)

    qk(0)
    t_c()
    wup_out_ref[...] = wup_ref[...].astype(BF16)
    wdown_out_ref[...] = wdown_ref[...].astype(BF16)
    softmax(0, 0)
    t_x()
    softmax(0, 1)
    t_gc()
    t_conv()
    pv(0)
    qk(1)
    t_b()
    softmax(1, 0)
    t_ga()
    softmax(1, 1)
    pv(1)
    qk(2)
    t_yconv()
    softmax(2, 0)
    t_yattn(0)
    softmax(2, 1)
    pv(2)
    qk(3)
    t_out(0)
    softmax(3, 0)
    softmax(3, 1)
    pv(3)
    kv_buf[0:BLOCK, :] = kv_buf[tile:tile + BLOCK, :]
    t_yattn(1)
    t_out(1)


def _mlp_kernel(x_ref, g_ref, wup_ref, wdown_ref, gf_ref, out_ref, *, subtiles):
    normed, act, mixed = {}, {}, {}
    starts = [sum(subtiles[:i]) for i in range(len(subtiles))]

    def rows(i):
        return slice(starts[i], starts[i] + subtiles[i])

    def norm(i):
        normed[i] = _rmsnorm(x_ref[rows(i), :], g_ref[...]).astype(BF16)

    def up(i):
        u = jnp.dot(normed.pop(i), wup_ref[...], preferred_element_type=F32)
        act[i] = jnp.square(jnp.maximum(u, 0.0)).astype(BF16)

    def down(i):
        mixed[i] = x_ref[rows(i), :] + jnp.dot(act.pop(i), wdown_ref[...], preferred_element_type=F32)

    def final(i):
        out_ref[rows(i), :] = _rmsnorm(mixed.pop(i), gf_ref[...])

    n = len(subtiles)
    norm(0)
    up(0)
    for i in range(n):
        if i + 1 < n:
            norm(i + 1)
        down(i)
        if i + 1 < n:
            up(i + 1)
        final(i)


def _resident(shape):
    return pl.BlockSpec(shape, lambda *_: (0,) * len(shape), pipeline_mode=pl.Buffered(1))


def _mixer(x, g, w_in, conv_w, w_conv_out, sinks, rel_bias, w_attn_out, w_o, w_up, w_down):
    batch, seq, _ = x.shape
    tile = TILE_MIX
    assert seq % tile == 0 and tile % BLOCK == 0
    tiles = seq // tile
    steps = batch * tiles
    up_rows, down_rows = D_MODEL // steps, D_FF // steps
    assert up_rows * steps == D_MODEL and up_rows % BF16_SUBLANES == 0
    assert down_rows * steps == D_FF and down_rows % BF16_SUBLANES == 0
    assert IN_COLS % STAGE_COLS == 0 and D_MODEL % STAGE_COLS == 0

    def tile_of(s):
        return jnp.maximum(s - 1, 0)

    def x_map(s):
        return (tile_of(s) // tiles, tile_of(s) % tiles, 0)

    def slice_map(s):
        return (tile_of(s), 0)

    hbm = pl.BlockSpec(memory_space=pl.ANY)
    return pl.pallas_call(
        functools.partial(_mixer_kernel, tile=tile, tiles=tiles),
        grid=(steps + 1,),
        in_specs=[
            pl.BlockSpec((1, tile, D_MODEL), x_map),
            _resident((1, D_MODEL)),
            hbm,
            _resident((CONV_WIDTH, D_MODEL)),
            hbm,
            pl.BlockSpec(memory_space=pltpu.SMEM),
            _resident((2 * BLOCK, BLOCK)),
            pl.BlockSpec(memory_space=pltpu.SMEM),
            hbm,
            hbm,
            pl.BlockSpec((up_rows, D_FF), slice_map),
            pl.BlockSpec((down_rows, D_MODEL), slice_map),
        ],
        out_specs=[
            pl.BlockSpec((1, tile, D_MODEL), x_map),
            pl.BlockSpec((up_rows, D_FF), slice_map),
            pl.BlockSpec((down_rows, D_MODEL), slice_map),
        ],
        out_shape=[
            jax.ShapeDtypeStruct(x.shape, F32),
            jax.ShapeDtypeStruct(w_up.shape, BF16),
            jax.ShapeDtypeStruct(w_down.shape, BF16),
        ],
        scratch_shapes=[
            pltpu.VMEM((D_MODEL, IN_COLS), BF16),
            pltpu.VMEM((D_MODEL, D_MODEL), BF16),
            pltpu.VMEM((D_MODEL, D_MODEL), BF16),
            pltpu.VMEM((D_MODEL, D_MODEL), BF16),
            pltpu.VMEM((STAGE_SLOTS, D_MODEL, STAGE_COLS), F32),
            pltpu.SemaphoreType.DMA((STAGE_SLOTS,)),
            pltpu.VMEM((2, N_HEADS, 2 * BLOCK, BLOCK), F32),
            pltpu.VMEM((tile + 2 * SUBLANES, D_MODEL), F32),
            pltpu.VMEM((tile + BLOCK, 4 * KV_COLS), BF16),
            pltpu.VMEM((D_MODEL, tile), BF16),
        ],
        compiler_params=pltpu.CompilerParams(
            dimension_semantics=("arbitrary",), vmem_limit_bytes=VMEM_LIMIT),
        name="token_mixer",
    )(x, g, w_in, conv_w, w_conv_out, sinks, jnp.asarray(_bucket_table()), rel_bias, w_attn_out, w_o, w_up, w_down)


def _mlp(x, g, w_up, w_down, g_final):
    rows = x.shape[0]
    tile = sum(SUBTILES_MLP)
    assert rows % tile == 0 and all(s % BF16_SUBLANES == 0 for s in SUBTILES_MLP)
    return pl.pallas_call(
        functools.partial(_mlp_kernel, subtiles=SUBTILES_MLP),
        grid=(rows // tile,),
        in_specs=[
            pl.BlockSpec((tile, D_MODEL), lambda i: (i, 0)),
            _resident((1, D_MODEL)),
            _resident((D_MODEL, D_FF)),
            _resident((D_FF, D_MODEL)),
            _resident((1, D_MODEL)),
        ],
        out_specs=pl.BlockSpec((tile, D_MODEL), lambda i: (i, 0)),
        out_shape=jax.ShapeDtypeStruct(x.shape, F32),
        compiler_params=pltpu.CompilerParams(
            dimension_semantics=("arbitrary",), vmem_limit_bytes=VMEM_LIMIT),
        name="channel_mixer",
    )(x, g, w_up, w_down, g_final)


def kernel(x, attn_norm_g, w_in, conv_w, w_conv_out, attn_sinks, rel_bias, w_attn_out, w_o, mlp_norm_g, w_up,
           w_down, final_norm_g):
    batch, seq, _ = x.shape
    depth = w_in.shape[0]
    assert depth == 1, "the final norm is fused into the (only) layer's channel mixer"
    x, w_up_bf16, w_down_bf16 = _mixer(
        x, attn_norm_g.reshape(1, D_MODEL), w_in.reshape(D_MODEL, IN_COLS), conv_w.reshape(CONV_WIDTH, D_MODEL),
        w_conv_out.reshape(D_MODEL, D_MODEL), attn_sinks.reshape(N_HEADS).astype(F32), rel_bias.astype(F32),
        w_attn_out.reshape(D_MODEL, D_MODEL), w_o.reshape(D_MODEL, D_MODEL), w_up.reshape(D_MODEL, D_FF),
        w_down.reshape(D_FF, D_MODEL))
    x = _mlp(x.reshape(batch * seq, D_MODEL), mlp_norm_g.reshape(1, D_MODEL), w_up_bf16, w_down_bf16,
             final_norm_g.reshape(1, D_MODEL))
    return x.reshape(batch, seq, D_MODEL)
```

```python
import functools
import math

import jax
import jax.numpy as jnp
import numpy as np
from jax import lax
from jax.experimental import pallas as pl
from jax.experimental.pallas import tpu as pltpu

D_MODEL = 1024
HEAD_DIM = 64
N_HEADS = 16
N_KV_HEADS = 2
GROUP = N_HEADS // N_KV_HEADS
BLOCK = 128
N_BUCKETS = 32
MAX_DISTANCE = 128
CONV_WIDTH = 3
D_FF = 4 * D_MODEL
EPS = 1e-6
NEG_INF = -1e30
LOG2E = math.log2(math.e)

LANES = 128
SUBLANES = 8
BF16_SUBLANES = 16
KV_COLS = N_KV_HEADS * HEAD_DIM
PAIRS_PER_GROUP = GROUP // 2

OFF_B, OFF_C, OFF_X, OFF_Q = 0, D_MODEL, 2 * D_MODEL, 3 * D_MODEL
OFF_K = 4 * D_MODEL
OFF_V = OFF_K + KV_COLS
OFF_GC = OFF_V + KV_COLS
OFF_GA = OFF_GC + D_MODEL
IN_COLS = OFF_GA + D_MODEL

TILE_MIX = 512
SUBTILES_MLP = (512, 512)
VMEM_LIMIT = 56 * 1024 * 1024
STAGE_COLS = 256
STAGE_SLOTS = 4

BF16 = jnp.bfloat16
F32 = jnp.float32

NT_DIMS = (((1,), (1,)), ((), ()))
TN_DIMS = (((0,), (0,)), ((), ()))


def _bucket_table():
    kj = np.arange(2 * BLOCK)[:, None]
    qi = np.arange(BLOCK)[None, :]
    dist = qi + BLOCK - kj
    n = np.maximum(dist, 0)
    max_exact = N_BUCKETS // 2
    ratio = np.log(np.maximum(n, max_exact).astype(np.float32) / max_exact) / np.log(MAX_DISTANCE / max_exact)
    large = np.minimum(max_exact + (ratio * (N_BUCKETS - max_exact)).astype(np.int32), N_BUCKETS - 1)
    bucket = np.where(n < max_exact, n, large).astype(np.int32)
    in_window = (dist >= 0) & (dist < BLOCK)
    return np.where(in_window, bucket, -1).astype(np.int32)


def _bias_table(head, bucket_ref, rel_ref, bias_buf):
    bucket = bucket_ref[...]
    acc = jnp.full(bucket.shape, NEG_INF, F32)
    for b in range(N_BUCKETS):
        acc = jnp.where(bucket == b, rel_ref[b, head], acc)
    acc = acc * LOG2E
    bias_buf[0, head] = acc
    key = lax.broadcasted_iota(jnp.int32, bucket.shape, 0)
    bias_buf[1, head] = jnp.where(key >= BLOCK, acc, NEG_INF)


def _rmsnorm(x, g):
    y = x * lax.rsqrt(jnp.mean(x * x, axis=-1, keepdims=True) + EPS)
    return y * g


def _stage_weights(pairs, stage, sems, fillers):
    chunks = [(src, dst, c) for src, dst in pairs for c in range(0, src.shape[1], STAGE_COLS)]
    slots = stage.shape[0]

    def copy(i):
        src, _, c = chunks[i]
        return pltpu.make_async_copy(src.at[:, c:c + STAGE_COLS], stage.at[i % slots], sems.at[i % slots])

    for i in range(min(slots, len(chunks))):
        copy(i).start()
    fillers = list(fillers)
    assert len(fillers) <= len(chunks)
    for i, (_, dst, c) in enumerate(chunks):
        if fillers:
            fillers.pop(0)()
        copy(i).wait()
        dst[:, c:c + STAGE_COLS] = stage[i % slots].astype(BF16)
        if i + slots < len(chunks):
            copy(i + slots).start()


def _mixer_kernel(x_ref, g_ref, win_hbm, convw_ref, wco_hbm, sinks_ref, bucket_ref, rel_ref, wao_hbm, wo_hbm,
                  wup_ref, wdown_ref, out_ref, wup_out_ref, wdown_out_ref,
                  win_buf, wco_buf, wao_buf, wo_buf, stage, stage_sems, bias_buf, cx_buf, kv_buf, attn_buf,
                  *, tile, tiles):
    s = pl.program_id(0)

    @pl.when(s == 0)
    def _():
        bias_tasks = [functools.partial(_bias_table, head, bucket_ref, rel_ref, bias_buf)
                      for head in range(N_HEADS)]
        _stage_weights([(win_hbm, win_buf), (wco_hbm, wco_buf), (wao_hbm, wao_buf), (wo_hbm, wo_buf)],
                       stage, stage_sems, bias_tasks)

    @pl.when(s > 0)
    def _():
        _mixer_step((s - 1) % tiles, x_ref, g_ref, win_buf, convw_ref, wco_buf, sinks_ref, bias_buf, wao_buf,
                    wo_buf, wup_ref, wdown_ref, out_ref, wup_out_ref, wdown_out_ref, cx_buf, kv_buf, attn_buf,
                    tile=tile)


def _mixer_step(t, x_ref, g_ref, win_ref, convw_ref, wco_ref, sinks_ref, bias_ref, wao_ref, wo_ref,
                wup_ref, wdown_ref, out_ref, wup_out_ref, wdown_out_ref, cx_buf, kv_buf, attn_buf, *, tile):
    @pl.when(t == 0)
    def _():
        cx_buf[0:SUBLANES, :] = jnp.zeros((SUBLANES, D_MODEL), F32)
        kv_buf[0:BLOCK, :] = jnp.zeros((BLOCK, 4 * KV_COLS), BF16)

    x = x_ref[0]
    half = tile // 2
    h_halves = [_rmsnorm(x[r * half:(r + 1) * half], g_ref[...]).astype(BF16) for r in range(2)]
    h = jnp.concatenate(h_halves, axis=0)

    def proj(lo, width):
        return jnp.dot(h, win_ref[:, lo:lo + width], preferred_element_type=F32)

    def proj_halves(lo, width):
        w = win_ref[:, lo:lo + width]
        return jnp.concatenate([jnp.dot(hr, w, preferred_element_type=F32) for hr in h_halves], axis=0)

    q = (proj_halves(OFF_Q, D_MODEL) * (HEAD_DIM ** -0.5 * LOG2E)).astype(BF16)
    kv = proj_halves(OFF_K, 2 * KV_COLS)
    k = kv[:, 0:KV_COLS]
    val = kv[:, KV_COLS:2 * KV_COLS]
    k_swapped = pltpu.roll(k, HEAD_DIM, axis=1)
    low_half_tile = lax.broadcasted_iota(jnp.int32, (tile, LANES), 1) < HEAD_DIM
    new_rows = slice(BLOCK, BLOCK + tile)
    kv_buf[new_rows, 0:KV_COLS] = jnp.where(low_half_tile, k, k_swapped).astype(BF16)
    kv_buf[new_rows, KV_COLS:2 * KV_COLS] = jnp.where(low_half_tile, k_swapped, k).astype(BF16)
    kv_buf[new_rows, 2 * KV_COLS:3 * KV_COLS] = jnp.where(low_half_tile, val, 1.0).astype(BF16)
    kv_buf[new_rows, 3 * KV_COLS:4 * KV_COLS] = jnp.where(low_half_tile, 1.0, val).astype(BF16)

    low_half = lax.broadcasted_iota(jnp.int32, (BLOCK, LANES), 1) < HEAD_DIM
    zero = jnp.zeros((BLOCK, LANES), BF16)

    scores, probs, sink_terms = {}, {}, {}

    def qk(j):
        rows = slice(j * BLOCK, (j + 2) * BLOCK)
        qj = q[j * BLOCK:(j + 1) * BLOCK, :]
        for g in range(N_KV_HEADS):
            parts = []
            for p in range(g * PAIRS_PER_GROUP, (g + 1) * PAIRS_PER_GROUP):
                qp = qj[:, p * LANES:(p + 1) * LANES]
                parts.append(jnp.where(low_half, qp, zero))
                parts.append(jnp.where(low_half, zero, qp))
            q_heads = jnp.concatenate(parts, axis=0)
            keys = kv_buf[rows, g * KV_COLS:(g + 1) * KV_COLS]
            scores[j, g] = lax.dot_general(keys, q_heads, NT_DIMS, preferred_element_type=F32)

    def softmax(j, g):
        first = (t == 0).astype(jnp.int32) if j == 0 else 0
        s = scores.pop((j, g))
        for i in range(GROUP):
            head = g * GROUP + i
            sh = s[:, i * BLOCK:(i + 1) * BLOCK] + bias_ref[first, head]
            sink = jnp.full((1, BLOCK), sinks_ref[head], F32) * LOG2E
            m = jnp.maximum(jnp.max(sh, axis=0, keepdims=True), sink)
            probs[j, head] = jnp.exp2(sh - m).astype(BF16)
            sink_terms[j, head] = jnp.exp2(sink - m)

    def pv(j):
        rows = slice(j * BLOCK, (j + 2) * BLOCK)
        for g in range(N_KV_HEADS):
            vals = kv_buf[rows, (2 + g) * KV_COLS:(3 + g) * KV_COLS]
            p = jnp.concatenate([probs.pop((j, g * GROUP + i)) for i in range(GROUP)], axis=1)
            o = lax.dot_general(vals, p, TN_DIMS, preferred_element_type=F32)
            out_rows = slice(g * HEAD_DIM, (g + 1) * HEAD_DIM)
            sum_row = (1 - g) * HEAD_DIM
            for i in range(GROUP):
                head = g * GROUP + i
                cols = slice(i * BLOCK, (i + 1) * BLOCK)
                denom = o[sum_row:sum_row + 1, cols] + sink_terms.pop((j, head))
                oh = o[out_rows, cols] * (1.0 / denom)
                attn_buf[head * HEAD_DIM:(head + 1) * HEAD_DIM, j * BLOCK:(j + 1) * BLOCK] = oh.astype(BF16)

    v = {}

    def t_c():
        v["c"] = proj(OFF_C, D_MODEL)

    def t_x():
        cx_buf[SUBLANES:SUBLANES + tile, :] = v.pop("c") * proj(OFF_X, D_MODEL)

    def t_conv():
        v["conv"] = (convw_ref[0:1, :] * cx_buf[SUBLANES - 2:SUBLANES - 2 + tile, :]
                     + convw_ref[1:2, :] * cx_buf[SUBLANES - 1:SUBLANES - 1 + tile, :]
                     + convw_ref[2:3, :] * cx_buf[SUBLANES:SUBLANES + tile, :])
        cx_buf[0:SUBLANES, :] = cx_buf[tile:tile + SUBLANES, :]

    def t_b():
        v["u"] = (proj(OFF_B, D_MODEL) * v.pop("conv")).astype(BF16)

    def t_gc():
        v["gate_conv"] = jax.nn.sigmoid(proj(OFF_GC, D_MODEL))

    def t_ga():
        v["gate_attn"] = jax.nn.sigmoid(proj(OFF_GA, D_MODEL))

    def t_yconv():
        v["mixed"] = v.pop("gate_conv") * jnp.dot(v.pop("u"), wco_ref[...], preferred_element_type=F32)

    def t_yattn(r):
        rows = slice(r * half, (r + 1) * half)
        y_attn = lax.dot_general(attn_buf[:, rows], wao_ref[...], TN_DIMS, preferred_element_type=F32)
        v["mixed", r] = (v["mixed"][rows] + v["gate_attn"][rows] * y_attn).astype(BF16)

    def t_out(r):
        rows = slice(r * half, (r + 1) * half)
        out_ref[0, rows, :] = x[rows] + jnp.dot(v.pop(("mixed", r)), wo_ref[...], preferred_element_type=F32)

    qk(0)
    t_c()
    wup_out_ref[...] = wup_ref[...].astype(BF16)
    wdown_out_ref[...] = wdown_ref[...].astype(BF16)
    softmax(0, 0)
    t_x()
    softmax(0, 1)
    t_gc()
    t_conv()
    pv(0)
    qk(1)
    t_b()
    softmax(1, 0)
    t_ga()
    softmax(1, 1)
    pv(1)
    qk(2)
    t_yconv()
    softmax(2, 0)
    t_yattn(0)
    softmax(2, 1)
    pv(2)
    qk(3)
    t_out(0)
    softmax(3, 0)
    softmax(3, 1)
    pv(3)
    kv_buf[0:BLOCK, :] = kv_buf[tile:tile + BLOCK, :]
    t_yattn(1)
    t_out(1)


def _mlp_kernel(x_ref, g_ref, wup_ref, wdown_ref, gf_ref, out_ref, *, subtiles):
    normed, act, mixed = {}, {}, {}
    starts = [sum(subtiles[:i]) for i in range(len(subtiles))]

    def rows(i):
        return slice(starts[i], starts[i] + subtiles[i])

    def norm(i):
        normed[i] = _rmsnorm(x_ref[rows(i), :], g_ref[...]).astype(BF16)

    def up(i):
        u = jnp.dot(normed.pop(i), wup_ref[...], preferred_element_type=F32)
        act[i] = jnp.square(jnp.maximum(u, 0.0)).astype(BF16)

    def down(i):
        mixed[i] = x_ref[rows(i), :] + jnp.dot(act.pop(i), wdown_ref[...], preferred_element_type=F32)

    def final(i):
        out_ref[rows(i), :] = _rmsnorm(mixed.pop(i), gf_ref[...])

    n = len(subtiles)
    norm(0)
    up(0)
    for i in range(n):
        if i + 1 < n:
            norm(i + 1)
        down(i)
        if i + 1 < n:
            up(i + 1)
        final(i)


def _resident(shape):
    return pl.BlockSpec(shape, lambda *_: (0,) * len(shape), pipeline_mode=pl.Buffered(1))


def _mixer(x, g, w_in, conv_w, w_conv_out, sinks, rel_bias, w_attn_out, w_o, w_up, w_down):
    batch, seq, _ = x.shape
    tile = TILE_MIX
    assert seq % tile == 0 and tile % BLOCK == 0
    tiles = seq // tile
    steps = batch * tiles
    up_rows, down_rows = D_MODEL // steps, D_FF // steps
    assert up_rows * steps == D_MODEL and up_rows % BF16_SUBLANES == 0
    assert down_rows * steps == D_FF and down_rows % BF16_SUBLANES == 0
    assert IN_COLS % STAGE_COLS == 0 and D_MODEL % STAGE_COLS == 0

    def tile_of(s):
        return jnp.maximum(s - 1, 0)

    def x_map(s):
        return (tile_of(s) // tiles, tile_of(s) % tiles, 0)

    def slice_map(s):
        return (tile_of(s), 0)

    hbm = pl.BlockSpec(memory_space=pl.ANY)
    return pl.pallas_call(
        functools.partial(_mixer_kernel, tile=tile, tiles=tiles),
        grid=(steps + 1,),
        in_specs=[
            pl.BlockSpec((1, tile, D_MODEL), x_map),
            _resident((1, D_MODEL)),
            hbm,
            _resident((CONV_WIDTH, D_MODEL)),
            hbm,
            pl.BlockSpec(memory_space=pltpu.SMEM),
            _resident((2 * BLOCK, BLOCK)),
            pl.BlockSpec(memory_space=pltpu.SMEM),
            hbm,
            hbm,
            pl.BlockSpec((up_rows, D_FF), slice_map),
            pl.BlockSpec((down_rows, D_MODEL), slice_map),
        ],
        out_specs=[
            pl.BlockSpec((1, tile, D_MODEL), x_map),
            pl.BlockSpec((up_rows, D_FF), slice_map),
            pl.BlockSpec((down_rows, D_MODEL), slice_map),
        ],
        out_shape=[
            jax.ShapeDtypeStruct(x.shape, F32),
            jax.ShapeDtypeStruct(w_up.shape, BF16),
            jax.ShapeDtypeStruct(w_down.shape, BF16),
        ],
        scratch_shapes=[
            pltpu.VMEM((D_MODEL, IN_COLS), BF16),
            pltpu.VMEM((D_MODEL, D_MODEL), BF16),
            pltpu.VMEM((D_MODEL, D_MODEL), BF16),
            pltpu.VMEM((D_MODEL, D_MODEL), BF16),
            pltpu.VMEM((STAGE_SLOTS, D_MODEL, STAGE_COLS), F32),
            pltpu.SemaphoreType.DMA((STAGE_SLOTS,)),
            pltpu.VMEM((2, N_HEADS, 2 * BLOCK, BLOCK), F32),
            pltpu.VMEM((tile + 2 * SUBLANES, D_MODEL), F32),
            pltpu.VMEM((tile + BLOCK, 4 * KV_COLS), BF16),
            pltpu.VMEM((D_MODEL, tile), BF16),
        ],
        compiler_params=pltpu.CompilerParams(
            dimension_semantics=("arbitrary",), vmem_limit_bytes=VMEM_LIMIT),
        name="token_mixer",
    )(x, g, w_in, conv_w, w_conv_out, sinks, jnp.asarray(_bucket_table()), rel_bias, w_attn_out, w_o, w_up, w_down)


def _mlp(x, g, w_up, w_down, g_final):
    rows = x.shape[0]
    tile = sum(SUBTILES_MLP)
    assert rows % tile == 0 and all(s % BF16_SUBLANES == 0 for s in SUBTILES_MLP)
    return pl.pallas_call(
        functools.partial(_mlp_kernel, subtiles=SUBTILES_MLP),
        grid=(rows // tile,),
        in_specs=[
            pl.BlockSpec((tile, D_MODEL), lambda i: (i, 0)),
            _resident((1, D_MODEL)),
            _resident((D_MODEL, D_FF)),
            _resident((D_FF, D_MODEL)),
            _resident((1, D_MODEL)),
        ],
        out_specs=pl.BlockSpec((tile, D_MODEL), lambda i: (i, 0)),
        out_shape=jax.ShapeDtypeStruct(x.shape, F32),
        compiler_params=pltpu.CompilerParams(
            dimension_semantics=("arbitrary",), vmem_limit_bytes=VMEM_LIMIT),
        name="channel_mixer",
    )(x, g, w_up, w_down, g_final)


def kernel(x, attn_norm_g, w_in, conv_w, w_conv_out, attn_sinks, rel_bias, w_attn_out, w_o, mlp_norm_g, w_up,
           w_down, final_norm_g):
    batch, seq, _ = x.shape
    depth = w_in.shape[0]
    assert depth == 1, "the final norm is fused into the (only) layer's channel mixer"
    x, w_up_bf16, w_down_bf16 = _mixer(
        x, attn_norm_g.reshape(1, D_MODEL), w_in.reshape(D_MODEL, IN_COLS), conv_w.reshape(CONV_WIDTH, D_MODEL),
        w_conv_out.reshape(D_MODEL, D_MODEL), attn_sinks.reshape(N_HEADS).astype(F32), rel_bias.astype(F32),
        w_attn_out.reshape(D_MODEL, D_MODEL), w_o.reshape(D_MODEL, D_MODEL), w_up.reshape(D_MODEL, D_FF),
        w_down.reshape(D_FF, D_MODEL))
    x = _mlp(x.reshape(batch * seq, D_MODEL), mlp_norm_g.reshape(1, D_MODEL), w_up_bf16, w_down_bf16,
             final_norm_g.reshape(1, D_MODEL))
    return x.reshape(batch, seq, D_MODEL)
```

```python
import functools
import math

import jax
import jax.numpy as jnp
import numpy as np
from jax import lax
from jax.experimental import pallas as pl
from jax.experimental.pallas import tpu as pltpu

D_MODEL = 1024
HEAD_DIM = 64
N_HEADS = 16
N_KV_HEADS = 2
GROUP = N_HEADS // N_KV_HEADS
BLOCK = 128
N_BUCKETS = 32
MAX_DISTANCE = 128
CONV_WIDTH = 3
D_FF = 4 * D_MODEL
EPS = 1e-6
NEG_INF = -1e30
LOG2E = math.log2(math.e)

LANES = 128
SUBLANES = 8
BF16_SUBLANES = 16
KV_COLS = N_KV_HEADS * HEAD_DIM
PAIRS_PER_GROUP = GROUP // 2

OFF_B, OFF_C, OFF_X, OFF_Q = 0, D_MODEL, 2 * D_MODEL, 3 * D_MODEL
OFF_K = 4 * D_MODEL
OFF_V = OFF_K + KV_COLS
OFF_GC = OFF_V + KV_COLS
OFF_GA = OFF_GC + D_MODEL
IN_COLS = OFF_GA + D_MODEL

TILE_MIX = 512
SUBTILES_MLP = (512, 512)
VMEM_LIMIT = 56 * 1024 * 1024
STAGE_COLS = 256
STAGE_SLOTS = 4

BF16 = jnp.bfloat16
F32 = jnp.float32

NT_DIMS = (((1,), (1,)), ((), ()))
TN_DIMS = (((0,), (0,)), ((), ()))


def _bucket_table():
    kj = np.arange(2 * BLOCK)[:, None]
    qi = np.arange(BLOCK)[None, :]
    dist = qi + BLOCK - kj
    n = np.maximum(dist, 0)
    max_exact = N_BUCKETS // 2
    ratio = np.log(np.maximum(n, max_exact).astype(np.float32) / max_exact) / np.log(MAX_DISTANCE / max_exact)
    large = np.minimum(max_exact + (ratio * (N_BUCKETS - max_exact)).astype(np.int32), N_BUCKETS - 1)
    bucket = np.where(n < max_exact, n, large).astype(np.int32)
    in_window = (dist >= 0) & (dist < BLOCK)
    return np.where(in_window, bucket, -1).astype(np.int32)


def _bias_table(head, bucket_ref, rel_ref, bias_buf):
    bucket = bucket_ref[...]
    acc = jnp.full(bucket.shape, NEG_INF, F32)
    for b in range(N_BUCKETS):
        acc = jnp.where(bucket == b, rel_ref[b, head], acc)
    acc = acc * LOG2E
    bias_buf[0, head] = acc
    key = lax.broadcasted_iota(jnp.int32, bucket.shape, 0)
    bias_buf[1, head] = jnp.where(key >= BLOCK, acc, NEG_INF)


def _rmsnorm(x, g):
    y = x * lax.rsqrt(jnp.mean(x * x, axis=-1, keepdims=True) + EPS)
    return y * g


def _stage_weights(pairs, stage, sems, fillers):
    chunks = [(src, dst, c) for src, dst in pairs for c in range(0, src.shape[1], STAGE_COLS)]
    slots = stage.shape[0]

    def copy(i):
        src, _, c = chunks[i]
        return pltpu.make_async_copy(src.at[:, c:c + STAGE_COLS], stage.at[i % slots], sems.at[i % slots])

    for i in range(min(slots, len(chunks))):
        copy(i).start()
    fillers = list(fillers)
    assert len(fillers) <= len(chunks)
    for i, (_, dst, c) in enumerate(chunks):
        if fillers:
            fillers.pop(0)()
        copy(i).wait()
        dst[:, c:c + STAGE_COLS] = stage[i % slots].astype(BF16)
        if i + slots < len(chunks):
            copy(i + slots).start()


def _mixer_kernel(x_ref, g_ref, win_hbm, convw_ref, wco_hbm, sinks_ref, bucket_ref, rel_ref, wao_hbm, wo_hbm,
                  wup_ref, wdown_ref, out_ref, wup_out_ref, wdown_out_ref,
                  win_buf, wco_buf, wao_buf, wo_buf, stage, stage_sems, bias_buf, cx_buf, kv_buf, attn_buf,
                  *, tile, tiles):
    s = pl.program_id(0)

    @pl.when(s == 0)
    def _():
        bias_tasks = [functools.partial(_bias_table, head, bucket_ref, rel_ref, bias_buf)
                      for head in range(N_HEADS)]
        _stage_weights([(win_hbm, win_buf), (wco_hbm, wco_buf), (wao_hbm, wao_buf), (wo_hbm, wo_buf)],
                       stage, stage_sems, bias_tasks)

    @pl.when(s > 0)
    def _():
        _mixer_step((s - 1) % tiles, x_ref, g_ref, win_buf, convw_ref, wco_buf, sinks_ref, bias_buf, wao_buf,
                    wo_buf, wup_ref, wdown_ref, out_ref, wup_out_ref, wdown_out_ref, cx_buf, kv_buf, attn_buf,
                    tile=tile)


def _mixer_step(t, x_ref, g_ref, win_ref, convw_ref, wco_ref, sinks_ref, bias_ref, wao_ref, wo_ref,
                wup_ref, wdown_ref, out_ref, wup_out_ref, wdown_out_ref, cx_buf, kv_buf, attn_buf, *, tile):
    @pl.when(t == 0)
    def _():
        cx_buf[0:SUBLANES, :] = jnp.zeros((SUBLANES, D_MODEL), F32)
        kv_buf[0:BLOCK, :] = jnp.zeros((BLOCK, 4 * KV_COLS), BF16)

    x = x_ref[0]
    half = tile // 2
    h_halves = [_rmsnorm(x[r * half:(r + 1) * half], g_ref[...]).astype(BF16) for r in range(2)]
    h = jnp.concatenate(h_halves, axis=0)

    def proj(lo, width):
        return jnp.dot(h, win_ref[:, lo:lo + width], preferred_element_type=F32)

    def proj_halves(lo, width):
        w = win_ref[:, lo:lo + width]
        return jnp.concatenate([jnp.dot(hr, w, preferred_element_type=F32) for hr in h_halves], axis=0)

    q = (proj_halves(OFF_Q, D_MODEL) * (HEAD_DIM ** -0.5 * LOG2E)).astype(BF16)
    kv = proj_halves(OFF_K, 2 * KV_COLS)
    k = kv[:, 0:KV_COLS]
    val = kv[:, KV_COLS:2 * KV_COLS]
    k_swapped = pltpu.roll(k, HEAD_DIM, axis=1)
    low_half_tile = lax.broadcasted_iota(jnp.int32, (tile, LANES), 1) < HEAD_DIM
    new_rows = slice(BLOCK, BLOCK + tile)
    kv_buf[new_rows, 0:KV_COLS] = jnp.where(low_half_tile, k, k_swapped).astype(BF16)
    kv_buf[new_rows, KV_COLS:2 * KV_COLS] = jnp.where(low_half_tile, k_swapped, k).astype(BF16)
    kv_buf[new_rows, 2 * KV_COLS:3 * KV_COLS] = jnp.where(low_half_tile, val, 1.0).astype(BF16)
    kv_buf[new_rows, 3 * KV_COLS:4 * KV_COLS] = jnp.where(low_half_tile, 1.0, val).astype(BF16)

    low_half = lax.broadcasted_iota(jnp.int32, (BLOCK, LANES), 1) < HEAD_DIM
    zero = jnp.zeros((BLOCK, LANES), BF16)

    scores, probs, sink_terms = {}, {}, {}

    def qk(j):
        rows = slice(j * BLOCK, (j + 2) * BLOCK)
        qj = q[j * BLOCK:(j + 1) * BLOCK, :]
        for g in range(N_KV_HEADS):
            parts = []
            for p in range(g * PAIRS_PER_GROUP, (g + 1) * PAIRS_PER_GROUP):
                qp = qj[:, p * LANES:(p + 1) * LANES]
                parts.append(jnp.where(low_half, qp, zero))
                parts.append(jnp.where(low_half, zero, qp))
            q_heads = jnp.concatenate(parts, axis=0)
            keys = kv_buf[rows, g * KV_COLS:(g + 1) * KV_COLS]
            scores[j, g] = lax.dot_general(keys, q_heads, NT_DIMS, preferred_element_type=F32)

    def softmax(j, g):
        first = (t == 0).astype(jnp.int32) if j == 0 else 0
        s = scores.pop((j, g))
        for i in range(GROUP):
            head = g * GROUP + i
            sh = s[:, i * BLOCK:(i + 1) * BLOCK] + bias_ref[first, head]
            sink = jnp.full((1, BLOCK), sinks_ref[head], F32) * LOG2E
            m = jnp.maximum(jnp.max(sh, axis=0, keepdims=True), sink)
            probs[j, head] = jnp.exp2(sh - m).astype(BF16)
            sink_terms[j, head] = jnp.exp2(sink - m)

    def pv(j):
        rows = slice(j * BLOCK, (j + 2) * BLOCK)
        for g in range(N_KV_HEADS):
            vals = kv_buf[rows, (2 + g) * KV_COLS:(3 + g) * KV_COLS]
            p = jnp.concatenate([probs.pop((j, g * GROUP + i)) for i in range(GROUP)], axis=1)
            o = lax.dot_general(vals, p, TN_DIMS, preferred_element_type=F32)
            out_rows = slice(g * HEAD_DIM, (g + 1) * HEAD_DIM)
            sum_row = (1 - g) * HEAD_DIM
            for i in range(GROUP):
                head = g * GROUP + i
                cols = slice(i * BLOCK, (i + 1) * BLOCK)
                denom = o[sum_row:sum_row + 1, cols] + sink_terms.pop((j, head))
                oh = o[out_rows, cols] * (1.0 / denom)
                attn_buf[head * HEAD_DIM:(head + 1) * HEAD_DIM, j * BLOCK:(j + 1) * BLOCK] = oh.astype(BF16)

    v = {}

    def t_c():
        v["c"] = proj(OFF_C, D_MODEL)

    def t_x():
        cx_buf[SUBLANES:SUBLANES + tile, :] = v.pop("c") * proj(OFF_X, D_MODEL)

    def t_conv():
        v["conv"] = (convw_ref[0:1, :] * cx_buf[SUBLANES - 2:SUBLANES - 2 + tile, :]
                     + convw_ref[1:2, :] * cx_buf[SUBLANES - 1:SUBLANES - 1 + tile, :]
                     + convw_ref[2:3, :] * cx_buf[SUBLANES:SUBLANES + tile, :])
        cx_buf[0:SUBLANES, :] = cx_buf[tile:tile + SUBLANES, :]

    def t_b():
        v["u"] = (proj(OFF_B, D_MODEL) * v.pop("conv")).astype(BF16)

    def t_gc():
        v["gate_conv"] = jax.nn.sigmoid(proj(OFF_GC, D_MODEL))

    def t_ga():
        v["gate_attn"] = jax.nn.sigmoid(proj(OFF_GA, D_MODEL))

    def t_yconv():
        v["mixed"] = v.pop("gate_conv") * jnp.dot(v.pop("u"), wco_ref[...], preferred_element_type=F32)

    def t_yattn(r):
        rows = slice(r * half, (r + 1) * half)
        y_attn = lax.dot_general(attn_buf[:, rows], wao_ref[...], TN_DIMS, preferred_element_type=F32)
        v["mixed", r] = (v["mixed"][rows] + v["gate_attn"][rows] * y_attn).astype(BF16)

    def t_out(r):
        rows = slice(r * half, (r + 1) * half)
        out_ref[0, rows, :] = x[rows] + jnp.dot(v.pop(("mixed", r)), wo_ref[...], preferred_element_type=F32)

    qk(0)
    t_c()
    wup_out_ref[...] = wup_ref[...].astype(BF16)
    wdown_out_ref[...] = wdown_ref[...].astype(BF16)
    softmax(0, 0)
    t_x()
    softmax(0, 1)
    t_gc()
    t_conv()
    pv(0)
    qk(1)
    t_b()
    softmax(1, 0)
    t_ga()
    softmax(1, 1)
    pv(1)
    qk(2)
    t_yconv()
    softmax(2, 0)
    t_yattn(0)
    softmax(2, 1)
    pv(2)
    qk(3)
    t_out(0)
    softmax(3, 0)
    softmax(3, 1)
    pv(3)
    kv_buf[0:BLOCK, :] = kv_buf[tile:tile + BLOCK, :]
    t_yattn(1)
    t_out(1)


def _mlp_kernel(x_ref, g_ref, wup_ref, wdown_ref, gf_ref, out_ref, *, subtiles):
    normed, act, mixed = {}, {}, {}
    starts = [sum(subtiles[:i]) for i in range(len(subtiles))]

    def rows(i):
        return slice(starts[i], starts[i] + subtiles[i])

    def norm(i):
        normed[i] = _rmsnorm(x_ref[rows(i), :], g_ref[...]).astype(BF16)

    def up(i):
        u = jnp.dot(normed.pop(i), wup_ref[...], preferred_element_type=F32)
        act[i] = jnp.square(jnp.maximum(u, 0.0)).astype(BF16)

    def down(i):
        mixed[i] = x_ref[rows(i), :] + jnp.dot(act.pop(i), wdown_ref[...], preferred_element_type=F32)

    def final(i):
        out_ref[rows(i), :] = _rmsnorm(mixed.pop(i), gf_ref[...])

    n = len(subtiles)
    norm(0)
    up(0)
    for i in range(n):
        if i + 1 < n:
            norm(i + 1)
            up(i + 1)
        if i > 0:
            final(i - 1)
        down(i)
    final(n - 1)


def _resident(shape):
    return pl.BlockSpec(shape, lambda *_: (0,) * len(shape), pipeline_mode=pl.Buffered(1))


def _mixer(x, g, w_in, conv_w, w_conv_out, sinks, rel_bias, w_attn_out, w_o, w_up, w_down):
    batch, seq, _ = x.shape
    tile = TILE_MIX
    assert seq % tile == 0 and tile % BLOCK == 0
    tiles = seq // tile
    steps = batch * tiles
    up_rows, down_rows = D_MODEL // steps, D_FF // steps
    assert up_rows * steps == D_MODEL and up_rows % BF16_SUBLANES == 0
    assert down_rows * steps == D_FF and down_rows % BF16_SUBLANES == 0
    assert IN_COLS % STAGE_COLS == 0 and D_MODEL % STAGE_COLS == 0

    def tile_of(s):
        return jnp.maximum(s - 1, 0)

    def x_map(s):
        return (tile_of(s) // tiles, tile_of(s) % tiles, 0)

    def slice_map(s):
        return (tile_of(s), 0)

    hbm = pl.BlockSpec(memory_space=pl.ANY)
    return pl.pallas_call(
        functools.partial(_mixer_kernel, tile=tile, tiles=tiles),
        grid=(steps + 1,),
        in_specs=[
            pl.BlockSpec((1, tile, D_MODEL), x_map),
            _resident((1, D_MODEL)),
            hbm,
            _resident((CONV_WIDTH, D_MODEL)),
            hbm,
            pl.BlockSpec(memory_space=pltpu.SMEM),
            _resident((2 * BLOCK, BLOCK)),
            pl.BlockSpec(memory_space=pltpu.SMEM),
            hbm,
            hbm,
            pl.BlockSpec((up_rows, D_FF), slice_map),
            pl.BlockSpec((down_rows, D_MODEL), slice_map),
        ],
        out_specs=[
            pl.BlockSpec((1, tile, D_MODEL), x_map),
            pl.BlockSpec((up_rows, D_FF), slice_map),
            pl.BlockSpec((down_rows, D_MODEL), slice_map),
        ],
        out_shape=[
            jax.ShapeDtypeStruct(x.shape, F32),
            jax.ShapeDtypeStruct(w_up.shape, BF16),
            jax.ShapeDtypeStruct(w_down.shape, BF16),
        ],
        scratch_shapes=[
            pltpu.VMEM((D_MODEL, IN_COLS), BF16),
            pltpu.VMEM((D_MODEL, D_MODEL), BF16),
            pltpu.VMEM((D_MODEL, D_MODEL), BF16),
            pltpu.VMEM((D_MODEL, D_MODEL), BF16),
            pltpu.VMEM((STAGE_SLOTS, D_MODEL, STAGE_COLS), F32),
            pltpu.SemaphoreType.DMA((STAGE_SLOTS,)),
            pltpu.VMEM((2, N_HEADS, 2 * BLOCK, BLOCK), F32),
            pltpu.VMEM((tile + 2 * SUBLANES, D_MODEL), F32),
            pltpu.VMEM((tile + BLOCK, 4 * KV_COLS), BF16),
            pltpu.VMEM((D_MODEL, tile), BF16),
        ],
        compiler_params=pltpu.CompilerParams(
            dimension_semantics=("arbitrary",), vmem_limit_bytes=VMEM_LIMIT),
        name="token_mixer",
    )(x, g, w_in, conv_w, w_conv_out, sinks, jnp.asarray(_bucket_table()), rel_bias, w_attn_out, w_o, w_up, w_down)


def _mlp(x, g, w_up, w_down, g_final):
    rows = x.shape[0]
    tile = sum(SUBTILES_MLP)
    assert rows % tile == 0 and all(s % BF16_SUBLANES == 0 for s in SUBTILES_MLP)
    return pl.pallas_call(
        functools.partial(_mlp_kernel, subtiles=SUBTILES_MLP),
        grid=(rows // tile,),
        in_specs=[
            pl.BlockSpec((tile, D_MODEL), lambda i: (i, 0)),
            _resident((1, D_MODEL)),
            _resident((D_MODEL, D_FF)),
            _resident((D_FF, D_MODEL)),
            _resident((1, D_MODEL)),
        ],
        out_specs=pl.BlockSpec((tile, D_MODEL), lambda i: (i, 0)),
        out_shape=jax.ShapeDtypeStruct(x.shape, F32),
        compiler_params=pltpu.CompilerParams(
            dimension_semantics=("arbitrary",), vmem_limit_bytes=VMEM_LIMIT),
        name="channel_mixer",
    )(x, g, w_up, w_down, g_final)


def kernel(x, attn_norm_g, w_in, conv_w, w_conv_out, attn_sinks, rel_bias, w_attn_out, w_o, mlp_norm_g, w_up,
           w_down, final_norm_g):
    batch, seq, _ = x.shape
    depth = w_in.shape[0]
    assert depth == 1, "the final norm is fused into the (only) layer's channel mixer"
    x, w_up_bf16, w_down_bf16 = _mixer(
        x, attn_norm_g.reshape(1, D_MODEL), w_in.reshape(D_MODEL, IN_COLS), conv_w.reshape(CONV_WIDTH, D_MODEL),
        w_conv_out.reshape(D_MODEL, D_MODEL), attn_sinks.reshape(N_HEADS).astype(F32), rel_bias.astype(F32),
        w_attn_out.reshape(D_MODEL, D_MODEL), w_o.reshape(D_MODEL, D_MODEL), w_up.reshape(D_MODEL, D_FF),
        w_down.reshape(D_FF, D_MODEL))
    x = _mlp(x.reshape(batch * seq, D_MODEL), mlp_norm_g.reshape(1, D_MODEL), w_up_bf16, w_down_bf16,
             final_norm_g.reshape(1, D_MODEL))
    return x.reshape(batch, seq, D_MODEL)
```

```python
import functools
import math

import jax
import jax.numpy as jnp
import numpy as np
from jax import lax
from jax.experimental import pallas as pl
from jax.experimental.pallas import tpu as pltpu

D_MODEL = 1024
HEAD_DIM = 64
N_HEADS = 16
N_KV_HEADS = 2
GROUP = N_HEADS // N_KV_HEADS
BLOCK = 128
N_BUCKETS = 32
MAX_DISTANCE = 128
CONV_WIDTH = 3
D_FF = 4 * D_MODEL
EPS = 1e-6
NEG_INF = -1e30
LOG2E = math.log2(math.e)

LANES = 128
SUBLANES = 8
BF16_SUBLANES = 16
KV_COLS = N_KV_HEADS * HEAD_DIM
PAIRS_PER_GROUP = GROUP // 2

OFF_B, OFF_C, OFF_X, OFF_Q = 0, D_MODEL, 2 * D_MODEL, 3 * D_MODEL
OFF_K = 4 * D_MODEL
OFF_V = OFF_K + KV_COLS
OFF_GC = OFF_V + KV_COLS
OFF_GA = OFF_GC + D_MODEL
IN_COLS = OFF_GA + D_MODEL

TILE_MIX = 512
SUBTILES_MLP = (512, 512)
VMEM_LIMIT = 56 * 1024 * 1024
STAGE_COLS = 640
STAGE_SLOTS = 3

BF16 = jnp.bfloat16
F32 = jnp.float32

NT_DIMS = (((1,), (1,)), ((), ()))
TN_DIMS = (((0,), (0,)), ((), ()))


def _bucket_table():
    kj = np.arange(2 * BLOCK)[:, None]
    qi = np.arange(BLOCK)[None, :]
    dist = qi + BLOCK - kj
    n = np.maximum(dist, 0)
    max_exact = N_BUCKETS // 2
    ratio = np.log(np.maximum(n, max_exact).astype(np.float32) / max_exact) / np.log(MAX_DISTANCE / max_exact)
    large = np.minimum(max_exact + (ratio * (N_BUCKETS - max_exact)).astype(np.int32), N_BUCKETS - 1)
    bucket = np.where(n < max_exact, n, large).astype(np.int32)
    in_window = (dist >= 0) & (dist < BLOCK)
    return np.where(in_window, bucket, -1).astype(np.int32)


def _bias_table(head, bucket_ref, rel_ref, bias_buf):
    bucket = bucket_ref[...]
    acc = jnp.full(bucket.shape, NEG_INF, F32)
    for b in range(N_BUCKETS):
        acc = jnp.where(bucket == b, rel_ref[b, head], acc)
    acc = acc * LOG2E
    bias_buf[0, head] = acc
    key = lax.broadcasted_iota(jnp.int32, bucket.shape, 0)
    bias_buf[1, head] = jnp.where(key >= BLOCK, acc, NEG_INF)


def _rmsnorm(x, g):
    y = x * lax.rsqrt(jnp.mean(x * x, axis=-1, keepdims=True) + EPS)
    return y * g


def _stage_weights(pairs, stage, sems, fillers):
    chunks = []
    for src, dst in pairs:
        cols = src.shape[1]
        width = max(w for w in range(LANES, STAGE_COLS + 1, LANES) if cols % w == 0)
        chunks += [(src, dst, c, width) for c in range(0, cols, width)]
    slots = stage.shape[0]

    def copy(i):
        src, _, c, width = chunks[i]
        return pltpu.make_async_copy(src.at[:, c:c + width], stage.at[i % slots, :, 0:width], sems.at[i % slots])

    for i in range(min(slots, len(chunks))):
        copy(i).start()
    fillers = list(fillers)
    assert len(fillers) <= len(chunks)
    for i, (_, dst, c, width) in enumerate(chunks):
        if fillers:
            fillers.pop(0)()
        copy(i).wait()
        dst[:, c:c + width] = stage[i % slots, :, 0:width].astype(BF16)
        if i + slots < len(chunks):
            copy(i + slots).start()


def _mixer_kernel(x_ref, g_ref, win_hbm, convw_ref, wco_hbm, sinks_ref, bucket_ref, rel_ref, wao_hbm, wo_hbm,
                  wup_ref, wdown_ref, out_ref, wup_out_ref, wdown_out_ref,
                  win_buf, wco_buf, wao_buf, wo_buf, stage, stage_sems, bias_buf, cx_buf, kv_buf, attn_buf,
                  *, tile, tiles):
    s = pl.program_id(0)

    @pl.when(s == 0)
    def _():
        bias_tasks = [functools.partial(_bias_table, head, bucket_ref, rel_ref, bias_buf)
                      for head in range(N_HEADS)]
        _stage_weights([(win_hbm, win_buf), (wco_hbm, wco_buf), (wao_hbm, wao_buf), (wo_hbm, wo_buf)],
                       stage, stage_sems, bias_tasks)

    @pl.when(s > 0)
    def _():
        _mixer_step((s - 1) % tiles, x_ref, g_ref, win_buf, convw_ref, wco_buf, sinks_ref, bias_buf, wao_buf,
                    wo_buf, wup_ref, wdown_ref, out_ref, wup_out_ref, wdown_out_ref, cx_buf, kv_buf, attn_buf,
                    tile=tile)


def _mixer_step(t, x_ref, g_ref, win_ref, convw_ref, wco_ref, sinks_ref, bias_ref, wao_ref, wo_ref,
                wup_ref, wdown_ref, out_ref, wup_out_ref, wdown_out_ref, cx_buf, kv_buf, attn_buf, *, tile):
    @pl.when(t == 0)
    def _():
        cx_buf[:, 0:SUBLANES, :] = jnp.zeros((D_MODEL // LANES, SUBLANES, LANES), F32)
        kv_buf[0:BLOCK, :] = jnp.zeros((BLOCK, 4 * KV_COLS), BF16)

    x = x_ref[0]
    half = tile // 2
    h_halves = [_rmsnorm(x[r * half:(r + 1) * half], g_ref[...]).astype(BF16) for r in range(2)]
    h = jnp.concatenate(h_halves, axis=0)

    def proj(lo, width):
        return jnp.dot(h, win_ref[:, lo:lo + width], preferred_element_type=F32)

    def proj_halves(lo, width):
        w = win_ref[:, lo:lo + width]
        return jnp.concatenate([jnp.dot(hr, w, preferred_element_type=F32) for hr in h_halves], axis=0)

    q = (proj_halves(OFF_Q, D_MODEL) * (HEAD_DIM ** -0.5 * LOG2E)).astype(BF16)
    kv = proj_halves(OFF_K, 2 * KV_COLS)
    k = kv[:, 0:KV_COLS]
    val = kv[:, KV_COLS:2 * KV_COLS]
    k_swapped = pltpu.roll(k, HEAD_DIM, axis=1)
    low_half_tile = lax.broadcasted_iota(jnp.int32, (tile, LANES), 1) < HEAD_DIM
    new_rows = slice(BLOCK, BLOCK + tile)
    kv_buf[new_rows, 0:KV_COLS] = jnp.where(low_half_tile, k, k_swapped).astype(BF16)
    kv_buf[new_rows, KV_COLS:2 * KV_COLS] = jnp.where(low_half_tile, k_swapped, k).astype(BF16)
    kv_buf[new_rows, 2 * KV_COLS:3 * KV_COLS] = jnp.where(low_half_tile, val, 1.0).astype(BF16)
    kv_buf[new_rows, 3 * KV_COLS:4 * KV_COLS] = jnp.where(low_half_tile, 1.0, val).astype(BF16)

    low_half = lax.broadcasted_iota(jnp.int32, (BLOCK, LANES), 1) < HEAD_DIM
    zero = jnp.zeros((BLOCK, LANES), BF16)

    scores, probs, sink_terms = {}, {}, {}

    def qk(j):
        rows = slice(j * BLOCK, (j + 2) * BLOCK)
        qj = q[j * BLOCK:(j + 1) * BLOCK, :]
        for g in range(N_KV_HEADS):
            parts = []
            for p in range(g * PAIRS_PER_GROUP, (g + 1) * PAIRS_PER_GROUP):
                qp = qj[:, p * LANES:(p + 1) * LANES]
                parts.append(jnp.where(low_half, qp, zero))
                parts.append(jnp.where(low_half, zero, qp))
            q_heads = jnp.concatenate(parts, axis=0)
            keys = kv_buf[rows, g * KV_COLS:(g + 1) * KV_COLS]
            scores[j, g] = lax.dot_general(keys, q_heads, NT_DIMS, preferred_element_type=F32)

    def softmax(j, g):
        first = (t == 0).astype(jnp.int32) if j == 0 else 0
        s = scores.pop((j, g))
        for i in range(GROUP):
            head = g * GROUP + i
            sh = s[:, i * BLOCK:(i + 1) * BLOCK] + bias_ref[first, head]
            sink = jnp.full((1, BLOCK), sinks_ref[head], F32) * LOG2E
            m = jnp.maximum(jnp.max(sh, axis=0, keepdims=True), sink)
            probs[j, head] = jnp.exp2(sh - m).astype(BF16)
            sink_terms[j, head] = jnp.exp2(sink - m)

    def pv(j):
        rows = slice(j * BLOCK, (j + 2) * BLOCK)
        for g in range(N_KV_HEADS):
            vals = kv_buf[rows, (2 + g) * KV_COLS:(3 + g) * KV_COLS]
            p = jnp.concatenate([probs.pop((j, g * GROUP + i)) for i in range(GROUP)], axis=1)
            o = lax.dot_general(vals, p, TN_DIMS, preferred_element_type=F32)
            out_rows = slice(g * HEAD_DIM, (g + 1) * HEAD_DIM)
            sum_row = (1 - g) * HEAD_DIM
            for i in range(GROUP):
                head = g * GROUP + i
                cols = slice(i * BLOCK, (i + 1) * BLOCK)
                denom = o[sum_row:sum_row + 1, cols] + sink_terms.pop((j, head))
                oh = o[out_rows, cols] * (1.0 / denom)
                attn_buf[head * HEAD_DIM:(head + 1) * HEAD_DIM, j * BLOCK:(j + 1) * BLOCK] = oh.astype(BF16)

    v = {}

    def t_c():
        v["c"] = proj(OFF_C, D_MODEL)

    def t_x():
        cx = v.pop("c") * proj(OFF_X, D_MODEL)
        for c in range(D_MODEL // LANES):
            cx_buf[c, SUBLANES:SUBLANES + tile, :] = cx[:, c * LANES:(c + 1) * LANES]

    def t_conv():
        slabs = []
        for c in range(D_MODEL // LANES):
            cols = slice(c * LANES, (c + 1) * LANES)
            slabs.append(convw_ref[0:1, cols] * cx_buf[c, SUBLANES - 2:SUBLANES - 2 + tile, :]
                         + convw_ref[1:2, cols] * cx_buf[c, SUBLANES - 1:SUBLANES - 1 + tile, :]
                         + convw_ref[2:3, cols] * cx_buf[c, SUBLANES:SUBLANES + tile, :])
        v["conv"] = jnp.concatenate(slabs, axis=1)
        cx_buf[:, 0:SUBLANES, :] = cx_buf[:, tile:tile + SUBLANES, :]

    def t_b():
        v["u"] = (proj(OFF_B, D_MODEL) * v.pop("conv")).astype(BF16)

    def t_gc():
        v["gate_conv"] = jax.nn.sigmoid(proj(OFF_GC, D_MODEL))

    def t_ga():
        v["gate_attn"] = jax.nn.sigmoid(proj(OFF_GA, D_MODEL))

    def t_yconv():
        v["mixed"] = v.pop("gate_conv") * jnp.dot(v.pop("u"), wco_ref[...], preferred_element_type=F32)

    def t_yattn(r):
        rows = slice(r * half, (r + 1) * half)
        y_attn = lax.dot_general(attn_buf[:, rows], wao_ref[...], TN_DIMS, preferred_element_type=F32)
        v["mixed", r] = (v["mixed"][rows] + v["gate_attn"][rows] * y_attn).astype(BF16)

    def t_out(r):
        rows = slice(r * half, (r + 1) * half)
        out_ref[0, rows, :] = x[rows] + jnp.dot(v.pop(("mixed", r)), wo_ref[...], preferred_element_type=F32)

    qk(0)
    t_c()
    wup_out_ref[...] = wup_ref[...].astype(BF16)
    wdown_out_ref[...] = wdown_ref[...].astype(BF16)
    softmax(0, 0)
    t_x()
    softmax(0, 1)
    t_gc()
    t_conv()
    pv(0)
    qk(1)
    t_b()
    softmax(1, 0)
    t_ga()
    softmax(1, 1)
    pv(1)
    qk(2)
    t_yconv()
    softmax(2, 0)
    t_yattn(0)
    softmax(2, 1)
    pv(2)
    qk(3)
    t_out(0)
    softmax(3, 0)
    softmax(3, 1)
    pv(3)
    kv_buf[0:BLOCK, :] = kv_buf[tile:tile + BLOCK, :]
    t_yattn(1)
    t_out(1)


def _mlp_kernel(x_ref, g_ref, wup_ref, wdown_ref, gf_ref, out_ref, *, subtiles):
    normed, act, mixed = {}, {}, {}
    starts = [sum(subtiles[:i]) for i in range(len(subtiles))]

    def rows(i):
        return slice(starts[i], starts[i] + subtiles[i])

    def norm(i):
        normed[i] = _rmsnorm(x_ref[rows(i), :], g_ref[...]).astype(BF16)

    def up(i):
        u = jnp.dot(normed.pop(i), wup_ref[...], preferred_element_type=F32)
        act[i] = jnp.square(jnp.maximum(u, 0.0)).astype(BF16)

    def down(i):
        mixed[i] = x_ref[rows(i), :] + jnp.dot(act.pop(i), wdown_ref[...], preferred_element_type=F32)

    def final(i):
        out_ref[rows(i), :] = _rmsnorm(mixed.pop(i), gf_ref[...])

    n = len(subtiles)
    norm(0)
    up(0)
    for i in range(n):
        if i + 1 < n:
            norm(i + 1)
            up(i + 1)
        if i > 0:
            final(i - 1)
        down(i)
    final(n - 1)


def _resident(shape):
    return pl.BlockSpec(shape, lambda *_: (0,) * len(shape), pipeline_mode=pl.Buffered(1))


def _mixer(x, g, w_in, conv_w, w_conv_out, sinks, rel_bias, w_attn_out, w_o, w_up, w_down):
    batch, seq, _ = x.shape
    tile = TILE_MIX
    assert seq % tile == 0 and tile % BLOCK == 0
    tiles = seq // tile
    steps = batch * tiles
    up_rows, down_rows = D_MODEL // steps, D_FF // steps
    assert up_rows * steps == D_MODEL and up_rows % BF16_SUBLANES == 0
    assert down_rows * steps == D_FF and down_rows % BF16_SUBLANES == 0

    def tile_of(s):
        return jnp.maximum(s - 1, 0)

    def x_map(s):
        return (tile_of(s) // tiles, tile_of(s) % tiles, 0)

    def slice_map(s):
        return (tile_of(s), 0)

    hbm = pl.BlockSpec(memory_space=pl.ANY)
    return pl.pallas_call(
        functools.partial(_mixer_kernel, tile=tile, tiles=tiles),
        grid=(steps + 1,),
        in_specs=[
            pl.BlockSpec((1, tile, D_MODEL), x_map),
            _resident((1, D_MODEL)),
            hbm,
            _resident((CONV_WIDTH, D_MODEL)),
            hbm,
            pl.BlockSpec(memory_space=pltpu.SMEM),
            _resident((2 * BLOCK, BLOCK)),
            pl.BlockSpec(memory_space=pltpu.SMEM),
            hbm,
            hbm,
            pl.BlockSpec((up_rows, D_FF), slice_map),
            pl.BlockSpec((down_rows, D_MODEL), slice_map),
        ],
        out_specs=[
            pl.BlockSpec((1, tile, D_MODEL), x_map),
            pl.BlockSpec((up_rows, D_FF), slice_map),
            pl.BlockSpec((down_rows, D_MODEL), slice_map),
        ],
        out_shape=[
            jax.ShapeDtypeStruct(x.shape, F32),
            jax.ShapeDtypeStruct(w_up.shape, BF16),
            jax.ShapeDtypeStruct(w_down.shape, BF16),
        ],
        scratch_shapes=[
            pltpu.VMEM((D_MODEL, IN_COLS), BF16),
            pltpu.VMEM((D_MODEL, D_MODEL), BF16),
            pltpu.VMEM((D_MODEL, D_MODEL), BF16),
            pltpu.VMEM((D_MODEL, D_MODEL), BF16),
            pltpu.VMEM((STAGE_SLOTS, D_MODEL, STAGE_COLS), F32),
            pltpu.SemaphoreType.DMA((STAGE_SLOTS,)),
            pltpu.VMEM((2, N_HEADS, 2 * BLOCK, BLOCK), F32),
            pltpu.VMEM((D_MODEL // LANES, tile + 2 * SUBLANES, LANES), F32),
            pltpu.VMEM((tile + BLOCK, 4 * KV_COLS), BF16),
            pltpu.VMEM((D_MODEL, tile), BF16),
        ],
        compiler_params=pltpu.CompilerParams(
            dimension_semantics=("arbitrary",), vmem_limit_bytes=VMEM_LIMIT),
        name="token_mixer",
    )(x, g, w_in, conv_w, w_conv_out, sinks, jnp.asarray(_bucket_table()), rel_bias, w_attn_out, w_o, w_up, w_down)


def _mlp(x, g, w_up, w_down, g_final):
    rows = x.shape[0]
    tile = sum(SUBTILES_MLP)
    assert rows % tile == 0 and all(s % BF16_SUBLANES == 0 for s in SUBTILES_MLP)
    return pl.pallas_call(
        functools.partial(_mlp_kernel, subtiles=SUBTILES_MLP),
        grid=(rows // tile,),
        in_specs=[
            pl.BlockSpec((tile, D_MODEL), lambda i: (i, 0)),
            _resident((1, D_MODEL)),
            _resident((D_MODEL, D_FF)),
            _resident((D_FF, D_MODEL)),
            _resident((1, D_MODEL)),
        ],
        out_specs=pl.BlockSpec((tile, D_MODEL), lambda i: (i, 0)),
        out_shape=jax.ShapeDtypeStruct(x.shape, F32),
        compiler_params=pltpu.CompilerParams(
            dimension_semantics=("arbitrary",), vmem_limit_bytes=VMEM_LIMIT),
        name="channel_mixer",
    )(x, g, w_up, w_down, g_final)


def kernel(x, attn_norm_g, w_in, conv_w, w_conv_out, attn_sinks, rel_bias, w_attn_out, w_o, mlp_norm_g, w_up,
           w_down, final_norm_g):
    batch, seq, _ = x.shape
    depth = w_in.shape[0]
    assert depth == 1, "the final norm is fused into the (only) layer's channel mixer"
    x, w_up_bf16, w_down_bf16 = _mixer(
        x, attn_norm_g.reshape(1, D_MODEL), w_in.reshape(D_MODEL, IN_COLS), conv_w.reshape(CONV_WIDTH, D_MODEL),
        w_conv_out.reshape(D_MODEL, D_MODEL), attn_sinks.reshape(N_HEADS).astype(F32), rel_bias.astype(F32),
        w_attn_out.reshape(D_MODEL, D_MODEL), w_o.reshape(D_MODEL, D_MODEL), w_up.reshape(D_MODEL, D_FF),
        w_down.reshape(D_FF, D_MODEL))
    x = _mlp(x.reshape(batch * seq, D_MODEL), mlp_norm_g.reshape(1, D_MODEL), w_up_bf16, w_down_bf16,
             final_norm_g.reshape(1, D_MODEL))
    return x.reshape(batch, seq, D_MODEL)
```

```python
import functools
import math

import jax
import jax.numpy as jnp
import numpy as np
from jax import lax
from jax.experimental import pallas as pl
from jax.experimental.pallas import tpu as pltpu

D_MODEL = 1024
HEAD_DIM = 64
N_HEADS = 16
N_KV_HEADS = 2
GROUP = N_HEADS // N_KV_HEADS
BLOCK = 128
N_BUCKETS = 32
MAX_DISTANCE = 128
CONV_WIDTH = 3
D_FF = 4 * D_MODEL
EPS = 1e-6
NEG_INF = -1e30
LOG2E = math.log2(math.e)

LANES = 128
SUBLANES = 8
BF16_SUBLANES = 16
KV_COLS = N_KV_HEADS * HEAD_DIM
PAIRS_PER_GROUP = GROUP // 2

OFF_B, OFF_C, OFF_X, OFF_Q = 0, D_MODEL, 2 * D_MODEL, 3 * D_MODEL
OFF_K = 4 * D_MODEL
OFF_V = OFF_K + KV_COLS
OFF_GC = OFF_V + KV_COLS
OFF_GA = OFF_GC + D_MODEL
IN_COLS = OFF_GA + D_MODEL

TILE_MIX = 512
SUBTILES_MLP = (512, 512)
VMEM_LIMIT = 56 * 1024 * 1024
STAGE_COLS = 640
STAGE_SLOTS = 3

BF16 = jnp.bfloat16
F32 = jnp.float32

NT_DIMS = (((1,), (1,)), ((), ()))
TN_DIMS = (((0,), (0,)), ((), ()))


def _bucket_table():
    kj = np.arange(2 * BLOCK)[:, None]
    qi = np.arange(BLOCK)[None, :]
    dist = qi + BLOCK - kj
    n = np.maximum(dist, 0)
    max_exact = N_BUCKETS // 2
    ratio = np.log(np.maximum(n, max_exact).astype(np.float32) / max_exact) / np.log(MAX_DISTANCE / max_exact)
    large = np.minimum(max_exact + (ratio * (N_BUCKETS - max_exact)).astype(np.int32), N_BUCKETS - 1)
    bucket = np.where(n < max_exact, n, large).astype(np.int32)
    in_window = (dist >= 0) & (dist < BLOCK)
    return np.where(in_window, bucket, -1).astype(np.int32)


def _bias_table(head, bucket_ref, rel_ref, bias_buf):
    bucket = bucket_ref[...]
    acc = jnp.full(bucket.shape, NEG_INF, F32)
    for b in range(N_BUCKETS):
        acc = jnp.where(bucket == b, rel_ref[b, head], acc)
    acc = acc * LOG2E
    bias_buf[0, head] = acc
    key = lax.broadcasted_iota(jnp.int32, bucket.shape, 0)
    bias_buf[1, head] = jnp.where(key >= BLOCK, acc, NEG_INF)


def _rmsnorm(x, g):
    y = x * lax.rsqrt(jnp.mean(x * x, axis=-1, keepdims=True) + EPS)
    return y * g


def _stage_weights(pairs, stage, sems, fillers):
    chunks = []
    for src, dst in pairs:
        cols = src.shape[1]
        width = max(w for w in range(LANES, STAGE_COLS + 1, LANES) if cols % w == 0)
        chunks += [(src, dst, c, width) for c in range(0, cols, width)]
    slots = stage.shape[0]

    def copy(i):
        src, _, c, width = chunks[i]
        return pltpu.make_async_copy(src.at[:, c:c + width], stage.at[i % slots, :, 0:width], sems.at[i % slots])

    for i in range(min(slots, len(chunks))):
        copy(i).start()
    fillers = list(fillers)
    assert len(fillers) <= len(chunks)
    for i, (_, dst, c, width) in enumerate(chunks):
        if fillers:
            fillers.pop(0)()
        copy(i).wait()
        dst[:, c:c + width] = stage[i % slots, :, 0:width].astype(BF16)
        if i + slots < len(chunks):
            copy(i + slots).start()


def _mixer_kernel(x_ref, g_ref, win_hbm, convw_ref, wco_hbm, sinks_ref, bucket_ref, rel_ref, wao_hbm, wo_hbm,
                  wup_ref, wdown_ref, out_ref, wup_out_ref, wdown_out_ref,
                  win_buf, wco_buf, wao_buf, wo_buf, stage, stage_sems, bias_buf, cx_buf, kv_buf, attn_buf,
                  *, tile, tiles):
    s = pl.program_id(0)

    @pl.when(s == 0)
    def _():
        bias_tasks = [functools.partial(_bias_table, head, bucket_ref, rel_ref, bias_buf)
                      for head in range(N_HEADS)]
        _stage_weights([(win_hbm, win_buf), (wco_hbm, wco_buf), (wao_hbm, wao_buf), (wo_hbm, wo_buf)],
                       stage, stage_sems, bias_tasks)

    @pl.when(s > 0)
    def _():
        _mixer_step((s - 1) % tiles, x_ref, g_ref, win_buf, convw_ref, wco_buf, sinks_ref, bias_buf, wao_buf,
                    wo_buf, wup_ref, wdown_ref, out_ref, wup_out_ref, wdown_out_ref, cx_buf, kv_buf, attn_buf,
                    tile=tile)


def _mixer_step(t, x_ref, g_ref, win_ref, convw_ref, wco_ref, sinks_ref, bias_ref, wao_ref, wo_ref,
                wup_ref, wdown_ref, out_ref, wup_out_ref, wdown_out_ref, cx_buf, kv_buf, attn_buf, *, tile):
    @pl.when(t == 0)
    def _():
        cx_buf[:, 0:SUBLANES, :] = jnp.zeros((D_MODEL // LANES, SUBLANES, LANES), F32)
        kv_buf[0:BLOCK, :] = jnp.zeros((BLOCK, 4 * KV_COLS), BF16)

    x = x_ref[0]
    half = tile // 2
    h_halves = [_rmsnorm(x[r * half:(r + 1) * half], g_ref[...]).astype(BF16) for r in range(2)]
    h = jnp.concatenate(h_halves, axis=0)

    def proj(lo, width):
        return jnp.dot(h, win_ref[:, lo:lo + width], preferred_element_type=F32)

    def proj_halves(lo, width):
        w = win_ref[:, lo:lo + width]
        return jnp.concatenate([jnp.dot(hr, w, preferred_element_type=F32) for hr in h_halves], axis=0)

    q = (proj_halves(OFF_Q, D_MODEL) * (HEAD_DIM ** -0.5 * LOG2E)).astype(BF16)
    kv = proj_halves(OFF_K, 2 * KV_COLS)
    k = kv[:, 0:KV_COLS]
    val = kv[:, KV_COLS:2 * KV_COLS]
    k_swapped = pltpu.roll(k, HEAD_DIM, axis=1)
    low_half_tile = lax.broadcasted_iota(jnp.int32, (tile, LANES), 1) < HEAD_DIM
    new_rows = slice(BLOCK, BLOCK + tile)
    kv_buf[new_rows, 0:KV_COLS] = jnp.where(low_half_tile, k, k_swapped).astype(BF16)
    kv_buf[new_rows, KV_COLS:2 * KV_COLS] = jnp.where(low_half_tile, k_swapped, k).astype(BF16)
    kv_buf[new_rows, 2 * KV_COLS:3 * KV_COLS] = jnp.where(low_half_tile, val, 1.0).astype(BF16)
    kv_buf[new_rows, 3 * KV_COLS:4 * KV_COLS] = jnp.where(low_half_tile, 1.0, val).astype(BF16)

    low_half = lax.broadcasted_iota(jnp.int32, (BLOCK, LANES), 1) < HEAD_DIM
    zero = jnp.zeros((BLOCK, LANES), BF16)

    scores, probs, sink_terms = {}, {}, {}

    def qk(j):
        rows = slice(j * BLOCK, (j + 2) * BLOCK)
        qj = q[j * BLOCK:(j + 1) * BLOCK, :]
        for g in range(N_KV_HEADS):
            parts = []
            for p in range(g * PAIRS_PER_GROUP, (g + 1) * PAIRS_PER_GROUP):
                qp = qj[:, p * LANES:(p + 1) * LANES]
                parts.append(jnp.where(low_half, qp, zero))
                parts.append(jnp.where(low_half, zero, qp))
            q_heads = jnp.concatenate(parts, axis=0)
            keys = kv_buf[rows, g * KV_COLS:(g + 1) * KV_COLS]
            scores[j, g] = lax.dot_general(keys, q_heads, NT_DIMS, preferred_element_type=F32)

    def softmax(j, g):
        first = (t == 0).astype(jnp.int32) if j == 0 else 0
        s = scores.pop((j, g))
        for i in range(GROUP):
            head = g * GROUP + i
            sh = s[:, i * BLOCK:(i + 1) * BLOCK] + bias_ref[first, head]
            sink = jnp.full((1, BLOCK), sinks_ref[head], F32) * LOG2E
            m = jnp.maximum(jnp.max(sh, axis=0, keepdims=True), sink)
            probs[j, head] = jnp.exp2(sh - m).astype(BF16)
            sink_terms[j, head] = jnp.exp2(sink - m)

    def pv(j):
        rows = slice(j * BLOCK, (j + 2) * BLOCK)
        for g in range(N_KV_HEADS):
            vals = kv_buf[rows, (2 + g) * KV_COLS:(3 + g) * KV_COLS]
            p = jnp.concatenate([probs.pop((j, g * GROUP + i)) for i in range(GROUP)], axis=1)
            o = lax.dot_general(vals, p, TN_DIMS, preferred_element_type=F32)
            out_rows = slice(g * HEAD_DIM, (g + 1) * HEAD_DIM)
            sum_row = (1 - g) * HEAD_DIM
            for i in range(GROUP):
                head = g * GROUP + i
                cols = slice(i * BLOCK, (i + 1) * BLOCK)
                denom = o[sum_row:sum_row + 1, cols] + sink_terms.pop((j, head))
                oh = o[out_rows, cols] * (1.0 / denom)
                attn_buf[head * HEAD_DIM:(head + 1) * HEAD_DIM, j * BLOCK:(j + 1) * BLOCK] = oh.astype(BF16)

    v = {}

    def t_c():
        v["c"] = proj(OFF_C, D_MODEL)

    def t_x():
        cx = v.pop("c") * proj(OFF_X, D_MODEL)
        for c in range(D_MODEL // LANES):
            cx_buf[c, SUBLANES:SUBLANES + tile, :] = cx[:, c * LANES:(c + 1) * LANES]

    def t_conv():
        slabs = []
        for c in range(D_MODEL // LANES):
            cols = slice(c * LANES, (c + 1) * LANES)
            slabs.append(convw_ref[0:1, cols] * cx_buf[c, SUBLANES - 2:SUBLANES - 2 + tile, :]
                         + convw_ref[1:2, cols] * cx_buf[c, SUBLANES - 1:SUBLANES - 1 + tile, :]
                         + convw_ref[2:3, cols] * cx_buf[c, SUBLANES:SUBLANES + tile, :])
        v["conv"] = jnp.concatenate(slabs, axis=1)
        cx_buf[:, 0:SUBLANES, :] = cx_buf[:, tile:tile + SUBLANES, :]

    def t_b():
        v["u"] = (proj(OFF_B, D_MODEL) * v.pop("conv")).astype(BF16)

    def t_gc():
        v["gate_conv"] = jax.nn.sigmoid(proj(OFF_GC, D_MODEL))

    def t_ga():
        v["gate_attn"] = jax.nn.sigmoid(proj(OFF_GA, D_MODEL))

    def t_yconv(r):
        rows = slice(r * half, (r + 1) * half)
        v["mixed_conv", r] = v["gate_conv"][rows] * jnp.dot(v["u"][rows], wco_ref[...],
                                                            preferred_element_type=F32)

    def t_yattn(r):
        rows = slice(r * half, (r + 1) * half)
        y_attn = lax.dot_general(attn_buf[:, rows], wao_ref[...], TN_DIMS, preferred_element_type=F32)
        v["mixed", r] = (v.pop(("mixed_conv", r)) + v["gate_attn"][rows] * y_attn).astype(BF16)

    def t_out(r):
        rows = slice(r * half, (r + 1) * half)
        out_ref[0, rows, :] = x[rows] + jnp.dot(v.pop(("mixed", r)), wo_ref[...], preferred_element_type=F32)

    t_c()
    qk(0)
    wup_out_ref[...] = wup_ref[...].astype(BF16)
    wdown_out_ref[...] = wdown_ref[...].astype(BF16)
    softmax(0, 0)
    t_x()
    softmax(0, 1)
    t_gc()
    t_conv()
    pv(0)
    qk(1)
    t_b()
    softmax(1, 0)
    t_ga()
    softmax(1, 1)
    pv(1)
    qk(2)
    t_yconv(0)
    softmax(2, 0)
    t_yconv(1)
    softmax(2, 1)
    pv(2)
    qk(3)
    t_yattn(0)
    softmax(3, 0)
    t_out(0)
    softmax(3, 1)
    pv(3)
    kv_buf[0:BLOCK, :] = kv_buf[tile:tile + BLOCK, :]
    t_yattn(1)
    t_out(1)


def _mlp_kernel(x_ref, g_ref, wup_ref, wdown_ref, gf_ref, out_ref, *, subtiles):
    normed, act, mixed = {}, {}, {}
    starts = [sum(subtiles[:i]) for i in range(len(subtiles))]

    def rows(i):
        return slice(starts[i], starts[i] + subtiles[i])

    def norm(i):
        normed[i] = _rmsnorm(x_ref[rows(i), :], g_ref[...]).astype(BF16)

    def up(i):
        u = jnp.dot(normed.pop(i), wup_ref[...], preferred_element_type=F32)
        act[i] = jnp.square(jnp.maximum(u, 0.0)).astype(BF16)

    def down(i):
        mixed[i] = x_ref[rows(i), :] + jnp.dot(act.pop(i), wdown_ref[...], preferred_element_type=F32)

    def final(i):
        out_ref[rows(i), :] = _rmsnorm(mixed.pop(i), gf_ref[...])

    n = len(subtiles)
    norm(0)
    up(0)
    for i in range(n):
        if i + 1 < n:
            norm(i + 1)
            up(i + 1)
        if i > 0:
            final(i - 1)
        down(i)
    final(n - 1)


def _resident(shape):
    return pl.BlockSpec(shape, lambda *_: (0,) * len(shape), pipeline_mode=pl.Buffered(1))


def _mixer(x, g, w_in, conv_w, w_conv_out, sinks, rel_bias, w_attn_out, w_o, w_up, w_down):
    batch, seq, _ = x.shape
    tile = TILE_MIX
    assert seq % tile == 0 and tile % BLOCK == 0
    tiles = seq // tile
    steps = batch * tiles
    up_rows, down_rows = D_MODEL // steps, D_FF // steps
    assert up_rows * steps == D_MODEL and up_rows % BF16_SUBLANES == 0
    assert down_rows * steps == D_FF and down_rows % BF16_SUBLANES == 0

    def tile_of(s):
        return jnp.maximum(s - 1, 0)

    def x_map(s):
        return (tile_of(s) // tiles, tile_of(s) % tiles, 0)

    def slice_map(s):
        return (tile_of(s), 0)

    hbm = pl.BlockSpec(memory_space=pl.ANY)
    return pl.pallas_call(
        functools.partial(_mixer_kernel, tile=tile, tiles=tiles),
        grid=(steps + 1,),
        in_specs=[
            pl.BlockSpec((1, tile, D_MODEL), x_map),
            _resident((1, D_MODEL)),
            hbm,
            _resident((CONV_WIDTH, D_MODEL)),
            hbm,
            pl.BlockSpec(memory_space=pltpu.SMEM),
            _resident((2 * BLOCK, BLOCK)),
            pl.BlockSpec(memory_space=pltpu.SMEM),
            hbm,
            hbm,
            pl.BlockSpec((up_rows, D_FF), slice_map),
            pl.BlockSpec((down_rows, D_MODEL), slice_map),
        ],
        out_specs=[
            pl.BlockSpec((1, tile, D_MODEL), x_map),
            pl.BlockSpec((up_rows, D_FF), slice_map),
            pl.BlockSpec((down_rows, D_MODEL), slice_map),
        ],
        out_shape=[
            jax.ShapeDtypeStruct(x.shape, F32),
            jax.ShapeDtypeStruct(w_up.shape, BF16),
            jax.ShapeDtypeStruct(w_down.shape, BF16),
        ],
        scratch_shapes=[
            pltpu.VMEM((D_MODEL, IN_COLS), BF16),
            pltpu.VMEM((D_MODEL, D_MODEL), BF16),
            pltpu.VMEM((D_MODEL, D_MODEL), BF16),
            pltpu.VMEM((D_MODEL, D_MODEL), BF16),
            pltpu.VMEM((STAGE_SLOTS, D_MODEL, STAGE_COLS), F32),
            pltpu.SemaphoreType.DMA((STAGE_SLOTS,)),
            pltpu.VMEM((2, N_HEADS, 2 * BLOCK, BLOCK), F32),
            pltpu.VMEM((D_MODEL // LANES, tile + 2 * SUBLANES, LANES), F32),
            pltpu.VMEM((tile + BLOCK, 4 * KV_COLS), BF16),
            pltpu.VMEM((D_MODEL, tile), BF16),
        ],
        compiler_params=pltpu.CompilerParams(
            dimension_semantics=("arbitrary",), vmem_limit_bytes=VMEM_LIMIT),
        name="token_mixer",
    )(x, g, w_in, conv_w, w_conv_out, sinks, jnp.asarray(_bucket_table()), rel_bias, w_attn_out, w_o, w_up, w_down)


def _mlp(x, g, w_up, w_down, g_final):
    rows = x.shape[0]
    tile = sum(SUBTILES_MLP)
    assert rows % tile == 0 and all(s % BF16_SUBLANES == 0 for s in SUBTILES_MLP)
    return pl.pallas_call(
        functools.partial(_mlp_kernel, subtiles=SUBTILES_MLP),
        grid=(rows // tile,),
        in_specs=[
            pl.BlockSpec((tile, D_MODEL), lambda i: (i, 0)),
            _resident((1, D_MODEL)),
            _resident((D_MODEL, D_FF)),
            _resident((D_FF, D_MODEL)),
            _resident((1, D_MODEL)),
        ],
        out_specs=pl.BlockSpec((tile, D_MODEL), lambda i: (i, 0)),
        out_shape=jax.ShapeDtypeStruct(x.shape, F32),
        compiler_params=pltpu.CompilerParams(
            dimension_semantics=("arbitrary",), vmem_limit_bytes=VMEM_LIMIT),
        name="channel_mixer",
    )(x, g, w_up, w_down, g_final)


def kernel(x, attn_norm_g, w_in, conv_w, w_conv_out, attn_sinks, rel_bias, w_attn_out, w_o, mlp_norm_g, w_up,
           w_down, final_norm_g):
    batch, seq, _ = x.shape
    depth = w_in.shape[0]
    assert depth == 1, "the final norm is fused into the (only) layer's channel mixer"
    x, w_up_bf16, w_down_bf16 = _mixer(
        x, attn_norm_g.reshape(1, D_MODEL), w_in.reshape(D_MODEL, IN_COLS), conv_w.reshape(CONV_WIDTH, D_MODEL),
        w_conv_out.reshape(D_MODEL, D_MODEL), attn_sinks.reshape(N_HEADS).astype(F32), rel_bias.astype(F32),
        w_attn_out.reshape(D_MODEL, D_MODEL), w_o.reshape(D_MODEL, D_MODEL), w_up.reshape(D_MODEL, D_FF),
        w_down.reshape(D_FF, D_MODEL))
    x = _mlp(x.reshape(batch * seq, D_MODEL), mlp_norm_g.reshape(1, D_MODEL), w_up_bf16, w_down_bf16,
             final_norm_g.reshape(1, D_MODEL))
    return x.reshape(batch, seq, D_MODEL)
```

```python
import functools
import math

import jax
import jax.numpy as jnp
import numpy as np
from jax import lax
from jax.experimental import pallas as pl
from jax.experimental.pallas import tpu as pltpu

D_MODEL = 1024
HEAD_DIM = 64
N_HEADS = 16
N_KV_HEADS = 2
GROUP = N_HEADS // N_KV_HEADS
BLOCK = 128
N_BUCKETS = 32
MAX_DISTANCE = 128
CONV_WIDTH = 3
D_FF = 4 * D_MODEL
EPS = 1e-6
NEG_INF = -1e30
LOG2E = math.log2(math.e)

LANES = 128
SUBLANES = 8
BF16_SUBLANES = 16
KV_COLS = N_KV_HEADS * HEAD_DIM
PAIRS_PER_GROUP = GROUP // 2

OFF_B, OFF_C, OFF_X, OFF_Q = 0, D_MODEL, 2 * D_MODEL, 3 * D_MODEL
OFF_K = 4 * D_MODEL
OFF_V = OFF_K + KV_COLS
OFF_GC = OFF_V + KV_COLS
OFF_GA = OFF_GC + D_MODEL
IN_COLS = OFF_GA + D_MODEL

TILE_MIX = 512
SUBTILES_MLP = (512, 512)
VMEM_LIMIT = 56 * 1024 * 1024
STAGE_COLS = 640
STAGE_SLOTS = 3

BF16 = jnp.bfloat16
F32 = jnp.float32

NT_DIMS = (((1,), (1,)), ((), ()))
TN_DIMS = (((0,), (0,)), ((), ()))


def _bucket_table():
    kj = np.arange(2 * BLOCK)[:, None]
    qi = np.arange(BLOCK)[None, :]
    dist = qi + BLOCK - kj
    n = np.maximum(dist, 0)
    max_exact = N_BUCKETS // 2
    ratio = np.log(np.maximum(n, max_exact).astype(np.float32) / max_exact) / np.log(MAX_DISTANCE / max_exact)
    large = np.minimum(max_exact + (ratio * (N_BUCKETS - max_exact)).astype(np.int32), N_BUCKETS - 1)
    bucket = np.where(n < max_exact, n, large).astype(np.int32)
    in_window = (dist >= 0) & (dist < BLOCK)
    return np.where(in_window, bucket, -1).astype(np.int32)


def _bias_table(head, bucket_ref, rel_ref, bias_buf):
    bucket = bucket_ref[...]
    acc = jnp.full(bucket.shape, NEG_INF, F32)
    for b in range(N_BUCKETS):
        acc = jnp.where(bucket == b, rel_ref[b, head], acc)
    acc = acc * LOG2E
    bias_buf[0, head] = acc
    key = lax.broadcasted_iota(jnp.int32, bucket.shape, 0)
    bias_buf[1, head] = jnp.where(key >= BLOCK, acc, NEG_INF)


def _rmsnorm(x, g):
    y = x * lax.rsqrt(jnp.mean(x * x, axis=-1, keepdims=True) + EPS)
    return y * g


class _WeightStager:
    def __init__(self, uses, stage, sems, fillers):
        self.stage, self.sems, self.slots = stage, sems, stage.shape[0]
        self.fillers = dict(fillers)
        self.chunks, self.needed, self.done = [], {}, 0
        for name, src, dst, lo, hi in uses:
            cols = src.shape[1]
            width = max(w for w in range(LANES, STAGE_COLS + 1, LANES) if cols % w == 0)
            for c in range(lo // width * width, hi, width):
                if not any(c == c0 and src is s0 for s0, _, c0, _ in self.chunks):
                    self.chunks.append((src, dst, c, width))
            self.needed[name] = len(self.chunks)

    def _copy(self, i):
        src, _, c, width = self.chunks[i]
        slot = i % self.slots
        return pltpu.make_async_copy(src.at[:, c:c + width], self.stage.at[slot, :, 0:width], self.sems.at[slot])

    def start(self):
        for i in range(min(self.slots, len(self.chunks))):
            self._copy(i).start()

    def need(self, name):
        while self.done < self.needed[name]:
            i = self.done
            for filler in self.fillers.pop(i, ()):
                filler()
            self._copy(i).wait()
            _, dst, c, width = self.chunks[i]
            dst[:, c:c + width] = self.stage[i % self.slots, :, 0:width].astype(BF16)
            if i + self.slots < len(self.chunks):
                self._copy(i + self.slots).start()
            self.done += 1


def _mixer_kernel(x_ref, g_ref, win_hbm, convw_ref, wco_hbm, sinks_ref, bucket_ref, rel_ref, wao_hbm, wo_hbm,
                  wup_ref, wdown_ref, out_ref, wup_out_ref, wdown_out_ref,
                  win_buf, wco_buf, wao_buf, wo_buf, stage, stage_sems, bias_buf, cx_buf, kv_buf, attn_buf,
                  *, tile, tiles):
    s = pl.program_id(0)
    step = functools.partial(_mixer_step, s % tiles, x_ref, g_ref, win_buf, convw_ref, wco_buf, sinks_ref,
                             bias_buf, wao_buf, wo_buf, wup_ref, wdown_ref, out_ref, wup_out_ref, wdown_out_ref,
                             cx_buf, kv_buf, attn_buf, tile=tile)

    @pl.when(s == 0)
    def _():
        uses = [("qkv", win_hbm, win_buf, OFF_Q, OFF_GC), ("c", win_hbm, win_buf, OFF_C, OFF_X),
                ("x", win_hbm, win_buf, OFF_X, OFF_Q), ("gc", win_hbm, win_buf, OFF_GC, OFF_GA),
                ("b", win_hbm, win_buf, OFF_B, OFF_C), ("ga", win_hbm, win_buf, OFF_GA, IN_COLS),
                ("yconv", wco_hbm, wco_buf, 0, D_MODEL), ("yattn", wao_hbm, wao_buf, 0, D_MODEL),
                ("out", wo_hbm, wo_buf, 0, D_MODEL)]
        per_chunk = 3
        fillers = {i: [functools.partial(_bias_table, head, bucket_ref, rel_ref, bias_buf)
                       for head in range(per_chunk * i, min(per_chunk * (i + 1), N_HEADS))]
                   for i in range(-(-N_HEADS // per_chunk))}
        stager = _WeightStager(uses, stage, stage_sems, fillers)
        assert stager.needed["c"] >= len(fillers)
        stager.start()
        step(need=stager.need)

    @pl.when(s > 0)
    def _():
        step(need=lambda name: None)


def _mixer_step(t, x_ref, g_ref, win_ref, convw_ref, wco_ref, sinks_ref, bias_ref, wao_ref, wo_ref,
                wup_ref, wdown_ref, out_ref, wup_out_ref, wdown_out_ref, cx_buf, kv_buf, attn_buf, *, tile, need):
    @pl.when(t == 0)
    def _():
        cx_buf[:, 0:SUBLANES, :] = jnp.zeros((D_MODEL // LANES, SUBLANES, LANES), F32)
        kv_buf[0:BLOCK, :] = jnp.zeros((BLOCK, 4 * KV_COLS), BF16)

    x = x_ref[0]
    half = tile // 2
    h_halves = [_rmsnorm(x[r * half:(r + 1) * half], g_ref[...]).astype(BF16) for r in range(2)]
    h = jnp.concatenate(h_halves, axis=0)

    def proj(lo, width):
        return jnp.dot(h, win_ref[:, lo:lo + width], preferred_element_type=F32)

    def proj_halves(lo, width):
        w = win_ref[:, lo:lo + width]
        return jnp.concatenate([jnp.dot(hr, w, preferred_element_type=F32) for hr in h_halves], axis=0)

    need("qkv")
    q = (proj_halves(OFF_Q, D_MODEL) * (HEAD_DIM ** -0.5 * LOG2E)).astype(BF16)
    kv = proj_halves(OFF_K, 2 * KV_COLS)
    k = kv[:, 0:KV_COLS]
    val = kv[:, KV_COLS:2 * KV_COLS]
    k_swapped = pltpu.roll(k, HEAD_DIM, axis=1)
    low_half_tile = lax.broadcasted_iota(jnp.int32, (tile, LANES), 1) < HEAD_DIM
    new_rows = slice(BLOCK, BLOCK + tile)
    kv_buf[new_rows, 0:KV_COLS] = jnp.where(low_half_tile, k, k_swapped).astype(BF16)
    kv_buf[new_rows, KV_COLS:2 * KV_COLS] = jnp.where(low_half_tile, k_swapped, k).astype(BF16)
    kv_buf[new_rows, 2 * KV_COLS:3 * KV_COLS] = jnp.where(low_half_tile, val, 1.0).astype(BF16)
    kv_buf[new_rows, 3 * KV_COLS:4 * KV_COLS] = jnp.where(low_half_tile, 1.0, val).astype(BF16)

    low_half = lax.broadcasted_iota(jnp.int32, (BLOCK, LANES), 1) < HEAD_DIM
    zero = jnp.zeros((BLOCK, LANES), BF16)

    scores, probs, sink_terms = {}, {}, {}

    def qk(j):
        rows = slice(j * BLOCK, (j + 2) * BLOCK)
        qj = q[j * BLOCK:(j + 1) * BLOCK, :]
        for g in range(N_KV_HEADS):
            parts = []
            for p in range(g * PAIRS_PER_GROUP, (g + 1) * PAIRS_PER_GROUP):
                qp = qj[:, p * LANES:(p + 1) * LANES]
                parts.append(jnp.where(low_half, qp, zero))
                parts.append(jnp.where(low_half, zero, qp))
            q_heads = jnp.concatenate(parts, axis=0)
            keys = kv_buf[rows, g * KV_COLS:(g + 1) * KV_COLS]
            scores[j, g] = lax.dot_general(keys, q_heads, NT_DIMS, preferred_element_type=F32)

    def softmax(j, g):
        first = (t == 0).astype(jnp.int32) if j == 0 else 0
        s = scores.pop((j, g))
        for i in range(GROUP):
            head = g * GROUP + i
            sh = s[:, i * BLOCK:(i + 1) * BLOCK] + bias_ref[first, head]
            sink = jnp.full((1, BLOCK), sinks_ref[head], F32) * LOG2E
            m = jnp.maximum(jnp.max(sh, axis=0, keepdims=True), sink)
            probs[j, head] = jnp.exp2(sh - m).astype(BF16)
            sink_terms[j, head] = jnp.exp2(sink - m)

    def pv(j):
        rows = slice(j * BLOCK, (j + 2) * BLOCK)
        for g in range(N_KV_HEADS):
            vals = kv_buf[rows, (2 + g) * KV_COLS:(3 + g) * KV_COLS]
            p = jnp.concatenate([probs.pop((j, g * GROUP + i)) for i in range(GROUP)], axis=1)
            o = lax.dot_general(vals, p, TN_DIMS, preferred_element_type=F32)
            out_rows = slice(g * HEAD_DIM, (g + 1) * HEAD_DIM)
            sum_row = (1 - g) * HEAD_DIM
            for i in range(GROUP):
                head = g * GROUP + i
                cols = slice(i * BLOCK, (i + 1) * BLOCK)
                denom = o[sum_row:sum_row + 1, cols] + sink_terms.pop((j, head))
                oh = o[out_rows, cols] * (1.0 / denom)
                attn_buf[head * HEAD_DIM:(head + 1) * HEAD_DIM, j * BLOCK:(j + 1) * BLOCK] = oh.astype(BF16)

    v = {}

    def t_c():
        need("c")
        v["c"] = proj(OFF_C, D_MODEL)

    def t_x():
        need("x")
        cx = v.pop("c") * proj(OFF_X, D_MODEL)
        for c in range(D_MODEL // LANES):
            cx_buf[c, SUBLANES:SUBLANES + tile, :] = cx[:, c * LANES:(c + 1) * LANES]

    def t_conv():
        slabs = []
        for c in range(D_MODEL // LANES):
            cols = slice(c * LANES, (c + 1) * LANES)
            slabs.append(convw_ref[0:1, cols] * cx_buf[c, SUBLANES - 2:SUBLANES - 2 + tile, :]
                         + convw_ref[1:2, cols] * cx_buf[c, SUBLANES - 1:SUBLANES - 1 + tile, :]
                         + convw_ref[2:3, cols] * cx_buf[c, SUBLANES:SUBLANES + tile, :])
        v["conv"] = jnp.concatenate(slabs, axis=1)
        cx_buf[:, 0:SUBLANES, :] = cx_buf[:, tile:tile + SUBLANES, :]

    def t_b():
        need("b")
        v["u"] = (proj(OFF_B, D_MODEL) * v.pop("conv")).astype(BF16)

    def t_gc():
        need("gc")
        v["gate_conv"] = jax.nn.sigmoid(proj(OFF_GC, D_MODEL))

    def t_ga():
        need("ga")
        v["gate_attn"] = jax.nn.sigmoid(proj(OFF_GA, D_MODEL))

    def t_yconv():
        need("yconv")
        v["mixed"] = v.pop("gate_conv") * jnp.dot(v.pop("u"), wco_ref[...], preferred_element_type=F32)

    def t_yattn(r):
        need("yattn")
        rows = slice(r * half, (r + 1) * half)
        y_attn = lax.dot_general(attn_buf[:, rows], wao_ref[...], TN_DIMS, preferred_element_type=F32)
        v["mixed", r] = (v["mixed"][rows] + v["gate_attn"][rows] * y_attn).astype(BF16)

    def t_out(r):
        need("out")
        rows = slice(r * half, (r + 1) * half)
        out_ref[0, rows, :] = x[rows] + jnp.dot(v.pop(("mixed", r)), wo_ref[...], preferred_element_type=F32)

    qk(0)
    t_c()
    wup_out_ref[...] = wup_ref[...].astype(BF16)
    wdown_out_ref[...] = wdown_ref[...].astype(BF16)
    softmax(0, 0)
    t_x()
    softmax(0, 1)
    t_gc()
    t_conv()
    pv(0)
    qk(1)
    t_b()
    softmax(1, 0)
    t_ga()
    softmax(1, 1)
    pv(1)
    qk(2)
    t_yconv()
    softmax(2, 0)
    t_yattn(0)
    softmax(2, 1)
    pv(2)
    qk(3)
    t_out(0)
    softmax(3, 0)
    softmax(3, 1)
    pv(3)
    kv_buf[0:BLOCK, :] = kv_buf[tile:tile + BLOCK, :]
    t_yattn(1)
    t_out(1)


def _mlp_kernel(x_ref, g_ref, wup_ref, wdown_ref, gf_ref, out_ref, *, subtiles):
    normed, act, mixed = {}, {}, {}
    starts = [sum(subtiles[:i]) for i in range(len(subtiles))]

    def rows(i):
        return slice(starts[i], starts[i] + subtiles[i])

    def norm(i):
        normed[i] = _rmsnorm(x_ref[rows(i), :], g_ref[...]).astype(BF16)

    def up(i):
        u = jnp.dot(normed.pop(i), wup_ref[...], preferred_element_type=F32)
        act[i] = jnp.square(jnp.maximum(u, 0.0)).astype(BF16)

    def down(i):
        mixed[i] = x_ref[rows(i), :] + jnp.dot(act.pop(i), wdown_ref[...], preferred_element_type=F32)

    def final(i):
        out_ref[rows(i), :] = _rmsnorm(mixed.pop(i), gf_ref[...])

    n = len(subtiles)
    norm(0)
    up(0)
    for i in range(n):
        if i + 1 < n:
            norm(i + 1)
            up(i + 1)
        if i > 0:
            final(i - 1)
        down(i)
    final(n - 1)


def _resident(shape):
    return pl.BlockSpec(shape, lambda *_: (0,) * len(shape), pipeline_mode=pl.Buffered(1))


def _mixer(x, g, w_in, conv_w, w_conv_out, sinks, rel_bias, w_attn_out, w_o, w_up, w_down):
    batch, seq, _ = x.shape
    tile = TILE_MIX
    assert seq % tile == 0 and tile % BLOCK == 0
    tiles = seq // tile
    steps = batch * tiles
    up_rows, down_rows = D_MODEL // steps, D_FF // steps
    assert up_rows * steps == D_MODEL and up_rows % BF16_SUBLANES == 0
    assert down_rows * steps == D_FF and down_rows % BF16_SUBLANES == 0

    def x_map(s):
        return (s // tiles, s % tiles, 0)

    def slice_map(s):
        return (s, 0)

    hbm = pl.BlockSpec(memory_space=pl.ANY)
    return pl.pallas_call(
        functools.partial(_mixer_kernel, tile=tile, tiles=tiles),
        grid=(steps,),
        in_specs=[
            pl.BlockSpec((1, tile, D_MODEL), x_map),
            _resident((1, D_MODEL)),
            hbm,
            _resident((CONV_WIDTH, D_MODEL)),
            hbm,
            pl.BlockSpec(memory_space=pltpu.SMEM),
            _resident((2 * BLOCK, BLOCK)),
            pl.BlockSpec(memory_space=pltpu.SMEM),
            hbm,
            hbm,
            pl.BlockSpec((up_rows, D_FF), slice_map),
            pl.BlockSpec((down_rows, D_MODEL), slice_map),
        ],
        out_specs=[
            pl.BlockSpec((1, tile, D_MODEL), x_map),
            pl.BlockSpec((up_rows, D_FF), slice_map),
            pl.BlockSpec((down_rows, D_MODEL), slice_map),
        ],
        out_shape=[
            jax.ShapeDtypeStruct(x.shape, F32),
            jax.ShapeDtypeStruct(w_up.shape, BF16),
            jax.ShapeDtypeStruct(w_down.shape, BF16),
        ],
        scratch_shapes=[
            pltpu.VMEM((D_MODEL, IN_COLS), BF16),
            pltpu.VMEM((D_MODEL, D_MODEL), BF16),
            pltpu.VMEM((D_MODEL, D_MODEL), BF16),
            pltpu.VMEM((D_MODEL, D_MODEL), BF16),
            pltpu.VMEM((STAGE_SLOTS, D_MODEL, STAGE_COLS), F32),
            pltpu.SemaphoreType.DMA((STAGE_SLOTS,)),
            pltpu.VMEM((2, N_HEADS, 2 * BLOCK, BLOCK), F32),
            pltpu.VMEM((D_MODEL // LANES, tile + 2 * SUBLANES, LANES), F32),
            pltpu.VMEM((tile + BLOCK, 4 * KV_COLS), BF16),
            pltpu.VMEM((D_MODEL, tile), BF16),
        ],
        compiler_params=pltpu.CompilerParams(
            dimension_semantics=("arbitrary",), vmem_limit_bytes=VMEM_LIMIT),
        name="token_mixer",
    )(x, g, w_in, conv_w, w_conv_out, sinks, jnp.asarray(_bucket_table()), rel_bias, w_attn_out, w_o, w_up, w_down)


def _mlp(x, g, w_up, w_down, g_final):
    rows = x.shape[0]
    tile = sum(SUBTILES_MLP)
    assert rows % tile == 0 and all(s % BF16_SUBLANES == 0 for s in SUBTILES_MLP)
    return pl.pallas_call(
        functools.partial(_mlp_kernel, subtiles=SUBTILES_MLP),
        grid=(rows // tile,),
        in_specs=[
            pl.BlockSpec((tile, D_MODEL), lambda i: (i, 0)),
            _resident((1, D_MODEL)),
            _resident((D_MODEL, D_FF)),
            _resident((D_FF, D_MODEL)),
            _resident((1, D_MODEL)),
        ],
        out_specs=pl.BlockSpec((tile, D_MODEL), lambda i: (i, 0)),
        out_shape=jax.ShapeDtypeStruct(x.shape, F32),
        compiler_params=pltpu.CompilerParams(
            dimension_semantics=("arbitrary",), vmem_limit_bytes=VMEM_LIMIT),
        name="channel_mixer",
    )(x, g, w_up, w_down, g_final)


def kernel(x, attn_norm_g, w_in, conv_w, w_conv_out, attn_sinks, rel_bias, w_attn_out, w_o, mlp_norm_g, w_up,
           w_down, final_norm_g):
    batch, seq, _ = x.shape
    depth = w_in.shape[0]
    assert depth == 1, "the final norm is fused into the (only) layer's channel mixer"
    x, w_up_bf16, w_down_bf16 = _mixer(
        x, attn_norm_g.reshape(1, D_MODEL), w_in.reshape(D_MODEL, IN_COLS), conv_w.reshape(CONV_WIDTH, D_MODEL),
        w_conv_out.reshape(D_MODEL, D_MODEL), attn_sinks.reshape(N_HEADS).astype(F32), rel_bias.astype(F32),
        w_attn_out.reshape(D_MODEL, D_MODEL), w_o.reshape(D_MODEL, D_MODEL), w_up.reshape(D_MODEL, D_FF),
        w_down.reshape(D_FF, D_MODEL))
    x = _mlp(x.reshape(batch * seq, D_MODEL), mlp_norm_g.reshape(1, D_MODEL), w_up_bf16, w_down_bf16,
             final_norm_g.reshape(1, D_MODEL))
    return x.reshape(batch, seq, D_MODEL)
```

```python
import functools
import math

import jax
import jax.numpy as jnp
import numpy as np
from jax import lax
from jax.experimental import pallas as pl
from jax.experimental.pallas import tpu as pltpu

D_MODEL = 1024
HEAD_DIM = 64
N_HEADS = 16
N_KV_HEADS = 2
GROUP = N_HEADS // N_KV_HEADS
BLOCK = 128
N_BUCKETS = 32
MAX_DISTANCE = 128
CONV_WIDTH = 3
D_FF = 4 * D_MODEL
EPS = 1e-6
NEG_INF = -1e30
LOG2E = math.log2(math.e)

LANES = 128
SUBLANES = 8
BF16_SUBLANES = 16
KV_COLS = N_KV_HEADS * HEAD_DIM
PAIRS_PER_GROUP = GROUP // 2

OFF_B, OFF_C, OFF_X, OFF_Q = 0, D_MODEL, 2 * D_MODEL, 3 * D_MODEL
OFF_K = 4 * D_MODEL
OFF_V = OFF_K + KV_COLS
OFF_GC = OFF_V + KV_COLS
OFF_GA = OFF_GC + D_MODEL
IN_COLS = OFF_GA + D_MODEL

TILE_MIX = 512
SUBTILES_MLP = (512, 512)
VMEM_LIMIT = 56 * 1024 * 1024
STAGE_COLS = 640
STAGE_SLOTS = 2

BF16 = jnp.bfloat16
F32 = jnp.float32

NT_DIMS = (((1,), (1,)), ((), ()))
TN_DIMS = (((0,), (0,)), ((), ()))


def _bucket_table():
    kj = np.arange(2 * BLOCK)[:, None]
    qi = np.arange(BLOCK)[None, :]
    dist = qi + BLOCK - kj
    n = np.maximum(dist, 0)
    max_exact = N_BUCKETS // 2
    ratio = np.log(np.maximum(n, max_exact).astype(np.float32) / max_exact) / np.log(MAX_DISTANCE / max_exact)
    large = np.minimum(max_exact + (ratio * (N_BUCKETS - max_exact)).astype(np.int32), N_BUCKETS - 1)
    bucket = np.where(n < max_exact, n, large).astype(np.int32)
    in_window = (dist >= 0) & (dist < BLOCK)
    return np.where(in_window, bucket, -1).astype(np.int32)


def _bias_table(head, bucket_ref, rel_ref, bias_buf):
    bucket = bucket_ref[...]
    acc = jnp.full(bucket.shape, NEG_INF, F32)
    for b in range(N_BUCKETS):
        acc = jnp.where(bucket == b, rel_ref[b, head], acc)
    bias_buf[head] = acc * LOG2E


def _rmsnorm(x, g):
    y = x * lax.rsqrt(jnp.mean(x * x, axis=-1, keepdims=True) + EPS)
    return y * g


def _stage_weights(pairs, stage, sems, fillers):
    chunks = []
    for src, dst in pairs:
        cols = src.shape[1]
        width = max(w for w in range(LANES, STAGE_COLS + 1, LANES) if cols % w == 0)
        chunks += [(src, dst, c, width) for c in range(0, cols, width)]
    slots = stage.shape[0]

    def copy(i):
        src, _, c, width = chunks[i]
        return pltpu.make_async_copy(src.at[:, c:c + width], stage.at[i % slots, :, 0:width], sems.at[i % slots])

    for i in range(min(slots, len(chunks))):
        copy(i).start()
    fillers = list(fillers)
    assert len(fillers) <= len(chunks)
    for i, (_, dst, c, width) in enumerate(chunks):
        if fillers:
            fillers.pop(0)()
        copy(i).wait()
        dst[:, c:c + width] = stage[i % slots, :, 0:width].astype(BF16)
        if i + slots < len(chunks):
            copy(i + slots).start()


def _prepare_tile(x_ref, g_ref, win_ref, h_buf, q_buf, kv_buf, *, tile):
    half = tile // 2
    w_q = win_ref[:, OFF_Q:OFF_Q + D_MODEL]
    w_kv = win_ref[:, OFF_K:OFF_K + 2 * KV_COLS]
    low_half = lax.broadcasted_iota(jnp.int32, (half, LANES), 1) < HEAD_DIM
    for r in range(2):
        rows = slice(r * half, (r + 1) * half)
        h = _rmsnorm(x_ref[0, rows, :], g_ref[...]).astype(BF16)
        h_buf[rows, :] = h
        q_buf[rows, :] = (jnp.dot(h, w_q, preferred_element_type=F32) * (HEAD_DIM ** -0.5 * LOG2E)).astype(BF16)
        kv = jnp.dot(h, w_kv, preferred_element_type=F32)
        k = kv[:, 0:KV_COLS]
        val = kv[:, KV_COLS:2 * KV_COLS]
        k_swapped = pltpu.roll(k, HEAD_DIM, axis=1)
        new_rows = slice(BLOCK + r * half, BLOCK + (r + 1) * half)
        kv_buf[new_rows, 0:KV_COLS] = jnp.where(low_half, k, k_swapped).astype(BF16)
        kv_buf[new_rows, KV_COLS:2 * KV_COLS] = jnp.where(low_half, k_swapped, k).astype(BF16)
        kv_buf[new_rows, 2 * KV_COLS:3 * KV_COLS] = jnp.where(low_half, val, 1.0).astype(BF16)
        kv_buf[new_rows, 3 * KV_COLS:4 * KV_COLS] = jnp.where(low_half, 1.0, val).astype(BF16)


def _mixer_kernel(x_ref, xnext_ref, g_ref, win_hbm, convw_ref, wco_hbm, sinks_ref, bucket_ref, rel_ref, wao_hbm,
                  wo_hbm, wup_ref, wdown_ref, out_ref, wup_out_ref, wdown_out_ref,
                  win_buf, wco_buf, wao_buf, wo_buf, stage, stage_sems, bias_buf, h_buf, q_buf, cx_buf, kv_buf,
                  attn_buf, *, tile, tiles):
    s = pl.program_id(0)
    prepare_next = functools.partial(_prepare_tile, xnext_ref, g_ref, win_buf, h_buf, q_buf, kv_buf, tile=tile)

    @pl.when(s == 0)
    def _():
        bias_tasks = [functools.partial(_bias_table, head, bucket_ref, rel_ref, bias_buf)
                      for head in range(N_HEADS)]
        _stage_weights([(win_hbm, win_buf), (wco_hbm, wco_buf), (wao_hbm, wao_buf), (wo_hbm, wo_buf)],
                       stage, stage_sems, bias_tasks)
        prepare_next()

    @pl.when(s > 0)
    def _():
        _mixer_step((s - 1) % tiles, x_ref, h_buf, q_buf, prepare_next, win_buf, convw_ref, wco_buf, sinks_ref,
                    bias_buf, wao_buf, wo_buf, wup_ref, wdown_ref, out_ref, wup_out_ref, wdown_out_ref, cx_buf,
                    kv_buf, attn_buf, tile=tile)


def _mixer_step(t, x_ref, h_ref, q_ref, prepare_next, win_ref, convw_ref, wco_ref, sinks_ref, bias_ref, wao_ref,
                wo_ref, wup_ref, wdown_ref, out_ref, wup_out_ref, wdown_out_ref, cx_buf, kv_buf, attn_buf, *, tile):
    @pl.when(t == 0)
    def _():
        cx_buf[:, 0:SUBLANES, :] = jnp.zeros((D_MODEL // LANES, SUBLANES, LANES), F32)
        kv_buf[0:BLOCK, :] = jnp.zeros((BLOCK, 4 * KV_COLS), BF16)

    x = x_ref[0]
    half = tile // 2

    def proj(lo, width):
        return jnp.dot(h_ref[...], win_ref[:, lo:lo + width], preferred_element_type=F32)

    low_half = lax.broadcasted_iota(jnp.int32, (BLOCK, LANES), 1) < HEAD_DIM
    zero = jnp.zeros((BLOCK, LANES), BF16)

    scores, probs, sink_terms = {}, {}, {}

    def qk(j):
        rows = slice(j * BLOCK, (j + 2) * BLOCK)
        qj = q_ref[j * BLOCK:(j + 1) * BLOCK, :]
        for g in range(N_KV_HEADS):
            parts = []
            for p in range(g * PAIRS_PER_GROUP, (g + 1) * PAIRS_PER_GROUP):
                qp = qj[:, p * LANES:(p + 1) * LANES]
                parts.append(jnp.where(low_half, qp, zero))
                parts.append(jnp.where(low_half, zero, qp))
            q_heads = jnp.concatenate(parts, axis=0)
            keys = kv_buf[rows, g * KV_COLS:(g + 1) * KV_COLS]
            scores[j, g] = lax.dot_general(keys, q_heads, NT_DIMS, preferred_element_type=F32)

    def softmax(j, g):
        s = scores.pop((j, g))
        for i in range(GROUP):
            head = g * GROUP + i
            bias = bias_ref[head]
            if j == 0:
                bias = jnp.concatenate([jnp.where(t == 0, NEG_INF, bias[0:BLOCK]), bias[BLOCK:]], axis=0)
            sh = s[:, i * BLOCK:(i + 1) * BLOCK] + bias
            sink = jnp.full((1, BLOCK), sinks_ref[head], F32) * LOG2E
            m = jnp.maximum(jnp.max(sh, axis=0, keepdims=True), sink)
            probs[j, head] = jnp.exp2(sh - m).astype(BF16)
            sink_terms[j, head] = jnp.exp2(sink - m)

    def pv(j):
        rows = slice(j * BLOCK, (j + 2) * BLOCK)
        for g in range(N_KV_HEADS):
            vals = kv_buf[rows, (2 + g) * KV_COLS:(3 + g) * KV_COLS]
            p = jnp.concatenate([probs.pop((j, g * GROUP + i)) for i in range(GROUP)], axis=1)
            o = lax.dot_general(vals, p, TN_DIMS, preferred_element_type=F32)
            out_rows = slice(g * HEAD_DIM, (g + 1) * HEAD_DIM)
            sum_row = (1 - g) * HEAD_DIM
            for i in range(GROUP):
                head = g * GROUP + i
                cols = slice(i * BLOCK, (i + 1) * BLOCK)
                denom = o[sum_row:sum_row + 1, cols] + sink_terms.pop((j, head))
                oh = o[out_rows, cols] * (1.0 / denom)
                attn_buf[head * HEAD_DIM:(head + 1) * HEAD_DIM, j * BLOCK:(j + 1) * BLOCK] = oh.astype(BF16)

    v = {}

    def t_c():
        v["c"] = proj(OFF_C, D_MODEL)

    def t_x():
        cx = v.pop("c") * proj(OFF_X, D_MODEL)
        for c in range(D_MODEL // LANES):
            cx_buf[c, SUBLANES:SUBLANES + tile, :] = cx[:, c * LANES:(c + 1) * LANES]

    def t_conv():
        slabs = []
        for c in range(D_MODEL // LANES):
            cols = slice(c * LANES, (c + 1) * LANES)
            slabs.append(convw_ref[0:1, cols] * cx_buf[c, SUBLANES - 2:SUBLANES - 2 + tile, :]
                         + convw_ref[1:2, cols] * cx_buf[c, SUBLANES - 1:SUBLANES - 1 + tile, :]
                         + convw_ref[2:3, cols] * cx_buf[c, SUBLANES:SUBLANES + tile, :])
        v["conv"] = jnp.concatenate(slabs, axis=1)
        cx_buf[:, 0:SUBLANES, :] = cx_buf[:, tile:tile + SUBLANES, :]

    def t_b():
        v["u"] = (proj(OFF_B, D_MODEL) * v.pop("conv")).astype(BF16)

    def t_gc():
        v["gate_conv"] = jax.nn.sigmoid(proj(OFF_GC, D_MODEL))

    def t_ga():
        v["gate_attn"] = jax.nn.sigmoid(proj(OFF_GA, D_MODEL))

    def t_yconv():
        v["mixed"] = v.pop("gate_conv") * jnp.dot(v.pop("u"), wco_ref[...], preferred_element_type=F32)

    def t_yattn(r):
        rows = slice(r * half, (r + 1) * half)
        y_attn = lax.dot_general(attn_buf[:, rows], wao_ref[...], TN_DIMS, preferred_element_type=F32)
        v["mixed", r] = (v["mixed"][rows] + v["gate_attn"][rows] * y_attn).astype(BF16)

    def t_out(r):
        rows = slice(r * half, (r + 1) * half)
        out_ref[0, rows, :] = x[rows] + jnp.dot(v.pop(("mixed", r)), wo_ref[...], preferred_element_type=F32)

    qk(0)
    t_c()
    wup_out_ref[...] = wup_ref[...].astype(BF16)
    wdown_out_ref[...] = wdown_ref[...].astype(BF16)
    softmax(0, 0)
    t_x()
    softmax(0, 1)
    t_gc()
    t_conv()
    pv(0)
    qk(1)
    t_b()
    softmax(1, 0)
    t_ga()
    softmax(1, 1)
    pv(1)
    qk(2)
    t_yconv()
    softmax(2, 0)
    t_yattn(0)
    softmax(2, 1)
    pv(2)
    qk(3)
    t_out(0)
    softmax(3, 0)
    softmax(3, 1)
    pv(3)
    kv_buf[0:BLOCK, :] = kv_buf[tile:tile + BLOCK, :]
    prepare_next()
    t_yattn(1)
    t_out(1)


def _mlp_kernel(x_ref, g_ref, wup_ref, wdown_ref, gf_ref, out_ref, *, subtiles):
    normed, act, mixed = {}, {}, {}
    starts = [sum(subtiles[:i]) for i in range(len(subtiles))]

    def rows(i):
        return slice(starts[i], starts[i] + subtiles[i])

    def norm(i):
        normed[i] = _rmsnorm(x_ref[rows(i), :], g_ref[...]).astype(BF16)

    def up(i):
        u = jnp.dot(normed.pop(i), wup_ref[...], preferred_element_type=F32)
        act[i] = jnp.square(jnp.maximum(u, 0.0)).astype(BF16)

    def down(i):
        mixed[i] = x_ref[rows(i), :] + jnp.dot(act.pop(i), wdown_ref[...], preferred_element_type=F32)

    def final(i):
        out_ref[rows(i), :] = _rmsnorm(mixed.pop(i), gf_ref[...])

    n = len(subtiles)
    norm(0)
    up(0)
    for i in range(n):
        if i + 1 < n:
            norm(i + 1)
            up(i + 1)
        if i > 0:
            final(i - 1)
        down(i)
    final(n - 1)


def _resident(shape):
    return pl.BlockSpec(shape, lambda *_: (0,) * len(shape), pipeline_mode=pl.Buffered(1))


def _mixer(x, g, w_in, conv_w, w_conv_out, sinks, rel_bias, w_attn_out, w_o, w_up, w_down):
    batch, seq, _ = x.shape
    tile = TILE_MIX
    assert seq % tile == 0 and tile % BLOCK == 0
    tiles = seq // tile
    steps = batch * tiles
    up_rows, down_rows = D_MODEL // steps, D_FF // steps
    assert up_rows * steps == D_MODEL and up_rows % BF16_SUBLANES == 0
    assert down_rows * steps == D_FF and down_rows % BF16_SUBLANES == 0

    def tile_of(s):
        return jnp.maximum(s - 1, 0)

    def x_map(s):
        return (tile_of(s) // tiles, tile_of(s) % tiles, 0)

    def next_map(s):
        nxt = jnp.minimum(s, steps - 1)
        return (nxt // tiles, nxt % tiles, 0)

    def slice_map(s):
        return (tile_of(s), 0)

    hbm = pl.BlockSpec(memory_space=pl.ANY)
    return pl.pallas_call(
        functools.partial(_mixer_kernel, tile=tile, tiles=tiles),
        grid=(steps + 1,),
        in_specs=[
            pl.BlockSpec((1, tile, D_MODEL), x_map),
            pl.BlockSpec((1, tile, D_MODEL), next_map),
            _resident((1, D_MODEL)),
            hbm,
            _resident((CONV_WIDTH, D_MODEL)),
            hbm,
            pl.BlockSpec(memory_space=pltpu.SMEM),
            _resident((2 * BLOCK, BLOCK)),
            pl.BlockSpec(memory_space=pltpu.SMEM),
            hbm,
            hbm,
            pl.BlockSpec((up_rows, D_FF), slice_map),
            pl.BlockSpec((down_rows, D_MODEL), slice_map),
        ],
        out_specs=[
            pl.BlockSpec((1, tile, D_MODEL), x_map),
            pl.BlockSpec((up_rows, D_FF), slice_map),
            pl.BlockSpec((down_rows, D_MODEL), slice_map),
        ],
        out_shape=[
            jax.ShapeDtypeStruct(x.shape, F32),
            jax.ShapeDtypeStruct(w_up.shape, BF16),
            jax.ShapeDtypeStruct(w_down.shape, BF16),
        ],
        scratch_shapes=[
            pltpu.VMEM((D_MODEL, IN_COLS), BF16),
            pltpu.VMEM((D_MODEL, D_MODEL), BF16),
            pltpu.VMEM((D_MODEL, D_MODEL), BF16),
            pltpu.VMEM((D_MODEL, D_MODEL), BF16),
            pltpu.VMEM((STAGE_SLOTS, D_MODEL, STAGE_COLS), F32),
            pltpu.SemaphoreType.DMA((STAGE_SLOTS,)),
            pltpu.VMEM((N_HEADS, 2 * BLOCK, BLOCK), F32),
            pltpu.VMEM((tile, D_MODEL), BF16),
            pltpu.VMEM((tile, D_MODEL), BF16),
            pltpu.VMEM((D_MODEL // LANES, tile + 2 * SUBLANES, LANES), F32),
            pltpu.VMEM((tile + BLOCK, 4 * KV_COLS), BF16),
            pltpu.VMEM((D_MODEL, tile), BF16),
        ],
        compiler_params=pltpu.CompilerParams(
            dimension_semantics=("arbitrary",), vmem_limit_bytes=VMEM_LIMIT),
        name="token_mixer",
    )(x, x, g, w_in, conv_w, w_conv_out, sinks, jnp.asarray(_bucket_table()), rel_bias, w_attn_out, w_o, w_up,
      w_down)


def _mlp(x, g, w_up, w_down, g_final):
    rows = x.shape[0]
    tile = sum(SUBTILES_MLP)
    assert rows % tile == 0 and all(s % BF16_SUBLANES == 0 for s in SUBTILES_MLP)
    return pl.pallas_call(
        functools.partial(_mlp_kernel, subtiles=SUBTILES_MLP),
        grid=(rows // tile,),
        in_specs=[
            pl.BlockSpec((tile, D_MODEL), lambda i: (i, 0)),
            _resident((1, D_MODEL)),
            _resident((D_MODEL, D_FF)),
            _resident((D_FF, D_MODEL)),
            _resident((1, D_MODEL)),
        ],
        out_specs=pl.BlockSpec((tile, D_MODEL), lambda i: (i, 0)),
        out_shape=jax.ShapeDtypeStruct(x.shape, F32),
        compiler_params=pltpu.CompilerParams(
            dimension_semantics=("arbitrary",), vmem_limit_bytes=VMEM_LIMIT),
        name="channel_mixer",
    )(x, g, w_up, w_down, g_final)


def kernel(x, attn_norm_g, w_in, conv_w, w_conv_out, attn_sinks, rel_bias, w_attn_out, w_o, mlp_norm_g, w_up,
           w_down, final_norm_g):
    batch, seq, _ = x.shape
    depth = w_in.shape[0]
    assert depth == 1, "the final norm is fused into the (only) layer's channel mixer"
    x, w_up_bf16, w_down_bf16 = _mixer(
        x, attn_norm_g.reshape(1, D_MODEL), w_in.reshape(D_MODEL, IN_COLS), conv_w.reshape(CONV_WIDTH, D_MODEL),
        w_conv_out.reshape(D_MODEL, D_MODEL), attn_sinks.reshape(N_HEADS).astype(F32), rel_bias.astype(F32),
        w_attn_out.reshape(D_MODEL, D_MODEL), w_o.reshape(D_MODEL, D_MODEL), w_up.reshape(D_MODEL, D_FF),
        w_down.reshape(D_FF, D_MODEL))
    x = _mlp(x.reshape(batch * seq, D_MODEL), mlp_norm_g.reshape(1, D_MODEL), w_up_bf16, w_down_bf16,
             final_norm_g.reshape(1, D_MODEL))
    return x.reshape(batch, seq, D_MODEL)
```

```python
import functools
import math

import jax
import jax.numpy as jnp
import numpy as np
from jax import lax
from jax.experimental import pallas as pl
from jax.experimental.pallas import tpu as pltpu

D_MODEL = 1024
HEAD_DIM = 64
N_HEADS = 16
N_KV_HEADS = 2
GROUP = N_HEADS // N_KV_HEADS
BLOCK = 128
N_BUCKETS = 32
MAX_DISTANCE = 128
CONV_WIDTH = 3
D_FF = 4 * D_MODEL
EPS = 1e-6
NEG_INF = -1e30
LOG2E = math.log2(math.e)

LANES = 128
SUBLANES = 8
BF16_SUBLANES = 16
KV_COLS = N_KV_HEADS * HEAD_DIM
PAIRS_PER_GROUP = GROUP // 2

OFF_B, OFF_C, OFF_X, OFF_Q = 0, D_MODEL, 2 * D_MODEL, 3 * D_MODEL
OFF_K = 4 * D_MODEL
OFF_V = OFF_K + KV_COLS
OFF_GC = OFF_V + KV_COLS
OFF_GA = OFF_GC + D_MODEL
IN_COLS = OFF_GA + D_MODEL

TILE_MIX = 512
SUBTILES_MLP = (512, 512)
VMEM_LIMIT = 56 * 1024 * 1024
STAGE_COLS = 640
STAGE_SLOTS = 3

BF16 = jnp.bfloat16
F32 = jnp.float32

NT_DIMS = (((1,), (1,)), ((), ()))
TN_DIMS = (((0,), (0,)), ((), ()))


def _bucket_table():
    kj = np.arange(2 * BLOCK)[:, None]
    qi = np.arange(BLOCK)[None, :]
    dist = qi + BLOCK - kj
    n = np.maximum(dist, 0)
    max_exact = N_BUCKETS // 2
    ratio = np.log(np.maximum(n, max_exact).astype(np.float32) / max_exact) / np.log(MAX_DISTANCE / max_exact)
    large = np.minimum(max_exact + (ratio * (N_BUCKETS - max_exact)).astype(np.int32), N_BUCKETS - 1)
    bucket = np.where(n < max_exact, n, large).astype(np.int32)
    in_window = (dist >= 0) & (dist < BLOCK)
    return np.where(in_window, bucket, -1).astype(np.int32)


def _bias_table(head, bucket_ref, rel_ref, bias_buf):
    bucket = bucket_ref[...]
    acc = jnp.full(bucket.shape, NEG_INF, F32)
    for b in range(N_BUCKETS):
        acc = jnp.where(bucket == b, rel_ref[b, head], acc)
    acc = acc * LOG2E
    bias_buf[0, head] = acc
    key = lax.broadcasted_iota(jnp.int32, bucket.shape, 0)
    bias_buf[1, head] = jnp.where(key >= BLOCK, acc, NEG_INF)


def _rmsnorm(x, g):
    y = x * lax.rsqrt(jnp.mean(x * x, axis=-1, keepdims=True) + EPS)
    return y * g


def _stage_weights(pairs, stage, sems, fillers):
    chunks = []
    for src, dst in pairs:
        cols = src.shape[1]
        width = max(w for w in range(LANES, STAGE_COLS + 1, LANES) if cols % w == 0)
        chunks += [(src, dst, c, width) for c in range(0, cols, width)]
    slots = stage.shape[0]

    def copy(i):
        src, _, c, width = chunks[i]
        return pltpu.make_async_copy(src.at[:, c:c + width], stage.at[i % slots, :, 0:width], sems.at[i % slots])

    for i in range(min(slots, len(chunks))):
        copy(i).start()
    fillers = list(fillers)
    assert len(fillers) <= len(chunks)
    for i, (_, dst, c, width) in enumerate(chunks):
        if fillers:
            fillers.pop(0)()
        copy(i).wait()
        dst[:, c:c + width] = stage[i % slots, :, 0:width].astype(BF16)
        if i + slots < len(chunks):
            copy(i + slots).start()


def _mixer_kernel(x_ref, g_ref, win_hbm, convw_ref, wco_hbm, sinks_ref, bucket_ref, rel_ref, wao_hbm, wo_hbm,
                  wup_ref, wdown_ref, out_ref, wup_out_ref, wdown_out_ref,
                  win_buf, wco_buf, wao_buf, wo_buf, stage, stage_sems, bias_buf, cx_buf, kv_buf, attn_buf,
                  *, tile, tiles):
    s = pl.program_id(0)

    @pl.when(s == 0)
    def _():
        bias_tasks = [functools.partial(_bias_table, head, bucket_ref, rel_ref, bias_buf)
                      for head in range(N_HEADS)]
        _stage_weights([(win_hbm, win_buf), (wco_hbm, wco_buf), (wao_hbm, wao_buf), (wo_hbm, wo_buf)],
                       stage, stage_sems, bias_tasks)

    @pl.when(s > 0)
    def _():
        _mixer_step((s - 1) % tiles, x_ref, g_ref, win_buf, convw_ref, wco_buf, sinks_ref, bias_buf, wao_buf,
                    wo_buf, wup_ref, wdown_ref, out_ref, wup_out_ref, wdown_out_ref, cx_buf, kv_buf, attn_buf,
                    tile=tile)


def _mixer_step(t, x_ref, g_ref, win_ref, convw_ref, wco_ref, sinks_ref, bias_ref, wao_ref, wo_ref,
                wup_ref, wdown_ref, out_ref, wup_out_ref, wdown_out_ref, cx_buf, kv_buf, attn_buf, *, tile):
    @pl.when(t == 0)
    def _():
        cx_buf[:, 0:SUBLANES, :] = jnp.zeros((D_MODEL // LANES, SUBLANES, LANES), F32)
        kv_buf[0:BLOCK, :] = jnp.zeros((BLOCK, 4 * KV_COLS), BF16)

    x = x_ref[0]
    half = tile // 2
    h_halves = [_rmsnorm(x[r * half:(r + 1) * half], g_ref[...]).astype(BF16) for r in range(2)]
    h = jnp.concatenate(h_halves, axis=0)

    def proj(lo, width):
        return jnp.dot(h, win_ref[:, lo:lo + width], preferred_element_type=F32)

    def proj_halves(lo, width):
        w = win_ref[:, lo:lo + width]
        return jnp.concatenate([jnp.dot(hr, w, preferred_element_type=F32) for hr in h_halves], axis=0)

    q = (proj_halves(OFF_Q, D_MODEL) * (HEAD_DIM ** -0.5 * LOG2E)).astype(BF16)
    kv = proj_halves(OFF_K, 2 * KV_COLS)
    k = kv[:, 0:KV_COLS]
    val = kv[:, KV_COLS:2 * KV_COLS]
    k_swapped = pltpu.roll(k, HEAD_DIM, axis=1)
    low_half_tile = lax.broadcasted_iota(jnp.int32, (tile, LANES), 1) < HEAD_DIM
    new_rows = slice(BLOCK, BLOCK + tile)
    kv_buf[new_rows, 0:KV_COLS] = jnp.where(low_half_tile, k, k_swapped).astype(BF16)
    kv_buf[new_rows, KV_COLS:2 * KV_COLS] = jnp.where(low_half_tile, k_swapped, k).astype(BF16)
    kv_buf[new_rows, 2 * KV_COLS:3 * KV_COLS] = jnp.where(low_half_tile, val, 1.0).astype(BF16)
    kv_buf[new_rows, 3 * KV_COLS:4 * KV_COLS] = jnp.where(low_half_tile, 1.0, val).astype(BF16)

    low_half = lax.broadcasted_iota(jnp.int32, (BLOCK, LANES), 1) < HEAD_DIM
    zero = jnp.zeros((BLOCK, LANES), BF16)

    scores, probs, sink_terms = {}, {}, {}

    def qk(j):
        rows = slice(j * BLOCK, (j + 2) * BLOCK)
        qj = q[j * BLOCK:(j + 1) * BLOCK, :]
        for g in range(N_KV_HEADS):
            parts = []
            for p in range(g * PAIRS_PER_GROUP, (g + 1) * PAIRS_PER_GROUP):
                qp = qj[:, p * LANES:(p + 1) * LANES]
                parts.append(jnp.where(low_half, qp, zero))
                parts.append(jnp.where(low_half, zero, qp))
            q_heads = jnp.concatenate(parts, axis=0)
            keys = kv_buf[rows, g * KV_COLS:(g + 1) * KV_COLS]
            scores[j, g] = lax.dot_general(keys, q_heads, NT_DIMS, preferred_element_type=F32)

    def softmax(j, g):
        first = (t == 0).astype(jnp.int32) if j == 0 else 0
        s = scores.pop((j, g))
        for i in range(GROUP):
            head = g * GROUP + i
            sh = s[:, i * BLOCK:(i + 1) * BLOCK] + bias_ref[first, head]
            sink = jnp.full((1, BLOCK), sinks_ref[head], F32) * LOG2E
            m = jnp.maximum(jnp.max(sh, axis=0, keepdims=True), sink)
            probs[j, head] = jnp.exp2(sh - m).astype(BF16)
            sink_terms[j, head] = jnp.exp2(sink - m)

    def pv(j):
        rows = slice(j * BLOCK, (j + 2) * BLOCK)
        for g in range(N_KV_HEADS):
            vals = kv_buf[rows, (2 + g) * KV_COLS:(3 + g) * KV_COLS]
            p = jnp.concatenate([probs.pop((j, g * GROUP + i)) for i in range(GROUP)], axis=1)
            o = lax.dot_general(vals, p, TN_DIMS, preferred_element_type=F32)
            out_rows = slice(g * HEAD_DIM, (g + 1) * HEAD_DIM)
            sum_row = (1 - g) * HEAD_DIM
            for i in range(GROUP):
                head = g * GROUP + i
                cols = slice(i * BLOCK, (i + 1) * BLOCK)
                denom = o[sum_row:sum_row + 1, cols] + sink_terms.pop((j, head))
                oh = o[out_rows, cols] * (1.0 / denom)
                attn_buf[head * HEAD_DIM:(head + 1) * HEAD_DIM, j * BLOCK:(j + 1) * BLOCK] = oh.astype(BF16)

    v = {}

    def t_c():
        v["c"] = proj(OFF_C, D_MODEL)

    def t_x():
        cx = v.pop("c") * proj(OFF_X, D_MODEL)
        for c in range(D_MODEL // LANES):
            cx_buf[c, SUBLANES:SUBLANES + tile, :] = cx[:, c * LANES:(c + 1) * LANES]

    def t_conv():
        slabs = []
        for c in range(D_MODEL // LANES):
            taps = [convw_ref[0:1, k * D_MODEL + c * LANES:k * D_MODEL + (c + 1) * LANES] for k in range(CONV_WIDTH)]
            slabs.append(taps[0] * cx_buf[c, SUBLANES - 2:SUBLANES - 2 + tile, :]
                         + taps[1] * cx_buf[c, SUBLANES - 1:SUBLANES - 1 + tile, :]
                         + taps[2] * cx_buf[c, SUBLANES:SUBLANES + tile, :])
        v["conv"] = jnp.concatenate(slabs, axis=1)
        cx_buf[:, 0:SUBLANES, :] = cx_buf[:, tile:tile + SUBLANES, :]

    def t_b():
        v["u"] = (proj(OFF_B, D_MODEL) * v.pop("conv")).astype(BF16)

    def t_gc():
        v["gate_conv"] = jax.nn.sigmoid(proj(OFF_GC, D_MODEL))

    def t_ga():
        v["gate_attn"] = jax.nn.sigmoid(proj(OFF_GA, D_MODEL))

    def t_yconv():
        v["mixed"] = v.pop("gate_conv") * jnp.dot(v.pop("u"), wco_ref[...], preferred_element_type=F32)

    def t_yattn(r):
        rows = slice(r * half, (r + 1) * half)
        y_attn = lax.dot_general(attn_buf[:, rows], wao_ref[...], TN_DIMS, preferred_element_type=F32)
        v["mixed", r] = (v["mixed"][rows] + v["gate_attn"][rows] * y_attn).astype(BF16)

    def t_out(r):
        rows = slice(r * half, (r + 1) * half)
        out_ref[0, rows, :] = x[rows] + jnp.dot(v.pop(("mixed", r)), wo_ref[...], preferred_element_type=F32)

    qk(0)
    t_c()
    wup_out_ref[...] = wup_ref[...].astype(BF16)
    wdown_out_ref[...] = wdown_ref[...].astype(BF16)
    softmax(0, 0)
    t_x()
    softmax(0, 1)
    t_gc()
    t_conv()
    pv(0)
    qk(1)
    t_b()
    softmax(1, 0)
    t_ga()
    softmax(1, 1)
    pv(1)
    qk(2)
    t_yconv()
    softmax(2, 0)
    t_yattn(0)
    softmax(2, 1)
    pv(2)
    qk(3)
    t_out(0)
    softmax(3, 0)
    softmax(3, 1)
    pv(3)
    kv_buf[0:BLOCK, :] = kv_buf[tile:tile + BLOCK, :]
    t_yattn(1)
    t_out(1)


def _mlp_kernel(x_ref, g_ref, wup_ref, wdown_ref, gf_ref, out_ref, *, subtiles):
    normed, act, mixed = {}, {}, {}
    starts = [sum(subtiles[:i]) for i in range(len(subtiles))]

    def rows(i):
        return slice(starts[i], starts[i] + subtiles[i])

    def norm(i):
        normed[i] = _rmsnorm(x_ref[rows(i), :], g_ref[...]).astype(BF16)

    def up(i):
        u = jnp.dot(normed.pop(i), wup_ref[...], preferred_element_type=F32)
        act[i] = jnp.square(jnp.maximum(u, 0.0)).astype(BF16)

    def down(i):
        mixed[i] = x_ref[rows(i), :] + jnp.dot(act.pop(i), wdown_ref[...], preferred_element_type=F32)

    def final(i):
        out_ref[rows(i), :] = _rmsnorm(mixed.pop(i), gf_ref[...])

    n = len(subtiles)
    norm(0)
    up(0)
    for i in range(n):
        if i + 1 < n:
            norm(i + 1)
            up(i + 1)
        if i > 0:
            final(i - 1)
        down(i)
    final(n - 1)


def _resident(shape):
    return pl.BlockSpec(shape, lambda *_: (0,) * len(shape), pipeline_mode=pl.Buffered(1))


def _mixer(x, g, w_in, conv_w, w_conv_out, sinks, rel_bias, w_attn_out, w_o, w_up, w_down):
    batch, seq, _ = x.shape
    tile = TILE_MIX
    assert seq % tile == 0 and tile % BLOCK == 0
    tiles = seq // tile
    steps = batch * tiles
    up_rows, down_rows = D_MODEL // steps, D_FF // steps
    assert up_rows * steps == D_MODEL and up_rows % BF16_SUBLANES == 0
    assert down_rows * steps == D_FF and down_rows % BF16_SUBLANES == 0

    def tile_of(s):
        return jnp.maximum(s - 1, 0)

    def x_map(s):
        return (tile_of(s) // tiles, tile_of(s) % tiles, 0)

    def slice_map(s):
        return (tile_of(s), 0)

    hbm = pl.BlockSpec(memory_space=pl.ANY)
    return pl.pallas_call(
        functools.partial(_mixer_kernel, tile=tile, tiles=tiles),
        grid=(steps + 1,),
        in_specs=[
            pl.BlockSpec((1, tile, D_MODEL), x_map),
            _resident((1, D_MODEL)),
            hbm,
            _resident((1, CONV_WIDTH * D_MODEL)),
            hbm,
            pl.BlockSpec(memory_space=pltpu.SMEM),
            _resident((2 * BLOCK, BLOCK)),
            pl.BlockSpec(memory_space=pltpu.SMEM),
            hbm,
            hbm,
            pl.BlockSpec((up_rows, D_FF), slice_map),
            pl.BlockSpec((down_rows, D_MODEL), slice_map),
        ],
        out_specs=[
            pl.BlockSpec((1, tile, D_MODEL), x_map),
            pl.BlockSpec((up_rows, D_FF), slice_map),
            pl.BlockSpec((down_rows, D_MODEL), slice_map),
        ],
        out_shape=[
            jax.ShapeDtypeStruct(x.shape, F32),
            jax.ShapeDtypeStruct(w_up.shape, BF16),
            jax.ShapeDtypeStruct(w_down.shape, BF16),
        ],
        scratch_shapes=[
            pltpu.VMEM((D_MODEL, IN_COLS), BF16),
            pltpu.VMEM((D_MODEL, D_MODEL), BF16),
            pltpu.VMEM((D_MODEL, D_MODEL), BF16),
            pltpu.VMEM((D_MODEL, D_MODEL), BF16),
            pltpu.VMEM((STAGE_SLOTS, D_MODEL, STAGE_COLS), F32),
            pltpu.SemaphoreType.DMA((STAGE_SLOTS,)),
            pltpu.VMEM((2, N_HEADS, 2 * BLOCK, BLOCK), F32),
            pltpu.VMEM((D_MODEL // LANES, tile + 2 * SUBLANES, LANES), F32),
            pltpu.VMEM((tile + BLOCK, 4 * KV_COLS), BF16),
            pltpu.VMEM((D_MODEL, tile), BF16),
        ],
        compiler_params=pltpu.CompilerParams(
            dimension_semantics=("arbitrary",), vmem_limit_bytes=VMEM_LIMIT),
        name="token_mixer",
    )(x, g, w_in, conv_w, w_conv_out, sinks, jnp.asarray(_bucket_table()), rel_bias, w_attn_out, w_o, w_up, w_down)


def _mlp(x, g, w_up, w_down, g_final):
    rows = x.shape[0]
    tile = sum(SUBTILES_MLP)
    assert rows % tile == 0 and all(s % BF16_SUBLANES == 0 for s in SUBTILES_MLP)
    return pl.pallas_call(
        functools.partial(_mlp_kernel, subtiles=SUBTILES_MLP),
        grid=(rows // tile,),
        in_specs=[
            pl.BlockSpec((tile, D_MODEL), lambda i: (i, 0)),
            _resident((1, D_MODEL)),
            _resident((D_MODEL, D_FF)),
            _resident((D_FF, D_MODEL)),
            _resident((1, D_MODEL)),
        ],
        out_specs=pl.BlockSpec((tile, D_MODEL), lambda i: (i, 0)),
        out_shape=jax.ShapeDtypeStruct(x.shape, F32),
        compiler_params=pltpu.CompilerParams(
            dimension_semantics=("arbitrary",), vmem_limit_bytes=VMEM_LIMIT),
        name="channel_mixer",
    )(x, g, w_up, w_down, g_final)


def kernel(x, attn_norm_g, w_in, conv_w, w_conv_out, attn_sinks, rel_bias, w_attn_out, w_o, mlp_norm_g, w_up,
           w_down, final_norm_g):
    batch, seq, _ = x.shape
    depth = w_in.shape[0]
    assert depth == 1, "the final norm is fused into the (only) layer's channel mixer"
    x, w_up_bf16, w_down_bf16 = _mixer(
        x, attn_norm_g.reshape(1, D_MODEL), w_in.reshape(D_MODEL, IN_COLS), conv_w.reshape(1, CONV_WIDTH * D_MODEL),
        w_conv_out.reshape(D_MODEL, D_MODEL), attn_sinks.reshape(N_HEADS).astype(F32), rel_bias.astype(F32),
        w_attn_out.reshape(D_MODEL, D_MODEL), w_o.reshape(D_MODEL, D_MODEL), w_up.reshape(D_MODEL, D_FF),
        w_down.reshape(D_FF, D_MODEL))
    x = _mlp(x.reshape(batch * seq, D_MODEL), mlp_norm_g.reshape(1, D_MODEL), w_up_bf16, w_down_bf16,
             final_norm_g.reshape(1, D_MODEL))
    return x.reshape(batch, seq, D_MODEL)
```

```python
import functools
import math

import jax
import jax.numpy as jnp
import numpy as np
from jax import lax
from jax.experimental import pallas as pl
from jax.experimental.pallas import tpu as pltpu

D_MODEL = 1024
HEAD_DIM = 64
N_HEADS = 16
N_KV_HEADS = 2
GROUP = N_HEADS // N_KV_HEADS
BLOCK = 128
N_BUCKETS = 32
MAX_DISTANCE = 128
CONV_WIDTH = 3
D_FF = 4 * D_MODEL
EPS = 1e-6
NEG_INF = -1e30
LOG2E = math.log2(math.e)

LANES = 128
SUBLANES = 8
BF16_SUBLANES = 16
KV_COLS = N_KV_HEADS * HEAD_DIM
PAIRS_PER_GROUP = GROUP // 2

OFF_B, OFF_C, OFF_X, OFF_Q = 0, D_MODEL, 2 * D_MODEL, 3 * D_MODEL
OFF_K = 4 * D_MODEL
OFF_V = OFF_K + KV_COLS
OFF_GC = OFF_V + KV_COLS
OFF_GA = OFF_GC + D_MODEL
IN_COLS = OFF_GA + D_MODEL

TILE_MIX = 512
TILE_MLP = 1024
SUBTILE_MLP = 512
VMEM_LIMIT = 56 * 1024 * 1024
STAGE_COLS = 640
STAGE_SLOTS = 3

BF16 = jnp.bfloat16
F32 = jnp.float32

NT_DIMS = (((1,), (1,)), ((), ()))
TN_DIMS = (((0,), (0,)), ((), ()))


def _bucket_table():
    kj = np.arange(2 * BLOCK)[:, None]
    qi = np.arange(BLOCK)[None, :]
    dist = qi + BLOCK - kj
    n = np.maximum(dist, 0)
    max_exact = N_BUCKETS // 2
    ratio = np.log(np.maximum(n, max_exact).astype(np.float32) / max_exact) / np.log(MAX_DISTANCE / max_exact)
    large = np.minimum(max_exact + (ratio * (N_BUCKETS - max_exact)).astype(np.int32), N_BUCKETS - 1)
    bucket = np.where(n < max_exact, n, large).astype(np.int32)
    in_window = (dist >= 0) & (dist < BLOCK)
    return np.where(in_window, bucket, -1).astype(np.int32)


def _bias_table(head, bucket_ref, rel_ref, bias_buf):
    bucket = bucket_ref[...]
    acc = jnp.full(bucket.shape, NEG_INF, F32)
    for b in range(N_BUCKETS):
        acc = jnp.where(bucket == b, rel_ref[b, head], acc)
    acc = acc * LOG2E
    bias_buf[0, head] = acc
    key = lax.broadcasted_iota(jnp.int32, bucket.shape, 0)
    bias_buf[1, head] = jnp.where(key >= BLOCK, acc, NEG_INF)


def _rmsnorm(x, g):
    y = x * lax.rsqrt(jnp.mean(x * x, axis=-1, keepdims=True) + EPS)
    return y * g


def _stage_weights(pairs, stage, sems, fillers):
    chunks = []
    for src, dst in pairs:
        cols = src.shape[1]
        width = max(w for w in range(LANES, STAGE_COLS + 1, LANES) if cols % w == 0)
        chunks += [(src, dst, c, width) for c in range(0, cols, width)]
    slots = stage.shape[0]

    def copy(i):
        src, _, c, width = chunks[i]
        return pltpu.make_async_copy(src.at[:, c:c + width], stage.at[i % slots, :, 0:width], sems.at[i % slots])

    for i in range(min(slots, len(chunks))):
        copy(i).start()
    fillers = list(fillers)
    assert len(fillers) <= len(chunks)
    for i, (_, dst, c, width) in enumerate(chunks):
        if fillers:
            fillers.pop(0)()
        copy(i).wait()
        dst[:, c:c + width] = stage[i % slots, :, 0:width].astype(BF16)
        if i + slots < len(chunks):
            copy(i + slots).start()


def _mixer_kernel(x_ref, g_ref, win_hbm, convw_ref, wco_hbm, sinks_ref, bucket_ref, rel_ref, wao_hbm, wo_hbm,
                  wup_ref, wdown_ref, out_ref, wup_out_ref, wdown_out_ref,
                  win_buf, wco_buf, wao_buf, wo_buf, stage, stage_sems, bias_buf, cx_buf, kv_buf, attn_buf,
                  *, tile, tiles):
    s = pl.program_id(0)

    @pl.when(s == 0)
    def _():
        bias_tasks = [functools.partial(_bias_table, head, bucket_ref, rel_ref, bias_buf)
                      for head in range(N_HEADS)]
        _stage_weights([(win_hbm, win_buf), (wco_hbm, wco_buf), (wao_hbm, wao_buf), (wo_hbm, wo_buf)],
                       stage, stage_sems, bias_tasks)

    @pl.when(s > 0)
    def _():
        _mixer_step((s - 1) % tiles, x_ref, g_ref, win_buf, convw_ref, wco_buf, sinks_ref, bias_buf, wao_buf,
                    wo_buf, wup_ref, wdown_ref, out_ref, wup_out_ref, wdown_out_ref, cx_buf, kv_buf, attn_buf,
                    tile=tile)


def _mixer_step(t, x_ref, g_ref, win_ref, convw_ref, wco_ref, sinks_ref, bias_ref, wao_ref, wo_ref,
                wup_ref, wdown_ref, out_ref, wup_out_ref, wdown_out_ref, cx_buf, kv_buf, attn_buf, *, tile):
    @pl.when(t == 0)
    def _():
        cx_buf[:, 0:SUBLANES, :] = jnp.zeros((D_MODEL // LANES, SUBLANES, LANES), F32)
        kv_buf[0:BLOCK, :] = jnp.zeros((BLOCK, 4 * KV_COLS), BF16)

    x = x_ref[0]
    half = tile // 2
    h_halves = [_rmsnorm(x[r * half:(r + 1) * half], g_ref[...]).astype(BF16) for r in range(2)]
    h = jnp.concatenate(h_halves, axis=0)

    def proj(lo, width):
        return jnp.dot(h, win_ref[:, lo:lo + width], preferred_element_type=F32)

    def proj_halves(lo, width):
        w = win_ref[:, lo:lo + width]
        return jnp.concatenate([jnp.dot(hr, w, preferred_element_type=F32) for hr in h_halves], axis=0)

    q = (proj_halves(OFF_Q, D_MODEL) * (HEAD_DIM ** -0.5 * LOG2E)).astype(BF16)
    kv = proj_halves(OFF_K, 2 * KV_COLS)
    k = kv[:, 0:KV_COLS]
    val = kv[:, KV_COLS:2 * KV_COLS]
    k_swapped = pltpu.roll(k, HEAD_DIM, axis=1)
    low_half_tile = lax.broadcasted_iota(jnp.int32, (tile, LANES), 1) < HEAD_DIM
    new_rows = slice(BLOCK, BLOCK + tile)
    kv_buf[new_rows, 0:KV_COLS] = jnp.where(low_half_tile, k, k_swapped).astype(BF16)
    kv_buf[new_rows, KV_COLS:2 * KV_COLS] = jnp.where(low_half_tile, k_swapped, k).astype(BF16)
    kv_buf[new_rows, 2 * KV_COLS:3 * KV_COLS] = jnp.where(low_half_tile, val, 1.0).astype(BF16)
    kv_buf[new_rows, 3 * KV_COLS:4 * KV_COLS] = jnp.where(low_half_tile, 1.0, val).astype(BF16)

    low_half = lax.broadcasted_iota(jnp.int32, (BLOCK, LANES), 1) < HEAD_DIM
    zero = jnp.zeros((BLOCK, LANES), BF16)

    scores, probs, sink_terms = {}, {}, {}

    def qk(j):
        rows = slice(j * BLOCK, (j + 2) * BLOCK)
        qj = q[j * BLOCK:(j + 1) * BLOCK, :]
        for g in range(N_KV_HEADS):
            parts = []
            for p in range(g * PAIRS_PER_GROUP, (g + 1) * PAIRS_PER_GROUP):
                qp = qj[:, p * LANES:(p + 1) * LANES]
                parts.append(jnp.where(low_half, qp, zero))
                parts.append(jnp.where(low_half, zero, qp))
            q_heads = jnp.concatenate(parts, axis=0)
            keys = kv_buf[rows, g * KV_COLS:(g + 1) * KV_COLS]
            scores[j, g] = lax.dot_general(keys, q_heads, NT_DIMS, preferred_element_type=F32)

    def softmax(j, g):
        first = (t == 0).astype(jnp.int32) if j == 0 else 0
        s = scores.pop((j, g))
        for i in range(GROUP):
            head = g * GROUP + i
            sh = s[:, i * BLOCK:(i + 1) * BLOCK] + bias_ref[first, head]
            sink = jnp.full((1, BLOCK), sinks_ref[head], F32) * LOG2E
            m = jnp.maximum(jnp.max(sh, axis=0, keepdims=True), sink)
            probs[j, head] = jnp.exp2(sh - m).astype(BF16)
            sink_terms[j, head] = jnp.exp2(sink - m)

    def pv(j):
        rows = slice(j * BLOCK, (j + 2) * BLOCK)
        for g in range(N_KV_HEADS):
            vals = kv_buf[rows, (2 + g) * KV_COLS:(3 + g) * KV_COLS]
            p = jnp.concatenate([probs.pop((j, g * GROUP + i)) for i in range(GROUP)], axis=1)
            o = lax.dot_general(vals, p, TN_DIMS, preferred_element_type=F32)
            out_rows = slice(g * HEAD_DIM, (g + 1) * HEAD_DIM)
            sum_row = (1 - g) * HEAD_DIM
            for i in range(GROUP):
                head = g * GROUP + i
                cols = slice(i * BLOCK, (i + 1) * BLOCK)
                denom = o[sum_row:sum_row + 1, cols] + sink_terms.pop((j, head))
                oh = o[out_rows, cols] * (1.0 / denom)
                attn_buf[head * HEAD_DIM:(head + 1) * HEAD_DIM, j * BLOCK:(j + 1) * BLOCK] = oh.astype(BF16)

    v = {}

    def t_c():
        v["c"] = proj(OFF_C, D_MODEL)

    def t_x():
        cx = v.pop("c") * proj(OFF_X, D_MODEL)
        for c in range(D_MODEL // LANES):
            cx_buf[c, SUBLANES:SUBLANES + tile, :] = cx[:, c * LANES:(c + 1) * LANES]

    def t_conv():
        slabs = []
        for c in range(D_MODEL // LANES):
            cols = slice(c * LANES, (c + 1) * LANES)
            slabs.append(convw_ref[0:1, cols] * cx_buf[c, SUBLANES - 2:SUBLANES - 2 + tile, :]
                         + convw_ref[1:2, cols] * cx_buf[c, SUBLANES - 1:SUBLANES - 1 + tile, :]
                         + convw_ref[2:3, cols] * cx_buf[c, SUBLANES:SUBLANES + tile, :])
        v["conv"] = jnp.concatenate(slabs, axis=1)
        cx_buf[:, 0:SUBLANES, :] = cx_buf[:, tile:tile + SUBLANES, :]

    def t_b():
        v["u"] = (proj(OFF_B, D_MODEL) * v.pop("conv")).astype(BF16)

    def t_gc():
        v["gate_conv"] = jax.nn.sigmoid(proj(OFF_GC, D_MODEL))

    def t_ga():
        v["gate_attn"] = jax.nn.sigmoid(proj(OFF_GA, D_MODEL))

    def t_yconv():
        v["mixed"] = v.pop("gate_conv") * jnp.dot(v.pop("u"), wco_ref[...], preferred_element_type=F32)

    def t_yattn(r):
        rows = slice(r * half, (r + 1) * half)
        y_attn = lax.dot_general(attn_buf[:, rows], wao_ref[...], TN_DIMS, preferred_element_type=F32)
        v["mixed", r] = (v["mixed"][rows] + v["gate_attn"][rows] * y_attn).astype(BF16)

    def t_out(r):
        rows = slice(r * half, (r + 1) * half)
        out_ref[0, rows, :] = x[rows] + jnp.dot(v.pop(("mixed", r)), wo_ref[...], preferred_element_type=F32)

    qk(0)
    t_c()
    wup_out_ref[...] = wup_ref[...].astype(BF16)
    wdown_out_ref[...] = wdown_ref[...].astype(BF16)
    softmax(0, 0)
    t_x()
    softmax(0, 1)
    t_gc()
    t_conv()
    pv(0)
    qk(1)
    t_b()
    softmax(1, 0)
    t_ga()
    softmax(1, 1)
    pv(1)
    qk(2)
    t_yconv()
    softmax(2, 0)
    t_yattn(0)
    softmax(2, 1)
    pv(2)
    qk(3)
    t_out(0)
    softmax(3, 0)
    softmax(3, 1)
    pv(3)
    kv_buf[0:BLOCK, :] = kv_buf[tile:tile + BLOCK, :]
    t_yattn(1)
    t_out(1)


def _mlp_kernel(x_hbm, g_ref, wup_hbm, wdown_hbm, gf_ref, out_hbm,
                wup_buf, wdown_buf, x_buf, out_buf, act_buf, w_sems, x_sems, out_sems, *, tile, sub, n_tiles):
    n_sub = tile // sub

    def x_copy(i, slot):
        rows = pl.ds(pl.multiple_of(i * tile, tile), tile)
        return pltpu.make_async_copy(x_hbm.at[rows, :], x_buf.at[slot], x_sems.at[slot])

    def out_copy(i, slot):
        rows = pl.ds(pl.multiple_of(i * tile, tile), tile)
        return pltpu.make_async_copy(out_buf.at[slot], out_hbm.at[rows, :], out_sems.at[slot])

    wup_copy = pltpu.make_async_copy(wup_hbm, wup_buf, w_sems.at[0])
    wdown_copy = pltpu.make_async_copy(wdown_hbm, wdown_buf, w_sems.at[1])

    def norm(slot, rows):
        return _rmsnorm(x_buf[slot, rows, :], g_ref[...]).astype(BF16)

    def up(h, rows):
        u = jnp.dot(h, wup_buf[...], preferred_element_type=F32)
        act_buf[rows, :] = jnp.square(jnp.maximum(u, 0.0)).astype(BF16)

    def down(slot, rows):
        return x_buf[slot, rows, :] + jnp.dot(act_buf[rows, :], wdown_buf[...], preferred_element_type=F32)

    def final(slot, rows, mixed):
        out_buf[slot, rows, :] = _rmsnorm(mixed, gf_ref[...])

    def loop_rows(k):
        return pl.ds(pl.multiple_of(k * sub, sub), sub)

    wup_copy.start()
    x_copy(0, 0).start()
    wdown_copy.start()
    x_copy(1, 1).start()
    wup_copy.wait()
    x_copy(0, 0).wait()

    def first_up(k, carry):
        up(norm(0, loop_rows(k)), loop_rows(k))
        return carry

    lax.fori_loop(0, n_sub, first_up, 0)
    wdown_copy.wait()

    def body(i, carry):
        slot = lax.rem(i, 2)
        nxt = 1 - slot

        @pl.when(i >= 2)
        def _():
            out_copy(i, slot).wait()

        x_copy(i + 1, nxt).wait()
        rows = [slice(k * sub, (k + 1) * sub) for k in range(n_sub)]
        mixed, normed = {}, {}
        mixed[0] = down(slot, rows[0])
        normed[0] = norm(nxt, rows[0])
        for k in range(1, n_sub):
            mixed[k] = down(slot, rows[k])
            final(slot, rows[k - 1], mixed.pop(k - 1))
            up(normed.pop(k - 1), rows[k - 1])
            normed[k] = norm(nxt, rows[k])
        final(slot, rows[n_sub - 1], mixed.pop(n_sub - 1))
        up(normed.pop(n_sub - 1), rows[n_sub - 1])
        out_copy(i, slot).start()

        @pl.when(i + 2 < n_tiles)
        def _():
            x_copy(i + 2, slot).start()

        return carry

    lax.fori_loop(0, n_tiles - 1, body, 0)

    last = n_tiles - 1
    slot = last % 2
    if last >= 2:
        out_copy(last, slot).wait()

    def last_down(k, carry):
        final(slot, loop_rows(k), down(slot, loop_rows(k)))
        return carry

    lax.fori_loop(0, n_sub, last_down, 0)
    out_copy(last, slot).start()
    out_copy(last - 1, 1 - slot).wait()
    out_copy(last, slot).wait()


def _resident(shape):
    return pl.BlockSpec(shape, lambda *_: (0,) * len(shape), pipeline_mode=pl.Buffered(1))


def _mixer(x, g, w_in, conv_w, w_conv_out, sinks, rel_bias, w_attn_out, w_o, w_up, w_down):
    batch, seq, _ = x.shape
    tile = TILE_MIX
    assert seq % tile == 0 and tile % BLOCK == 0
    tiles = seq // tile
    steps = batch * tiles
    up_rows, down_rows = D_MODEL // steps, D_FF // steps
    assert up_rows * steps == D_MODEL and up_rows % BF16_SUBLANES == 0
    assert down_rows * steps == D_FF and down_rows % BF16_SUBLANES == 0

    def tile_of(s):
        return jnp.maximum(s - 1, 0)

    def x_map(s):
        return (tile_of(s) // tiles, tile_of(s) % tiles, 0)

    def slice_map(s):
        return (tile_of(s), 0)

    hbm = pl.BlockSpec(memory_space=pl.ANY)
    return pl.pallas_call(
        functools.partial(_mixer_kernel, tile=tile, tiles=tiles),
        grid=(steps + 1,),
        in_specs=[
            pl.BlockSpec((1, tile, D_MODEL), x_map),
            _resident((1, D_MODEL)),
            hbm,
            _resident((CONV_WIDTH, D_MODEL)),
            hbm,
            pl.BlockSpec(memory_space=pltpu.SMEM),
            _resident((2 * BLOCK, BLOCK)),
            pl.BlockSpec(memory_space=pltpu.SMEM),
            hbm,
            hbm,
            pl.BlockSpec((up_rows, D_FF), slice_map),
            pl.BlockSpec((down_rows, D_MODEL), slice_map),
        ],
        out_specs=[
            pl.BlockSpec((1, tile, D_MODEL), x_map),
            pl.BlockSpec((up_rows, D_FF), slice_map),
            pl.BlockSpec((down_rows, D_MODEL), slice_map),
        ],
        out_shape=[
            jax.ShapeDtypeStruct(x.shape, F32),
            jax.ShapeDtypeStruct(w_up.shape, BF16),
            jax.ShapeDtypeStruct(w_down.shape, BF16),
        ],
        scratch_shapes=[
            pltpu.VMEM((D_MODEL, IN_COLS), BF16),
            pltpu.VMEM((D_MODEL, D_MODEL), BF16),
            pltpu.VMEM((D_MODEL, D_MODEL), BF16),
            pltpu.VMEM((D_MODEL, D_MODEL), BF16),
            pltpu.VMEM((STAGE_SLOTS, D_MODEL, STAGE_COLS), F32),
            pltpu.SemaphoreType.DMA((STAGE_SLOTS,)),
            pltpu.VMEM((2, N_HEADS, 2 * BLOCK, BLOCK), F32),
            pltpu.VMEM((D_MODEL // LANES, tile + 2 * SUBLANES, LANES), F32),
            pltpu.VMEM((tile + BLOCK, 4 * KV_COLS), BF16),
            pltpu.VMEM((D_MODEL, tile), BF16),
        ],
        compiler_params=pltpu.CompilerParams(
            dimension_semantics=("arbitrary",), vmem_limit_bytes=VMEM_LIMIT),
        name="token_mixer",
    )(x, g, w_in, conv_w, w_conv_out, sinks, jnp.asarray(_bucket_table()), rel_bias, w_attn_out, w_o, w_up, w_down)


def _mlp(x, g, w_up, w_down, g_final):
    rows = x.shape[0]
    tile, sub = TILE_MLP, SUBTILE_MLP
    assert rows % tile == 0 and tile % sub == 0 and rows // tile >= 2
    hbm = pl.BlockSpec(memory_space=pl.ANY)
    vmem = pl.BlockSpec(memory_space=pltpu.VMEM)
    return pl.pallas_call(
        functools.partial(_mlp_kernel, tile=tile, sub=sub, n_tiles=rows // tile),
        in_specs=[hbm, vmem, hbm, hbm, vmem],
        out_specs=hbm,
        out_shape=jax.ShapeDtypeStruct(x.shape, F32),
        scratch_shapes=[
            pltpu.VMEM((D_MODEL, D_FF), BF16),
            pltpu.VMEM((D_FF, D_MODEL), BF16),
            pltpu.VMEM((2, tile, D_MODEL), F32),
            pltpu.VMEM((2, tile, D_MODEL), F32),
            pltpu.VMEM((tile, D_FF), BF16),
            pltpu.SemaphoreType.DMA((2,)),
            pltpu.SemaphoreType.DMA((2,)),
            pltpu.SemaphoreType.DMA((2,)),
        ],
        compiler_params=pltpu.CompilerParams(vmem_limit_bytes=VMEM_LIMIT),
        name="channel_mixer",
    )(x, g, w_up, w_down, g_final)


def kernel(x, attn_norm_g, w_in, conv_w, w_conv_out, attn_sinks, rel_bias, w_attn_out, w_o, mlp_norm_g, w_up,
           w_down, final_norm_g):
    batch, seq, _ = x.shape
    depth = w_in.shape[0]
    assert depth == 1, "the final norm is fused into the (only) layer's channel mixer"
    x, w_up_bf16, w_down_bf16 = _mixer(
        x, attn_norm_g.reshape(1, D_MODEL), w_in.reshape(D_MODEL, IN_COLS), conv_w.reshape(CONV_WIDTH, D_MODEL),
        w_conv_out.reshape(D_MODEL, D_MODEL), attn_sinks.reshape(N_HEADS).astype(F32), rel_bias.astype(F32),
        w_attn_out.reshape(D_MODEL, D_MODEL), w_o.reshape(D_MODEL, D_MODEL), w_up.reshape(D_MODEL, D_FF),
        w_down.reshape(D_FF, D_MODEL))
    x = _mlp(x.reshape(batch * seq, D_MODEL), mlp_norm_g.reshape(1, D_MODEL), w_up_bf16, w_down_bf16,
             final_norm_g.reshape(1, D_MODEL))
    return x.reshape(batch, seq, D_MODEL)
```

```python
import functools
import math

import jax
import jax.numpy as jnp
import numpy as np
from jax import lax
from jax.experimental import pallas as pl
from jax.experimental.pallas import tpu as pltpu

D_MODEL = 1024
HEAD_DIM = 64
N_HEADS = 16
N_KV_HEADS = 2
GROUP = N_HEADS // N_KV_HEADS
BLOCK = 128
N_BUCKETS = 32
MAX_DISTANCE = 128
CONV_WIDTH = 3
D_FF = 4 * D_MODEL
EPS = 1e-6
NEG_INF = -1e30
LOG2E = math.log2(math.e)

LANES = 128
SUBLANES = 8
BF16_SUBLANES = 16
KV_COLS = N_KV_HEADS * HEAD_DIM
PAIRS_PER_GROUP = GROUP // 2

OFF_B, OFF_C, OFF_X, OFF_Q = 0, D_MODEL, 2 * D_MODEL, 3 * D_MODEL
OFF_K = 4 * D_MODEL
OFF_V = OFF_K + KV_COLS
OFF_GC = OFF_V + KV_COLS
OFF_GA = OFF_GC + D_MODEL
IN_COLS = OFF_GA + D_MODEL

TILE_MIX = 512
TILE_MLP = 1024
SUBTILE_MLP = 512
VMEM_LIMIT = 56 * 1024 * 1024
STAGE_COLS = 640
STAGE_SLOTS = 3

BF16 = jnp.bfloat16
F32 = jnp.float32

NT_DIMS = (((1,), (1,)), ((), ()))
TN_DIMS = (((0,), (0,)), ((), ()))


def _bucket_table():
    kj = np.arange(2 * BLOCK)[:, None]
    qi = np.arange(BLOCK)[None, :]
    dist = qi + BLOCK - kj
    n = np.maximum(dist, 0)
    max_exact = N_BUCKETS // 2
    ratio = np.log(np.maximum(n, max_exact).astype(np.float32) / max_exact) / np.log(MAX_DISTANCE / max_exact)
    large = np.minimum(max_exact + (ratio * (N_BUCKETS - max_exact)).astype(np.int32), N_BUCKETS - 1)
    bucket = np.where(n < max_exact, n, large).astype(np.int32)
    in_window = (dist >= 0) & (dist < BLOCK)
    return np.where(in_window, bucket, -1).astype(np.int32)


def _bias_table(head, bucket_ref, rel_ref, bias_buf):
    bucket = bucket_ref[...]
    acc = jnp.full(bucket.shape, NEG_INF, F32)
    for b in range(N_BUCKETS):
        acc = jnp.where(bucket == b, rel_ref[b, head], acc)
    acc = acc * LOG2E
    bias_buf[0, head] = acc
    key = lax.broadcasted_iota(jnp.int32, bucket.shape, 0)
    bias_buf[1, head] = jnp.where(key >= BLOCK, acc, NEG_INF)


def _rmsnorm(x, g):
    y = x * lax.rsqrt(jnp.mean(x * x, axis=-1, keepdims=True) + EPS)
    return y * g


def _stage_weights(pairs, stage, sems, fillers):
    chunks = []
    for src, dst in pairs:
        cols = src.shape[1]
        width = max(w for w in range(LANES, STAGE_COLS + 1, LANES) if cols % w == 0)
        chunks += [(src, dst, c, width) for c in range(0, cols, width)]
    slots = stage.shape[0]

    def copy(i):
        src, _, c, width = chunks[i]
        return pltpu.make_async_copy(src.at[:, c:c + width], stage.at[i % slots, :, 0:width], sems.at[i % slots])

    for i in range(min(slots, len(chunks))):
        copy(i).start()
    fillers = list(fillers)
    assert len(fillers) <= len(chunks)
    for i, (_, dst, c, width) in enumerate(chunks):
        if fillers:
            fillers.pop(0)()
        copy(i).wait()
        dst[:, c:c + width] = stage[i % slots, :, 0:width].astype(BF16)
        if i + slots < len(chunks):
            copy(i + slots).start()


def _mixer_kernel(x_ref, g_ref, win_hbm, convw_ref, wco_hbm, sinks_ref, bucket_ref, rel_ref, wao_hbm, wo_hbm,
                  wup_ref, wdown_ref, out_ref, wup_out_ref, wdown_out_ref,
                  win_buf, wco_buf, wao_buf, wo_buf, stage, stage_sems, bias_buf, cx_buf, kv_buf, attn_buf,
                  *, tile, tiles):
    s = pl.program_id(0)

    @pl.when(s == 0)
    def _():
        bias_tasks = [functools.partial(_bias_table, head, bucket_ref, rel_ref, bias_buf)
                      for head in range(N_HEADS)]
        _stage_weights([(win_hbm, win_buf), (wco_hbm, wco_buf), (wao_hbm, wao_buf), (wo_hbm, wo_buf)],
                       stage, stage_sems, bias_tasks)

    @pl.when(s > 0)
    def _():
        _mixer_step((s - 1) % tiles, x_ref, g_ref, win_buf, convw_ref, wco_buf, sinks_ref, bias_buf, wao_buf,
                    wo_buf, wup_ref, wdown_ref, out_ref, wup_out_ref, wdown_out_ref, cx_buf, kv_buf, attn_buf,
                    tile=tile)


def _mixer_step(t, x_ref, g_ref, win_ref, convw_ref, wco_ref, sinks_ref, bias_ref, wao_ref, wo_ref,
                wup_ref, wdown_ref, out_ref, wup_out_ref, wdown_out_ref, cx_buf, kv_buf, attn_buf, *, tile):
    @pl.when(t == 0)
    def _():
        cx_buf[:, 0:SUBLANES, :] = jnp.zeros((D_MODEL // LANES, SUBLANES, LANES), F32)
        kv_buf[0:BLOCK, :] = jnp.zeros((BLOCK, 4 * KV_COLS), BF16)

    x = x_ref[0]
    half = tile // 2
    h_halves = [_rmsnorm(x[r * half:(r + 1) * half], g_ref[...]).astype(BF16) for r in range(2)]
    h = jnp.concatenate(h_halves, axis=0)

    def proj(lo, width):
        return jnp.dot(h, win_ref[:, lo:lo + width], preferred_element_type=F32)

    def proj_halves(lo, width):
        w = win_ref[:, lo:lo + width]
        return jnp.concatenate([jnp.dot(hr, w, preferred_element_type=F32) for hr in h_halves], axis=0)

    q = proj_halves(OFF_Q, D_MODEL) * (HEAD_DIM ** -0.5 * LOG2E)
    q_t = {(j, p): q[j * BLOCK:(j + 1) * BLOCK, p * LANES:(p + 1) * LANES].T.astype(BF16)
           for j in range(tile // BLOCK) for p in range(N_HEADS // 2)}
    kv = proj_halves(OFF_K, 2 * KV_COLS)
    k = kv[:, 0:KV_COLS]
    val = kv[:, KV_COLS:2 * KV_COLS]
    k_swapped = pltpu.roll(k, HEAD_DIM, axis=1)
    low_half_tile = lax.broadcasted_iota(jnp.int32, (tile, LANES), 1) < HEAD_DIM
    new_rows = slice(BLOCK, BLOCK + tile)
    kv_buf[new_rows, 0:KV_COLS] = jnp.where(low_half_tile, k, k_swapped).astype(BF16)
    kv_buf[new_rows, KV_COLS:2 * KV_COLS] = jnp.where(low_half_tile, k_swapped, k).astype(BF16)
    kv_buf[new_rows, 2 * KV_COLS:3 * KV_COLS] = jnp.where(low_half_tile, val, 1.0).astype(BF16)
    kv_buf[new_rows, 3 * KV_COLS:4 * KV_COLS] = jnp.where(low_half_tile, 1.0, val).astype(BF16)

    zero = jnp.zeros((HEAD_DIM, BLOCK), BF16)

    scores, probs, sink_terms = {}, {}, {}

    def qk(j):
        rows = slice(j * BLOCK, (j + 2) * BLOCK)
        for g in range(N_KV_HEADS):
            parts = []
            for p in range(g * PAIRS_PER_GROUP, (g + 1) * PAIRS_PER_GROUP):
                qp = q_t.pop((j, p))
                parts.append(jnp.concatenate([qp[0:HEAD_DIM], zero], axis=0))
                parts.append(jnp.concatenate([zero, qp[HEAD_DIM:]], axis=0))
            q_heads = jnp.concatenate(parts, axis=1)
            keys = kv_buf[rows, g * KV_COLS:(g + 1) * KV_COLS]
            scores[j, g] = jnp.dot(keys, q_heads, preferred_element_type=F32)

    def softmax(j, g):
        first = (t == 0).astype(jnp.int32) if j == 0 else 0
        s = scores.pop((j, g))
        for i in range(GROUP):
            head = g * GROUP + i
            sh = s[:, i * BLOCK:(i + 1) * BLOCK] + bias_ref[first, head]
            sink = jnp.full((1, BLOCK), sinks_ref[head], F32) * LOG2E
            m = jnp.maximum(jnp.max(sh, axis=0, keepdims=True), sink)
            probs[j, head] = jnp.exp2(sh - m).astype(BF16)
            sink_terms[j, head] = jnp.exp2(sink - m)

    def pv(j):
        rows = slice(j * BLOCK, (j + 2) * BLOCK)
        for g in range(N_KV_HEADS):
            vals = kv_buf[rows, (2 + g) * KV_COLS:(3 + g) * KV_COLS]
            p = jnp.concatenate([probs.pop((j, g * GROUP + i)) for i in range(GROUP)], axis=1)
            o = lax.dot_general(vals, p, TN_DIMS, preferred_element_type=F32)
            out_rows = slice(g * HEAD_DIM, (g + 1) * HEAD_DIM)
            sum_row = (1 - g) * HEAD_DIM
            for i in range(GROUP):
                head = g * GROUP + i
                cols = slice(i * BLOCK, (i + 1) * BLOCK)
                denom = o[sum_row:sum_row + 1, cols] + sink_terms.pop((j, head))
                oh = o[out_rows, cols] * (1.0 / denom)
                attn_buf[head * HEAD_DIM:(head + 1) * HEAD_DIM, j * BLOCK:(j + 1) * BLOCK] = oh.astype(BF16)

    v = {}

    def t_c():
        v["c"] = proj(OFF_C, D_MODEL)

    def t_x():
        cx = v.pop("c") * proj(OFF_X, D_MODEL)
        for c in range(D_MODEL // LANES):
            cx_buf[c, SUBLANES:SUBLANES + tile, :] = cx[:, c * LANES:(c + 1) * LANES]

    def t_conv():
        slabs = []
        for c in range(D_MODEL // LANES):
            cols = slice(c * LANES, (c + 1) * LANES)
            slabs.append(convw_ref[0:1, cols] * cx_buf[c, SUBLANES - 2:SUBLANES - 2 + tile, :]
                         + convw_ref[1:2, cols] * cx_buf[c, SUBLANES - 1:SUBLANES - 1 + tile, :]
                         + convw_ref[2:3, cols] * cx_buf[c, SUBLANES:SUBLANES + tile, :])
        v["conv"] = jnp.concatenate(slabs, axis=1)
        cx_buf[:, 0:SUBLANES, :] = cx_buf[:, tile:tile + SUBLANES, :]

    def t_b():
        v["u"] = (proj(OFF_B, D_MODEL) * v.pop("conv")).astype(BF16)

    def t_gc():
        v["gate_conv"] = jax.nn.sigmoid(proj(OFF_GC, D_MODEL))

    def t_ga():
        v["gate_attn"] = jax.nn.sigmoid(proj(OFF_GA, D_MODEL))

    def t_yconv():
        v["mixed"] = v.pop("gate_conv") * jnp.dot(v.pop("u"), wco_ref[...], preferred_element_type=F32)

    def t_yattn(r):
        rows = slice(r * half, (r + 1) * half)
        y_attn = lax.dot_general(attn_buf[:, rows], wao_ref[...], TN_DIMS, preferred_element_type=F32)
        v["mixed", r] = (v["mixed"][rows] + v["gate_attn"][rows] * y_attn).astype(BF16)

    def t_out(r):
        rows = slice(r * half, (r + 1) * half)
        out_ref[0, rows, :] = x[rows] + jnp.dot(v.pop(("mixed", r)), wo_ref[...], preferred_element_type=F32)

    qk(0)
    t_c()
    wup_out_ref[...] = wup_ref[...].astype(BF16)
    wdown_out_ref[...] = wdown_ref[...].astype(BF16)
    softmax(0, 0)
    t_x()
    softmax(0, 1)
    t_gc()
    t_conv()
    pv(0)
    qk(1)
    t_b()
    softmax(1, 0)
    t_ga()
    softmax(1, 1)
    pv(1)
    qk(2)
    t_yconv()
    softmax(2, 0)
    t_yattn(0)
    softmax(2, 1)
    pv(2)
    qk(3)
    t_out(0)
    softmax(3, 0)
    softmax(3, 1)
    pv(3)
    kv_buf[0:BLOCK, :] = kv_buf[tile:tile + BLOCK, :]
    t_yattn(1)
    t_out(1)


def _mlp_kernel(x_hbm, g_ref, wup_hbm, wdown_hbm, gf_ref, out_hbm,
                wup_buf, wdown_buf, x_buf, out_buf, act_buf, w_sems, x_sems, out_sems, *, tile, sub, n_tiles):
    n_sub = tile // sub

    def x_copy(i, slot):
        rows = pl.ds(pl.multiple_of(i * tile, tile), tile)
        return pltpu.make_async_copy(x_hbm.at[rows, :], x_buf.at[slot], x_sems.at[slot])

    def out_copy(i, slot):
        rows = pl.ds(pl.multiple_of(i * tile, tile), tile)
        return pltpu.make_async_copy(out_buf.at[slot], out_hbm.at[rows, :], out_sems.at[slot])

    wup_copy = pltpu.make_async_copy(wup_hbm, wup_buf, w_sems.at[0])
    wdown_copy = pltpu.make_async_copy(wdown_hbm, wdown_buf, w_sems.at[1])

    def norm(slot, rows):
        return _rmsnorm(x_buf[slot, rows, :], g_ref[...]).astype(BF16)

    def up(h, rows):
        u = jnp.dot(h, wup_buf[...], preferred_element_type=F32)
        act_buf[rows, :] = jnp.square(jnp.maximum(u, 0.0)).astype(BF16)

    def down(slot, rows):
        return x_buf[slot, rows, :] + jnp.dot(act_buf[rows, :], wdown_buf[...], preferred_element_type=F32)

    def final(slot, rows, mixed):
        out_buf[slot, rows, :] = _rmsnorm(mixed, gf_ref[...])

    def loop_rows(k):
        return pl.ds(pl.multiple_of(k * sub, sub), sub)

    wup_copy.start()
    x_copy(0, 0).start()
    wdown_copy.start()
    x_copy(1, 1).start()
    wup_copy.wait()
    x_copy(0, 0).wait()

    def first_up(k, carry):
        up(norm(0, loop_rows(k)), loop_rows(k))
        return carry

    lax.fori_loop(0, n_sub, first_up, 0)
    wdown_copy.wait()

    def body(i, carry):
        slot = lax.rem(i, 2)
        nxt = 1 - slot

        @pl.when(i >= 2)
        def _():
            out_copy(i, slot).wait()

        x_copy(i + 1, nxt).wait()
        rows = [slice(k * sub, (k + 1) * sub) for k in range(n_sub)]
        mixed, normed = {}, {}
        mixed[0] = down(slot, rows[0])
        normed[0] = norm(nxt, rows[0])
        for k in range(1, n_sub):
            mixed[k] = down(slot, rows[k])
            final(slot, rows[k - 1], mixed.pop(k - 1))
            up(normed.pop(k - 1), rows[k - 1])
            normed[k] = norm(nxt, rows[k])
        final(slot, rows[n_sub - 1], mixed.pop(n_sub - 1))
        up(normed.pop(n_sub - 1), rows[n_sub - 1])
        out_copy(i, slot).start()

        @pl.when(i + 2 < n_tiles)
        def _():
            x_copy(i + 2, slot).start()

        return carry

    lax.fori_loop(0, n_tiles - 1, body, 0)

    last = n_tiles - 1
    slot = last % 2
    if last >= 2:
        out_copy(last, slot).wait()

    def last_down(k, carry):
        final(slot, loop_rows(k), down(slot, loop_rows(k)))
        return carry

    lax.fori_loop(0, n_sub, last_down, 0)
    out_copy(last, slot).start()
    out_copy(last - 1, 1 - slot).wait()
    out_copy(last, slot).wait()


def _resident(shape):
    return pl.BlockSpec(shape, lambda *_: (0,) * len(shape), pipeline_mode=pl.Buffered(1))


def _mixer(x, g, w_in, conv_w, w_conv_out, sinks, rel_bias, w_attn_out, w_o, w_up, w_down):
    batch, seq, _ = x.shape
    tile = TILE_MIX
    assert seq % tile == 0 and tile % BLOCK == 0
    tiles = seq // tile
    steps = batch * tiles
    up_rows, down_rows = D_MODEL // steps, D_FF // steps
    assert up_rows * steps == D_MODEL and up_rows % BF16_SUBLANES == 0
    assert down_rows * steps == D_FF and down_rows % BF16_SUBLANES == 0

    def tile_of(s):
        return jnp.maximum(s - 1, 0)

    def x_map(s):
        return (tile_of(s) // tiles, tile_of(s) % tiles, 0)

    def slice_map(s):
        return (tile_of(s), 0)

    hbm = pl.BlockSpec(memory_space=pl.ANY)
    return pl.pallas_call(
        functools.partial(_mixer_kernel, tile=tile, tiles=tiles),
        grid=(steps + 1,),
        in_specs=[
            pl.BlockSpec((1, tile, D_MODEL), x_map),
            _resident((1, D_MODEL)),
            hbm,
            _resident((CONV_WIDTH, D_MODEL)),
            hbm,
            pl.BlockSpec(memory_space=pltpu.SMEM),
            _resident((2 * BLOCK, BLOCK)),
            pl.BlockSpec(memory_space=pltpu.SMEM),
            hbm,
            hbm,
            pl.BlockSpec((up_rows, D_FF), slice_map),
            pl.BlockSpec((down_rows, D_MODEL), slice_map),
        ],
        out_specs=[
            pl.BlockSpec((1, tile, D_MODEL), x_map),
            pl.BlockSpec((up_rows, D_FF), slice_map),
            pl.BlockSpec((down_rows, D_MODEL), slice_map),
        ],
        out_shape=[
            jax.ShapeDtypeStruct(x.shape, F32),
            jax.ShapeDtypeStruct(w_up.shape, BF16),
            jax.ShapeDtypeStruct(w_down.shape, BF16),
        ],
        scratch_shapes=[
            pltpu.VMEM((D_MODEL, IN_COLS), BF16),
            pltpu.VMEM((D_MODEL, D_MODEL), BF16),
            pltpu.VMEM((D_MODEL, D_MODEL), BF16),
            pltpu.VMEM((D_MODEL, D_MODEL), BF16),
            pltpu.VMEM((STAGE_SLOTS, D_MODEL, STAGE_COLS), F32),
            pltpu.SemaphoreType.DMA((STAGE_SLOTS,)),
            pltpu.VMEM((2, N_HEADS, 2 * BLOCK, BLOCK), F32),
            pltpu.VMEM((D_MODEL // LANES, tile + 2 * SUBLANES, LANES), F32),
            pltpu.VMEM((tile + BLOCK, 4 * KV_COLS), BF16),
            pltpu.VMEM((D_MODEL, tile), BF16),
        ],
        compiler_params=pltpu.CompilerParams(
            dimension_semantics=("arbitrary",), vmem_limit_bytes=VMEM_LIMIT),
        name="token_mixer",
    )(x, g, w_in, conv_w, w_conv_out, sinks, jnp.asarray(_bucket_table()), rel_bias, w_attn_out, w_o, w_up, w_down)


def _mlp(x, g, w_up, w_down, g_final):
    rows = x.shape[0]
    tile, sub = TILE_MLP, SUBTILE_MLP
    assert rows % tile == 0 and tile % sub == 0 and rows // tile >= 2
    hbm = pl.BlockSpec(memory_space=pl.ANY)
    vmem = pl.BlockSpec(memory_space=pltpu.VMEM)
    return pl.pallas_call(
        functools.partial(_mlp_kernel, tile=tile, sub=sub, n_tiles=rows // tile),
        in_specs=[hbm, vmem, hbm, hbm, vmem],
        out_specs=hbm,
        out_shape=jax.ShapeDtypeStruct(x.shape, F32),
        scratch_shapes=[
            pltpu.VMEM((D_MODEL, D_FF), BF16),
            pltpu.VMEM((D_FF, D_MODEL), BF16),
            pltpu.VMEM((2, tile, D_MODEL), F32),
            pltpu.VMEM((2, tile, D_MODEL), F32),
            pltpu.VMEM((tile, D_FF), BF16),
            pltpu.SemaphoreType.DMA((2,)),
            pltpu.SemaphoreType.DMA((2,)),
            pltpu.SemaphoreType.DMA((2,)),
        ],
        compiler_params=pltpu.CompilerParams(vmem_limit_bytes=VMEM_LIMIT),
        name="channel_mixer",
    )(x, g, w_up, w_down, g_final)


def kernel(x, attn_norm_g, w_in, conv_w, w_conv_out, attn_sinks, rel_bias, w_attn_out, w_o, mlp_norm_g, w_up,
           w_down, final_norm_g):
    batch, seq, _ = x.shape
    depth = w_in.shape[0]
    assert depth == 1, "the final norm is fused into the (only) layer's channel mixer"
    x, w_up_bf16, w_down_bf16 = _mixer(
        x, attn_norm_g.reshape(1, D_MODEL), w_in.reshape(D_MODEL, IN_COLS), conv_w.reshape(CONV_WIDTH, D_MODEL),
        w_conv_out.reshape(D_MODEL, D_MODEL), attn_sinks.reshape(N_HEADS).astype(F32), rel_bias.astype(F32),
        w_attn_out.reshape(D_MODEL, D_MODEL), w_o.reshape(D_MODEL, D_MODEL), w_up.reshape(D_MODEL, D_FF),
        w_down.reshape(D_FF, D_MODEL))
    x = _mlp(x.reshape(batch * seq, D_MODEL), mlp_norm_g.reshape(1, D_MODEL), w_up_bf16, w_down_bf16,
             final_norm_g.reshape(1, D_MODEL))
    return x.reshape(batch, seq, D_MODEL)
```

```python
import functools
import math

import jax
import jax.numpy as jnp
import numpy as np
from jax import lax
from jax.experimental import pallas as pl
from jax.experimental.pallas import tpu as pltpu

D_MODEL = 1024
HEAD_DIM = 64
N_HEADS = 16
N_KV_HEADS = 2
GROUP = N_HEADS // N_KV_HEADS
BLOCK = 128
N_BUCKETS = 32
MAX_DISTANCE = 128
CONV_WIDTH = 3
D_FF = 4 * D_MODEL
EPS = 1e-6
NEG_INF = -1e30
LOG2E = math.log2(math.e)

LANES = 128
SUBLANES = 8
BF16_SUBLANES = 16
KV_COLS = N_KV_HEADS * HEAD_DIM
PAIRS_PER_GROUP = GROUP // 2

OFF_B, OFF_C, OFF_X, OFF_Q = 0, D_MODEL, 2 * D_MODEL, 3 * D_MODEL
OFF_K = 4 * D_MODEL
OFF_V = OFF_K + KV_COLS
OFF_GC = OFF_V + KV_COLS
OFF_GA = OFF_GC + D_MODEL
IN_COLS = OFF_GA + D_MODEL

TILE_MIX = 512
SUBTILES_MLP = (512, 512)
VMEM_LIMIT = 56 * 1024 * 1024
STAGE_COLS = 640
STAGE_SLOTS = 3

BF16 = jnp.bfloat16
F32 = jnp.float32

NT_DIMS = (((1,), (1,)), ((), ()))
TN_DIMS = (((0,), (0,)), ((), ()))


def _bucket_table():
    kj = np.arange(2 * BLOCK)[:, None]
    qi = np.arange(BLOCK)[None, :]
    dist = qi + BLOCK - kj
    n = np.maximum(dist, 0)
    max_exact = N_BUCKETS // 2
    ratio = np.log(np.maximum(n, max_exact).astype(np.float32) / max_exact) / np.log(MAX_DISTANCE / max_exact)
    large = np.minimum(max_exact + (ratio * (N_BUCKETS - max_exact)).astype(np.int32), N_BUCKETS - 1)
    bucket = np.where(n < max_exact, n, large).astype(np.int32)
    in_window = (dist >= 0) & (dist < BLOCK)
    return np.where(in_window, bucket, -1).astype(np.int32)


def _bias_table(head, bucket_ref, rel_ref, bias_buf):
    bucket = bucket_ref[...]
    acc = jnp.full(bucket.shape, NEG_INF, F32)
    for b in range(N_BUCKETS):
        acc = jnp.where(bucket == b, rel_ref[b, head], acc)
    acc = acc * LOG2E
    bias_buf[0, head] = acc
    key = lax.broadcasted_iota(jnp.int32, bucket.shape, 0)
    bias_buf[1, head] = jnp.where(key >= BLOCK, acc, NEG_INF)


def _rmsnorm(x, g):
    y = x * lax.rsqrt(jnp.mean(x * x, axis=-1, keepdims=True) + EPS)
    return y * g


def _stage_weights(pairs, stage, sems, fillers):
    chunks = []
    for src, dst in pairs:
        cols = src.shape[1]
        width = max(w for w in range(LANES, STAGE_COLS + 1, LANES) if cols % w == 0)
        chunks += [(src, dst, c, width) for c in range(0, cols, width)]
    slots = stage.shape[0]

    def copy(i):
        src, _, c, width = chunks[i]
        return pltpu.make_async_copy(src.at[:, c:c + width], stage.at[i % slots, :, 0:width], sems.at[i % slots])

    for i in range(min(slots, len(chunks))):
        copy(i).start()
    fillers = list(fillers)
    assert len(fillers) <= len(chunks)
    for i, (_, dst, c, width) in enumerate(chunks):
        if fillers:
            fillers.pop(0)()
        copy(i).wait()
        dst[:, c:c + width] = stage[i % slots, :, 0:width].astype(BF16)
        if i + slots < len(chunks):
            copy(i + slots).start()


def _mixer_kernel(x_ref, g_ref, win_hbm, convw_ref, wco_hbm, sinks_ref, bucket_ref, rel_ref, wao_hbm, wo_hbm,
                  wup_ref, wdown_ref, out_ref, wup_out_ref, wdown_out_ref,
                  win_buf, wco_buf, wao_buf, wo_buf, stage, stage_sems, bias_buf, cx_buf, kv_buf, attn_buf,
                  *, tile, tiles):
    s = pl.program_id(0)

    @pl.when(s == 0)
    def _():
        bias_tasks = [functools.partial(_bias_table, head, bucket_ref, rel_ref, bias_buf)
                      for head in range(N_HEADS)]
        _stage_weights([(win_hbm, win_buf), (wco_hbm, wco_buf), (wao_hbm, wao_buf), (wo_hbm, wo_buf)],
                       stage, stage_sems, bias_tasks)

    @pl.when(s > 0)
    def _():
        _mixer_step((s - 1) % tiles, x_ref, g_ref, win_buf, convw_ref, wco_buf, sinks_ref, bias_buf, wao_buf,
                    wo_buf, wup_ref, wdown_ref, out_ref, wup_out_ref, wdown_out_ref, cx_buf, kv_buf, attn_buf,
                    tile=tile)


def _mixer_step(t, x_ref, g_ref, win_ref, convw_ref, wco_ref, sinks_ref, bias_ref, wao_ref, wo_ref,
                wup_ref, wdown_ref, out_ref, wup_out_ref, wdown_out_ref, cx_buf, kv_buf, attn_buf, *, tile):
    @pl.when(t == 0)
    def _():
        cx_buf[:, 0:SUBLANES, :] = jnp.zeros((D_MODEL // LANES, SUBLANES, LANES), F32)
        kv_buf[0:BLOCK, :] = jnp.zeros((BLOCK, 4 * KV_COLS), BF16)

    x = x_ref[0]
    half = tile // 2
    h_halves = [_rmsnorm(x[r * half:(r + 1) * half], g_ref[...]).astype(BF16) for r in range(2)]
    h = jnp.concatenate(h_halves, axis=0)

    def proj(lo, width):
        return jnp.dot(h, win_ref[:, lo:lo + width], preferred_element_type=F32)

    def proj_halves(lo, width):
        w = win_ref[:, lo:lo + width]
        return jnp.concatenate([jnp.dot(hr, w, preferred_element_type=F32) for hr in h_halves], axis=0)

    q = (proj_halves(OFF_Q, D_MODEL) * (HEAD_DIM ** -0.5 * LOG2E)).astype(BF16)
    kv = proj_halves(OFF_K, 2 * KV_COLS)
    k = kv[:, 0:KV_COLS]
    val = kv[:, KV_COLS:2 * KV_COLS]
    k_swapped = pltpu.roll(k, HEAD_DIM, axis=1)
    low_half_tile = lax.broadcasted_iota(jnp.int32, (tile, LANES), 1) < HEAD_DIM
    new_rows = slice(BLOCK, BLOCK + tile)
    kv_buf[new_rows, 0:KV_COLS] = jnp.where(low_half_tile, k, k_swapped).astype(BF16)
    kv_buf[new_rows, KV_COLS:2 * KV_COLS] = jnp.where(low_half_tile, k_swapped, k).astype(BF16)
    kv_buf[new_rows, 2 * KV_COLS:3 * KV_COLS] = jnp.where(low_half_tile, val, 1.0).astype(BF16)
    kv_buf[new_rows, 3 * KV_COLS:4 * KV_COLS] = jnp.where(low_half_tile, 1.0, val).astype(BF16)

    low_half = lax.broadcasted_iota(jnp.int32, (BLOCK, LANES), 1) < HEAD_DIM
    zero = jnp.zeros((BLOCK, LANES), BF16)

    scores, probs, sink_terms = {}, {}, {}

    def qk(j):
        rows = slice(j * BLOCK, (j + 2) * BLOCK)
        qj = q[j * BLOCK:(j + 1) * BLOCK, :]
        for g in range(N_KV_HEADS):
            parts = []
            for p in range(g * PAIRS_PER_GROUP, (g + 1) * PAIRS_PER_GROUP):
                qp = qj[:, p * LANES:(p + 1) * LANES]
                parts.append(jnp.where(low_half, qp, zero))
                parts.append(jnp.where(low_half, zero, qp))
            q_heads = jnp.concatenate(parts, axis=0)
            keys = kv_buf[rows, g * KV_COLS:(g + 1) * KV_COLS]
            scores[j, g] = lax.dot_general(keys, q_heads, NT_DIMS, preferred_element_type=F32)

    def softmax(j, g):
        first = (t == 0).astype(jnp.int32) if j == 0 else 0
        s = scores.pop((j, g))
        for i in range(GROUP):
            head = g * GROUP + i
            sh = s[:, i * BLOCK:(i + 1) * BLOCK] + bias_ref[first, head]
            sink = jnp.full((1, BLOCK), sinks_ref[head], F32) * LOG2E
            m = jnp.maximum(jnp.max(sh, axis=0, keepdims=True), sink)
            probs[j, head] = jnp.exp2(sh - m).astype(BF16)
            sink_terms[j, head] = jnp.exp2(sink - m)

    def pv(j):
        rows = slice(j * BLOCK, (j + 2) * BLOCK)
        for g in range(N_KV_HEADS):
            vals = kv_buf[rows, (2 + g) * KV_COLS:(3 + g) * KV_COLS]
            p = jnp.concatenate([probs.pop((j, g * GROUP + i)) for i in range(GROUP)], axis=1)
            o = lax.dot_general(vals, p, TN_DIMS, preferred_element_type=F32)
            out_rows = slice(g * HEAD_DIM, (g + 1) * HEAD_DIM)
            sum_row = (1 - g) * HEAD_DIM
            for i in range(GROUP):
                head = g * GROUP + i
                cols = slice(i * BLOCK, (i + 1) * BLOCK)
                denom = o[sum_row:sum_row + 1, cols] + sink_terms.pop((j, head))
                oh = o[out_rows, cols] * (1.0 / denom)
                attn_buf[head * HEAD_DIM:(head + 1) * HEAD_DIM, j * BLOCK:(j + 1) * BLOCK] = oh.astype(BF16)

    v = {}

    def t_c():
        v["c"] = proj(OFF_C, D_MODEL)

    def t_x():
        cx = v.pop("c") * proj(OFF_X, D_MODEL)
        for c in range(D_MODEL // LANES):
            cx_buf[c, SUBLANES:SUBLANES + tile, :] = cx[:, c * LANES:(c + 1) * LANES]

    def t_conv():
        slabs = []
        for c in range(D_MODEL // LANES):
            cols = slice(c * LANES, (c + 1) * LANES)
            slabs.append(convw_ref[0:1, cols] * cx_buf[c, SUBLANES - 2:SUBLANES - 2 + tile, :]
                         + convw_ref[1:2, cols] * cx_buf[c, SUBLANES - 1:SUBLANES - 1 + tile, :]
                         + convw_ref[2:3, cols] * cx_buf[c, SUBLANES:SUBLANES + tile, :])
        v["conv"] = jnp.concatenate(slabs, axis=1)
        cx_buf[:, 0:SUBLANES, :] = cx_buf[:, tile:tile + SUBLANES, :]

    def t_b():
        v["u"] = (proj(OFF_B, D_MODEL) * v.pop("conv")).astype(BF16)

    def t_gc():
        v["gate_conv"] = jax.nn.sigmoid(proj(OFF_GC, D_MODEL)).astype(BF16)

    def t_ga():
        v["gate_attn"] = jax.nn.sigmoid(proj(OFF_GA, D_MODEL)).astype(BF16)

    def t_yconv():
        y_conv = jnp.dot(v.pop("u"), wco_ref[...], preferred_element_type=F32)
        v["mixed"] = v.pop("gate_conv") * y_conv.astype(BF16)

    def t_yattn(r):
        rows = slice(r * half, (r + 1) * half)
        y_attn = lax.dot_general(attn_buf[:, rows], wao_ref[...], TN_DIMS, preferred_element_type=F32)
        v["mixed", r] = v["mixed"][rows] + v["gate_attn"][rows] * y_attn.astype(BF16)

    def t_out(r):
        rows = slice(r * half, (r + 1) * half)
        out_ref[0, rows, :] = x[rows] + jnp.dot(v.pop(("mixed", r)), wo_ref[...], preferred_element_type=F32)

    qk(0)
    t_c()
    wup_out_ref[...] = wup_ref[...].astype(BF16)
    wdown_out_ref[...] = wdown_ref[...].astype(BF16)
    softmax(0, 0)
    t_x()
    softmax(0, 1)
    t_gc()
    t_conv()
    pv(0)
    qk(1)
    t_b()
    softmax(1, 0)
    t_ga()
    softmax(1, 1)
    pv(1)
    qk(2)
    t_yconv()
    softmax(2, 0)
    t_yattn(0)
    softmax(2, 1)
    pv(2)
    qk(3)
    t_out(0)
    softmax(3, 0)
    softmax(3, 1)
    pv(3)
    kv_buf[0:BLOCK, :] = kv_buf[tile:tile + BLOCK, :]
    t_yattn(1)
    t_out(1)


def _mlp_kernel(x_ref, g_ref, wup_ref, wdown_ref, gf_ref, out_ref, *, subtiles):
    normed, act, mixed = {}, {}, {}
    starts = [sum(subtiles[:i]) for i in range(len(subtiles))]

    def rows(i):
        return slice(starts[i], starts[i] + subtiles[i])

    def norm(i):
        normed[i] = _rmsnorm(x_ref[rows(i), :], g_ref[...]).astype(BF16)

    def up(i):
        u = jnp.dot(normed.pop(i), wup_ref[...], preferred_element_type=F32)
        act[i] = jnp.square(jnp.maximum(u.astype(BF16), 0))

    def down(i):
        mixed[i] = x_ref[rows(i), :] + jnp.dot(act.pop(i), wdown_ref[...], preferred_element_type=F32)

    def final(i):
        out_ref[rows(i), :] = _rmsnorm(mixed.pop(i), gf_ref[...])

    n = len(subtiles)
    norm(0)
    up(0)
    for i in range(n):
        if i + 1 < n:
            norm(i + 1)
            up(i + 1)
        if i > 0:
            final(i - 1)
        down(i)
    final(n - 1)


def _resident(shape):
    return pl.BlockSpec(shape, lambda *_: (0,) * len(shape), pipeline_mode=pl.Buffered(1))


def _mixer(x, g, w_in, conv_w, w_conv_out, sinks, rel_bias, w_attn_out, w_o, w_up, w_down):
    batch, seq, _ = x.shape
    tile = TILE_MIX
    assert seq % tile == 0 and tile % BLOCK == 0
    tiles = seq // tile
    steps = batch * tiles
    up_rows, down_rows = D_MODEL // steps, D_FF // steps
    assert up_rows * steps == D_MODEL and up_rows % BF16_SUBLANES == 0
    assert down_rows * steps == D_FF and down_rows % BF16_SUBLANES == 0

    def tile_of(s):
        return jnp.maximum(s - 1, 0)

    def x_map(s):
        return (tile_of(s) // tiles, tile_of(s) % tiles, 0)

    def slice_map(s):
        return (tile_of(s), 0)

    hbm = pl.BlockSpec(memory_space=pl.ANY)
    return pl.pallas_call(
        functools.partial(_mixer_kernel, tile=tile, tiles=tiles),
        grid=(steps + 1,),
        in_specs=[
            pl.BlockSpec((1, tile, D_MODEL), x_map),
            _resident((1, D_MODEL)),
            hbm,
            _resident((CONV_WIDTH, D_MODEL)),
            hbm,
            pl.BlockSpec(memory_space=pltpu.SMEM),
            _resident((2 * BLOCK, BLOCK)),
            pl.BlockSpec(memory_space=pltpu.SMEM),
            hbm,
            hbm,
            pl.BlockSpec((up_rows, D_FF), slice_map),
            pl.BlockSpec((down_rows, D_MODEL), slice_map),
        ],
        out_specs=[
            pl.BlockSpec((1, tile, D_MODEL), x_map),
            pl.BlockSpec((up_rows, D_FF), slice_map),
            pl.BlockSpec((down_rows, D_MODEL), slice_map),
        ],
        out_shape=[
            jax.ShapeDtypeStruct(x.shape, F32),
            jax.ShapeDtypeStruct(w_up.shape, BF16),
            jax.ShapeDtypeStruct(w_down.shape, BF16),
        ],
        scratch_shapes=[
            pltpu.VMEM((D_MODEL, IN_COLS), BF16),
            pltpu.VMEM((D_MODEL, D_MODEL), BF16),
            pltpu.VMEM((D_MODEL, D_MODEL), BF16),
            pltpu.VMEM((D_MODEL, D_MODEL), BF16),
            pltpu.VMEM((STAGE_SLOTS, D_MODEL, STAGE_COLS), F32),
            pltpu.SemaphoreType.DMA((STAGE_SLOTS,)),
            pltpu.VMEM((2, N_HEADS, 2 * BLOCK, BLOCK), F32),
            pltpu.VMEM((D_MODEL // LANES, tile + 2 * SUBLANES, LANES), F32),
            pltpu.VMEM((tile + BLOCK, 4 * KV_COLS), BF16),
            pltpu.VMEM((D_MODEL, tile), BF16),
        ],
        compiler_params=pltpu.CompilerParams(
            dimension_semantics=("arbitrary",), vmem_limit_bytes=VMEM_LIMIT),
        name="token_mixer",
    )(x, g, w_in, conv_w, w_conv_out, sinks, jnp.asarray(_bucket_table()), rel_bias, w_attn_out, w_o, w_up, w_down)


def _mlp(x, g, w_up, w_down, g_final):
    rows = x.shape[0]
    tile = sum(SUBTILES_MLP)
    assert rows % tile == 0 and all(s % BF16_SUBLANES == 0 for s in SUBTILES_MLP)
    return pl.pallas_call(
        functools.partial(_mlp_kernel, subtiles=SUBTILES_MLP),
        grid=(rows // tile,),
        in_specs=[
            pl.BlockSpec((tile, D_MODEL), lambda i: (i, 0)),
            _resident((1, D_MODEL)),
            _resident((D_MODEL, D_FF)),
            _resident((D_FF, D_MODEL)),
            _resident((1, D_MODEL)),
        ],
        out_specs=pl.BlockSpec((tile, D_MODEL), lambda i: (i, 0)),
        out_shape=jax.ShapeDtypeStruct(x.shape, F32),
        compiler_params=pltpu.CompilerParams(
            dimension_semantics=("arbitrary",), vmem_limit_bytes=VMEM_LIMIT),
        name="channel_mixer",
    )(x, g, w_up, w_down, g_final)


def kernel(x, attn_norm_g, w_in, conv_w, w_conv_out, attn_sinks, rel_bias, w_attn_out, w_o, mlp_norm_g, w_up,
           w_down, final_norm_g):
    batch, seq, _ = x.shape
    depth = w_in.shape[0]
    assert depth == 1, "the final norm is fused into the (only) layer's channel mixer"
    x, w_up_bf16, w_down_bf16 = _mixer(
        x, attn_norm_g.reshape(1, D_MODEL), w_in.reshape(D_MODEL, IN_COLS), conv_w.reshape(CONV_WIDTH, D_MODEL),
        w_conv_out.reshape(D_MODEL, D_MODEL), attn_sinks.reshape(N_HEADS).astype(F32), rel_bias.astype(F32),
        w_attn_out.reshape(D_MODEL, D_MODEL), w_o.reshape(D_MODEL, D_MODEL), w_up.reshape(D_MODEL, D_FF),
        w_down.reshape(D_FF, D_MODEL))
    x = _mlp(x.reshape(batch * seq, D_MODEL), mlp_norm_g.reshape(1, D_MODEL), w_up_bf16, w_down_bf16,
             final_norm_g.reshape(1, D_MODEL))
    return x.reshape(batch, seq, D_MODEL)
```

```python
import functools
import math

import jax
import jax.numpy as jnp
import numpy as np
from jax import lax
from jax.experimental import pallas as pl
from jax.experimental.pallas import tpu as pltpu

D_MODEL = 1024
HEAD_DIM = 64
N_HEADS = 16
N_KV_HEADS = 2
GROUP = N_HEADS // N_KV_HEADS
BLOCK = 128
N_BUCKETS = 32
MAX_DISTANCE = 128
CONV_WIDTH = 3
D_FF = 4 * D_MODEL
EPS = 1e-6
NEG_INF = -1e30
LOG2E = math.log2(math.e)

LANES = 128
SUBLANES = 8
BF16_SUBLANES = 16
KV_COLS = N_KV_HEADS * HEAD_DIM
PAIRS_PER_GROUP = GROUP // 2

OFF_B, OFF_C, OFF_X, OFF_Q = 0, D_MODEL, 2 * D_MODEL, 3 * D_MODEL
OFF_K = 4 * D_MODEL
OFF_V = OFF_K + KV_COLS
OFF_GC = OFF_V + KV_COLS
OFF_GA = OFF_GC + D_MODEL
IN_COLS = OFF_GA + D_MODEL

TILE_MIX = 512
SUBTILES_MLP = (512, 512)
VMEM_LIMIT = 56 * 1024 * 1024
STAGE_COLS = 640
STAGE_SLOTS = 3

BF16 = jnp.bfloat16
F32 = jnp.float32

NT_DIMS = (((1,), (1,)), ((), ()))
TN_DIMS = (((0,), (0,)), ((), ()))


def _bucket_table():
    kj = np.arange(2 * BLOCK)[:, None]
    qi = np.arange(BLOCK)[None, :]
    dist = qi + BLOCK - kj
    n = np.maximum(dist, 0)
    max_exact = N_BUCKETS // 2
    ratio = np.log(np.maximum(n, max_exact).astype(np.float32) / max_exact) / np.log(MAX_DISTANCE / max_exact)
    large = np.minimum(max_exact + (ratio * (N_BUCKETS - max_exact)).astype(np.int32), N_BUCKETS - 1)
    bucket = np.where(n < max_exact, n, large).astype(np.int32)
    in_window = (dist >= 0) & (dist < BLOCK)
    return np.where(in_window, bucket, -1).astype(np.int32)


def _bias_table(head, bucket_ref, rel_ref, bias_buf):
    bucket = bucket_ref[...]
    acc = jnp.full(bucket.shape, NEG_INF, F32)
    for b in range(N_BUCKETS):
        acc = jnp.where(bucket == b, rel_ref[b, head], acc)
    acc = acc * LOG2E
    bias_buf[0, head] = acc
    key = lax.broadcasted_iota(jnp.int32, bucket.shape, 0)
    bias_buf[1, head] = jnp.where(key >= BLOCK, acc, NEG_INF)


def _rmsnorm(x, g):
    y = x * lax.rsqrt(jnp.mean(x * x, axis=-1, keepdims=True) + EPS)
    return y * g


def _stage_weights(pairs, stage, sems, fillers):
    chunks = []
    for src, dst in pairs:
        cols = src.shape[1]
        width = max(w for w in range(LANES, STAGE_COLS + 1, LANES) if cols % w == 0)
        chunks += [(src, dst, c, width) for c in range(0, cols, width)]
    slots = stage.shape[0]

    def copy(i):
        src, _, c, width = chunks[i]
        return pltpu.make_async_copy(src.at[:, c:c + width], stage.at[i % slots, :, 0:width], sems.at[i % slots])

    for i in range(min(slots, len(chunks))):
        copy(i).start()
    fillers = list(fillers)
    per_chunk = -(-len(fillers) // len(chunks))
    for i, (_, dst, c, width) in enumerate(chunks):
        for _ in range(min(per_chunk, len(fillers))):
            fillers.pop(0)()
        copy(i).wait()
        dst[:, c:c + width] = stage[i % slots, :, 0:width].astype(BF16)
        if i + slots < len(chunks):
            copy(i + slots).start()


def _mixer_kernel(x_ref, g_ref, win_hbm, convw_ref, wco_hbm, sinks_ref, bucket_ref, rel_ref, wao_hbm,
                  wo_ref, wup_ref, wdown_ref, out_ref, wo_out_ref, wup_out_ref, wdown_out_ref,
                  win_buf, wco_buf, wao_buf, stage, stage_sems, bias_buf, cx_buf, kv_buf, attn_buf,
                  *, tile, tiles):
    s = pl.program_id(0)

    @pl.when(s == 0)
    def _():
        bias_tasks = [functools.partial(_bias_table, head, bucket_ref, rel_ref, bias_buf)
                      for head in range(N_HEADS)]
        _stage_weights([(win_hbm, win_buf), (wco_hbm, wco_buf), (wao_hbm, wao_buf)],
                       stage, stage_sems, bias_tasks)

    @pl.when(s > 0)
    def _():
        _mixer_step((s - 1) % tiles, x_ref, g_ref, win_buf, convw_ref, wco_buf, sinks_ref, bias_buf, wao_buf,
                    wo_ref, wup_ref, wdown_ref, out_ref, wo_out_ref, wup_out_ref, wdown_out_ref, cx_buf, kv_buf,
                    attn_buf, tile=tile)


def _mixer_step(t, x_ref, g_ref, win_ref, convw_ref, wco_ref, sinks_ref, bias_ref, wao_ref, wo_ref,
                wup_ref, wdown_ref, out_ref, wo_out_ref, wup_out_ref, wdown_out_ref, cx_buf, kv_buf, attn_buf,
                *, tile):
    @pl.when(t == 0)
    def _():
        cx_buf[:, 0:SUBLANES, :] = jnp.zeros((D_MODEL // LANES, SUBLANES, LANES), F32)
        kv_buf[0:BLOCK, :] = jnp.zeros((BLOCK, 4 * KV_COLS), BF16)

    x = x_ref[0]
    half = tile // 2
    h_halves = [_rmsnorm(x[r * half:(r + 1) * half], g_ref[...]).astype(BF16) for r in range(2)]
    h = jnp.concatenate(h_halves, axis=0)

    def proj(lo, width):
        return jnp.dot(h, win_ref[:, lo:lo + width], preferred_element_type=F32)

    def proj_halves(lo, width):
        w = win_ref[:, lo:lo + width]
        return jnp.concatenate([jnp.dot(hr, w, preferred_element_type=F32) for hr in h_halves], axis=0)

    q = (proj_halves(OFF_Q, D_MODEL) * (HEAD_DIM ** -0.5 * LOG2E)).astype(BF16)
    kv = proj_halves(OFF_K, 2 * KV_COLS)
    k = kv[:, 0:KV_COLS]
    val = kv[:, KV_COLS:2 * KV_COLS]
    k_swapped = pltpu.roll(k, HEAD_DIM, axis=1)
    low_half_tile = lax.broadcasted_iota(jnp.int32, (tile, LANES), 1) < HEAD_DIM
    new_rows = slice(BLOCK, BLOCK + tile)
    kv_buf[new_rows, 0:KV_COLS] = jnp.where(low_half_tile, k, k_swapped).astype(BF16)
    kv_buf[new_rows, KV_COLS:2 * KV_COLS] = jnp.where(low_half_tile, k_swapped, k).astype(BF16)
    kv_buf[new_rows, 2 * KV_COLS:3 * KV_COLS] = jnp.where(low_half_tile, val, 1.0).astype(BF16)
    kv_buf[new_rows, 3 * KV_COLS:4 * KV_COLS] = jnp.where(low_half_tile, 1.0, val).astype(BF16)

    low_half = lax.broadcasted_iota(jnp.int32, (BLOCK, LANES), 1) < HEAD_DIM
    zero = jnp.zeros((BLOCK, LANES), BF16)

    scores, probs, sink_terms = {}, {}, {}

    def qk(j):
        rows = slice(j * BLOCK, (j + 2) * BLOCK)
        qj = q[j * BLOCK:(j + 1) * BLOCK, :]
        for g in range(N_KV_HEADS):
            parts = []
            for p in range(g * PAIRS_PER_GROUP, (g + 1) * PAIRS_PER_GROUP):
                qp = qj[:, p * LANES:(p + 1) * LANES]
                parts.append(jnp.where(low_half, qp, zero))
                parts.append(jnp.where(low_half, zero, qp))
            q_heads = jnp.concatenate(parts, axis=0)
            keys = kv_buf[rows, g * KV_COLS:(g + 1) * KV_COLS]
            scores[j, g] = lax.dot_general(keys, q_heads, NT_DIMS, preferred_element_type=F32)

    def softmax(j, g):
        first = (t == 0).astype(jnp.int32) if j == 0 else 0
        s = scores.pop((j, g))
        for i in range(GROUP):
            head = g * GROUP + i
            sh = s[:, i * BLOCK:(i + 1) * BLOCK] + bias_ref[first, head]
            sink = jnp.full((1, BLOCK), sinks_ref[head], F32) * LOG2E
            m = jnp.maximum(jnp.max(sh, axis=0, keepdims=True), sink)
            probs[j, head] = jnp.exp2(sh - m).astype(BF16)
            sink_terms[j, head] = jnp.exp2(sink - m)

    def pv(j):
        rows = slice(j * BLOCK, (j + 2) * BLOCK)
        for g in range(N_KV_HEADS):
            vals = kv_buf[rows, (2 + g) * KV_COLS:(3 + g) * KV_COLS]
            p = jnp.concatenate([probs.pop((j, g * GROUP + i)) for i in range(GROUP)], axis=1)
            o = lax.dot_general(vals, p, TN_DIMS, preferred_element_type=F32)
            out_rows = slice(g * HEAD_DIM, (g + 1) * HEAD_DIM)
            sum_row = (1 - g) * HEAD_DIM
            for i in range(GROUP):
                head = g * GROUP + i
                cols = slice(i * BLOCK, (i + 1) * BLOCK)
                denom = o[sum_row:sum_row + 1, cols] + sink_terms.pop((j, head))
                oh = o[out_rows, cols] * (1.0 / denom)
                attn_buf[head * HEAD_DIM:(head + 1) * HEAD_DIM, j * BLOCK:(j + 1) * BLOCK] = oh.astype(BF16)

    v = {}

    def t_c():
        v["c"] = proj(OFF_C, D_MODEL)

    def t_x():
        cx = v.pop("c") * proj(OFF_X, D_MODEL)
        for c in range(D_MODEL // LANES):
            cx_buf[c, SUBLANES:SUBLANES + tile, :] = cx[:, c * LANES:(c + 1) * LANES]

    def t_conv():
        slabs = []
        for c in range(D_MODEL // LANES):
            cols = slice(c * LANES, (c + 1) * LANES)
            slabs.append(convw_ref[0:1, cols] * cx_buf[c, SUBLANES - 2:SUBLANES - 2 + tile, :]
                         + convw_ref[1:2, cols] * cx_buf[c, SUBLANES - 1:SUBLANES - 1 + tile, :]
                         + convw_ref[2:3, cols] * cx_buf[c, SUBLANES:SUBLANES + tile, :])
        v["conv"] = jnp.concatenate(slabs, axis=1)
        cx_buf[:, 0:SUBLANES, :] = cx_buf[:, tile:tile + SUBLANES, :]

    def t_b():
        v["u"] = (proj(OFF_B, D_MODEL) * v.pop("conv")).astype(BF16)

    def t_gc():
        v["gate_conv"] = jax.nn.sigmoid(proj(OFF_GC, D_MODEL))

    def t_ga():
        v["gate_attn"] = jax.nn.sigmoid(proj(OFF_GA, D_MODEL))

    def t_yconv():
        v["mixed"] = v.pop("gate_conv") * jnp.dot(v.pop("u"), wco_ref[...], preferred_element_type=F32)

    def t_yattn(r):
        rows = slice(r * half, (r + 1) * half)
        y_attn = lax.dot_general(attn_buf[:, rows], wao_ref[...], TN_DIMS, preferred_element_type=F32)
        out_ref[0, rows, :] = (v["mixed"][rows] + v["gate_attn"][rows] * y_attn).astype(BF16)

    qk(0)
    t_c()
    wo_out_ref[...] = wo_ref[...].astype(BF16)
    wup_out_ref[...] = wup_ref[...].astype(BF16)
    wdown_out_ref[...] = wdown_ref[...].astype(BF16)
    softmax(0, 0)
    t_x()
    softmax(0, 1)
    t_gc()
    t_conv()
    pv(0)
    qk(1)
    t_b()
    softmax(1, 0)
    t_ga()
    softmax(1, 1)
    pv(1)
    qk(2)
    t_yconv()
    softmax(2, 0)
    softmax(2, 1)
    pv(2)
    qk(3)
    t_yattn(0)
    softmax(3, 0)
    softmax(3, 1)
    pv(3)
    kv_buf[0:BLOCK, :] = kv_buf[tile:tile + BLOCK, :]
    t_yattn(1)


def _mlp_kernel(x_ref, mix_ref, wo_ref, g_ref, wup_ref, wdown_ref, gf_ref, out_ref, *, subtiles):
    normed, act, mixed = {}, {}, {}
    starts = [sum(subtiles[:i]) for i in range(len(subtiles))]

    def rows(i):
        return slice(starts[i], starts[i] + subtiles[i])

    def proj(i):
        out_ref[rows(i), :] = x_ref[rows(i), :] + jnp.dot(mix_ref[rows(i), :], wo_ref[...],
                                                          preferred_element_type=F32)

    def norm(i):
        normed[i] = _rmsnorm(out_ref[rows(i), :], g_ref[...]).astype(BF16)

    def up(i):
        u = jnp.dot(normed.pop(i), wup_ref[...], preferred_element_type=F32)
        act[i] = jnp.square(jnp.maximum(u, 0.0)).astype(BF16)

    def down(i):
        mixed[i] = out_ref[rows(i), :] + jnp.dot(act.pop(i), wdown_ref[...], preferred_element_type=F32)

    def final(i):
        out_ref[rows(i), :] = _rmsnorm(mixed.pop(i), gf_ref[...])

    n = len(subtiles)
    proj(0)
    if n > 1:
        proj(1)
    norm(0)
    up(0)
    for i in range(n):
        if i + 2 < n:
            proj(i + 2)
        if i + 1 < n:
            norm(i + 1)
            up(i + 1)
        if i > 0:
            final(i - 1)
        down(i)
    final(n - 1)


def _resident(shape):
    return pl.BlockSpec(shape, lambda *_: (0,) * len(shape), pipeline_mode=pl.Buffered(1))


def _mixer(x, g, w_in, conv_w, w_conv_out, sinks, rel_bias, w_attn_out, w_o, w_up, w_down):
    batch, seq, _ = x.shape
    tile = TILE_MIX
    assert seq % tile == 0 and tile % BLOCK == 0
    tiles = seq // tile
    steps = batch * tiles
    up_rows, down_rows = D_MODEL // steps, D_FF // steps
    assert up_rows * steps == D_MODEL and up_rows % BF16_SUBLANES == 0
    assert down_rows * steps == D_FF and down_rows % BF16_SUBLANES == 0

    def tile_of(s):
        return jnp.maximum(s - 1, 0)

    def x_map(s):
        return (tile_of(s) // tiles, tile_of(s) % tiles, 0)

    def slice_map(s):
        return (tile_of(s), 0)

    hbm = pl.BlockSpec(memory_space=pl.ANY)
    return pl.pallas_call(
        functools.partial(_mixer_kernel, tile=tile, tiles=tiles),
        grid=(steps + 1,),
        in_specs=[
            pl.BlockSpec((1, tile, D_MODEL), x_map),
            _resident((1, D_MODEL)),
            hbm,
            _resident((CONV_WIDTH, D_MODEL)),
            hbm,
            pl.BlockSpec(memory_space=pltpu.SMEM),
            _resident((2 * BLOCK, BLOCK)),
            pl.BlockSpec(memory_space=pltpu.SMEM),
            hbm,
            pl.BlockSpec((up_rows, D_MODEL), slice_map),
            pl.BlockSpec((up_rows, D_FF), slice_map),
            pl.BlockSpec((down_rows, D_MODEL), slice_map),
        ],
        out_specs=[
            pl.BlockSpec((1, tile, D_MODEL), x_map),
            pl.BlockSpec((up_rows, D_MODEL), slice_map),
            pl.BlockSpec((up_rows, D_FF), slice_map),
            pl.BlockSpec((down_rows, D_MODEL), slice_map),
        ],
        out_shape=[
            jax.ShapeDtypeStruct(x.shape, BF16),
            jax.ShapeDtypeStruct(w_o.shape, BF16),
            jax.ShapeDtypeStruct(w_up.shape, BF16),
            jax.ShapeDtypeStruct(w_down.shape, BF16),
        ],
        scratch_shapes=[
            pltpu.VMEM((D_MODEL, IN_COLS), BF16),
            pltpu.VMEM((D_MODEL, D_MODEL), BF16),
            pltpu.VMEM((D_MODEL, D_MODEL), BF16),
            pltpu.VMEM((STAGE_SLOTS, D_MODEL, STAGE_COLS), F32),
            pltpu.SemaphoreType.DMA((STAGE_SLOTS,)),
            pltpu.VMEM((2, N_HEADS, 2 * BLOCK, BLOCK), F32),
            pltpu.VMEM((D_MODEL // LANES, tile + 2 * SUBLANES, LANES), F32),
            pltpu.VMEM((tile + BLOCK, 4 * KV_COLS), BF16),
            pltpu.VMEM((D_MODEL, tile), BF16),
        ],
        compiler_params=pltpu.CompilerParams(
            dimension_semantics=("arbitrary",), vmem_limit_bytes=VMEM_LIMIT),
        name="token_mixer",
    )(x, g, w_in, conv_w, w_conv_out, sinks, jnp.asarray(_bucket_table()), rel_bias, w_attn_out, w_o, w_up, w_down)


def _mlp(x, mixed, w_o, g, w_up, w_down, g_final):
    rows = x.shape[0]
    tile = sum(SUBTILES_MLP)
    assert rows % tile == 0 and all(s % BF16_SUBLANES == 0 for s in SUBTILES_MLP)
    return pl.pallas_call(
        functools.partial(_mlp_kernel, subtiles=SUBTILES_MLP),
        grid=(rows // tile,),
        in_specs=[
            pl.BlockSpec((tile, D_MODEL), lambda i: (i, 0)),
            pl.BlockSpec((tile, D_MODEL), lambda i: (i, 0)),
            _resident((D_MODEL, D_MODEL)),
            _resident((1, D_MODEL)),
            _resident((D_MODEL, D_FF)),
            _resident((D_FF, D_MODEL)),
            _resident((1, D_MODEL)),
        ],
        out_specs=pl.BlockSpec((tile, D_MODEL), lambda i: (i, 0)),
        out_shape=jax.ShapeDtypeStruct(x.shape, F32),
        compiler_params=pltpu.CompilerParams(
            dimension_semantics=("arbitrary",), vmem_limit_bytes=VMEM_LIMIT),
        name="channel_mixer",
    )(x, mixed, w_o, g, w_up, w_down, g_final)


def kernel(x, attn_norm_g, w_in, conv_w, w_conv_out, attn_sinks, rel_bias, w_attn_out, w_o, mlp_norm_g, w_up,
           w_down, final_norm_g):
    batch, seq, _ = x.shape
    depth = w_in.shape[0]
    assert depth == 1, "the final norm is fused into the (only) layer's channel mixer"
    mixed, w_o_bf16, w_up_bf16, w_down_bf16 = _mixer(
        x, attn_norm_g.reshape(1, D_MODEL), w_in.reshape(D_MODEL, IN_COLS), conv_w.reshape(CONV_WIDTH, D_MODEL),
        w_conv_out.reshape(D_MODEL, D_MODEL), attn_sinks.reshape(N_HEADS).astype(F32), rel_bias.astype(F32),
        w_attn_out.reshape(D_MODEL, D_MODEL), w_o.reshape(D_MODEL, D_MODEL), w_up.reshape(D_MODEL, D_FF),
        w_down.reshape(D_FF, D_MODEL))
    x = _mlp(x.reshape(batch * seq, D_MODEL), mixed.reshape(batch * seq, D_MODEL), w_o_bf16,
             mlp_norm_g.reshape(1, D_MODEL), w_up_bf16, w_down_bf16, final_norm_g.reshape(1, D_MODEL))
    return x.reshape(batch, seq, D_MODEL)
```

```python
import functools
import math

import jax
import jax.numpy as jnp
import numpy as np
from jax import lax
from jax.experimental import pallas as pl
from jax.experimental.pallas import tpu as pltpu

D_MODEL = 1024
HEAD_DIM = 64
N_HEADS = 16
N_KV_HEADS = 2
GROUP = N_HEADS // N_KV_HEADS
BLOCK = 128
N_BUCKETS = 32
MAX_DISTANCE = 128
CONV_WIDTH = 3
D_FF = 4 * D_MODEL
EPS = 1e-6
NEG_INF = -1e30
LOG2E = math.log2(math.e)

LANES = 128
SUBLANES = 8
BF16_SUBLANES = 16
KV_COLS = N_KV_HEADS * HEAD_DIM
PAIRS_PER_GROUP = GROUP // 2

OFF_B, OFF_C, OFF_X, OFF_Q = 0, D_MODEL, 2 * D_MODEL, 3 * D_MODEL
OFF_K = 4 * D_MODEL
OFF_V = OFF_K + KV_COLS
OFF_GC = OFF_V + KV_COLS
OFF_GA = OFF_GC + D_MODEL
IN_COLS = OFF_GA + D_MODEL

TILE_MIX = 512
SUBTILES_MLP = (512, 512)
VMEM_LIMIT = 56 * 1024 * 1024
STAGE_COLS = 640
STAGE_SLOTS = 3

BF16 = jnp.bfloat16
F32 = jnp.float32

NT_DIMS = (((1,), (1,)), ((), ()))
TN_DIMS = (((0,), (0,)), ((), ()))


def _bucket_table():
    kj = np.arange(2 * BLOCK)[:, None]
    qi = np.arange(BLOCK)[None, :]
    dist = qi + BLOCK - kj
    n = np.maximum(dist, 0)
    max_exact = N_BUCKETS // 2
    ratio = np.log(np.maximum(n, max_exact).astype(np.float32) / max_exact) / np.log(MAX_DISTANCE / max_exact)
    large = np.minimum(max_exact + (ratio * (N_BUCKETS - max_exact)).astype(np.int32), N_BUCKETS - 1)
    bucket = np.where(n < max_exact, n, large).astype(np.int32)
    in_window = (dist >= 0) & (dist < BLOCK)
    return np.where(in_window, bucket, -1).astype(np.int32)


def _bias_table(head, bucket_ref, rel_ref, bias_buf):
    bucket = bucket_ref[...]
    acc = jnp.full(bucket.shape, NEG_INF, F32)
    for b in range(N_BUCKETS):
        acc = jnp.where(bucket == b, rel_ref[b, head], acc)
    acc = acc * LOG2E
    bias_buf[0, head] = acc
    key = lax.broadcasted_iota(jnp.int32, bucket.shape, 0)
    bias_buf[1, head] = jnp.where(key >= BLOCK, acc, NEG_INF)


def _rmsnorm(x, g):
    y = x * lax.rsqrt(jnp.mean(x * x, axis=-1, keepdims=True) + EPS)
    return y * g


def _stage_weights(pairs, stage, sems, fillers):
    chunks = []
    for src, dst in pairs:
        cols = src.shape[1]
        width = max(w for w in range(LANES, STAGE_COLS + 1, LANES) if cols % w == 0)
        chunks += [(src, dst, c, width) for c in range(0, cols, width)]
    slots = stage.shape[0]

    def copy(i):
        src, _, c, width = chunks[i]
        return pltpu.make_async_copy(src.at[:, c:c + width], stage.at[i % slots, :, 0:width], sems.at[i % slots])

    for i in range(min(slots, len(chunks))):
        copy(i).start()
    fillers = list(fillers)
    assert len(fillers) <= len(chunks)
    for i, (_, dst, c, width) in enumerate(chunks):
        if fillers:
            fillers.pop(0)()
        copy(i).wait()
        dst[:, c:c + width] = stage[i % slots, :, 0:width].astype(BF16)
        if i + slots < len(chunks):
            copy(i + slots).start()


def _mixer_kernel(x_ref, g_ref, win_hbm, convw_ref, wco_hbm, sinks_ref, bucket_ref, rel_ref, wao_hbm, wo_hbm,
                  wup_ref, wdown_ref, out_ref, wup_out_ref, wdown_out_ref,
                  win_buf, wco_buf, wao_buf, wo_buf, stage, stage_sems, bias_buf, cx_buf, kv_buf, attn_buf,
                  *, tile, tiles):
    s = pl.program_id(0)

    @pl.when(s == 0)
    def _():
        bias_tasks = [functools.partial(_bias_table, head, bucket_ref, rel_ref, bias_buf)
                      for head in range(N_HEADS)]
        _stage_weights([(win_hbm, win_buf), (wco_hbm, wco_buf), (wao_hbm, wao_buf), (wo_hbm, wo_buf)],
                       stage, stage_sems, bias_tasks)

    @pl.when(s > 0)
    def _():
        _mixer_step((s - 1) % tiles, x_ref, g_ref, win_buf, convw_ref, wco_buf, sinks_ref, bias_buf, wao_buf,
                    wo_buf, wup_ref, wdown_ref, out_ref, wup_out_ref, wdown_out_ref, cx_buf, kv_buf, attn_buf,
                    tile=tile)


def _mixer_step(t, x_ref, g_ref, win_ref, convw_ref, wco_ref, sinks_ref, bias_ref, wao_ref, wo_ref,
                wup_ref, wdown_ref, out_ref, wup_out_ref, wdown_out_ref, cx_buf, kv_buf, attn_buf, *, tile):
    @pl.when(t == 0)
    def _():
        cx_buf[:, 0:SUBLANES, :] = jnp.zeros((D_MODEL // LANES, SUBLANES, LANES), F32)
        kv_buf[0:BLOCK, :] = jnp.zeros((BLOCK, 4 * KV_COLS), BF16)

    x = x_ref[0]
    half = tile // 2
    h_halves = [_rmsnorm(x[r * half:(r + 1) * half], g_ref[...]).astype(BF16) for r in range(2)]
    h = jnp.concatenate(h_halves, axis=0)

    def proj(lo, width):
        return jnp.dot(h, win_ref[:, lo:lo + width], preferred_element_type=F32)

    w_q = win_ref[:, OFF_Q:OFF_Q + D_MODEL]
    w_kv = win_ref[:, OFF_K:OFF_K + 2 * KV_COLS]
    q_0 = jnp.dot(h_halves[0], w_q, preferred_element_type=F32)
    kv = jnp.concatenate([jnp.dot(hr, w_kv, preferred_element_type=F32) for hr in h_halves], axis=0)
    k = kv[:, 0:KV_COLS]
    val = kv[:, KV_COLS:2 * KV_COLS]
    k_swapped = pltpu.roll(k, HEAD_DIM, axis=1)
    low_half_tile = lax.broadcasted_iota(jnp.int32, (tile, LANES), 1) < HEAD_DIM
    new_rows = slice(BLOCK, BLOCK + tile)
    kv_buf[new_rows, 0:KV_COLS] = jnp.where(low_half_tile, k, k_swapped).astype(BF16)
    kv_buf[new_rows, KV_COLS:2 * KV_COLS] = jnp.where(low_half_tile, k_swapped, k).astype(BF16)
    kv_buf[new_rows, 2 * KV_COLS:3 * KV_COLS] = jnp.where(low_half_tile, val, 1.0).astype(BF16)
    kv_buf[new_rows, 3 * KV_COLS:4 * KV_COLS] = jnp.where(low_half_tile, 1.0, val).astype(BF16)
    q_1 = jnp.dot(h_halves[1], w_q, preferred_element_type=F32)
    q = (jnp.concatenate([q_0, q_1], axis=0) * (HEAD_DIM ** -0.5 * LOG2E)).astype(BF16)

    low_half = lax.broadcasted_iota(jnp.int32, (BLOCK, LANES), 1) < HEAD_DIM
    zero = jnp.zeros((BLOCK, LANES), BF16)

    scores, probs, sink_terms = {}, {}, {}

    def qk(j):
        rows = slice(j * BLOCK, (j + 2) * BLOCK)
        qj = q[j * BLOCK:(j + 1) * BLOCK, :]
        for g in range(N_KV_HEADS):
            parts = []
            for p in range(g * PAIRS_PER_GROUP, (g + 1) * PAIRS_PER_GROUP):
                qp = qj[:, p * LANES:(p + 1) * LANES]
                parts.append(jnp.where(low_half, qp, zero))
                parts.append(jnp.where(low_half, zero, qp))
            q_heads = jnp.concatenate(parts, axis=0)
            keys = kv_buf[rows, g * KV_COLS:(g + 1) * KV_COLS]
            scores[j, g] = lax.dot_general(keys, q_heads, NT_DIMS, preferred_element_type=F32)

    def softmax(j, g):
        first = (t == 0).astype(jnp.int32) if j == 0 else 0
        s = scores.pop((j, g))
        for i in range(GROUP):
            head = g * GROUP + i
            sh = s[:, i * BLOCK:(i + 1) * BLOCK] + bias_ref[first, head]
            sink = jnp.full((1, BLOCK), sinks_ref[head], F32) * LOG2E
            m = jnp.maximum(jnp.max(sh, axis=0, keepdims=True), sink)
            probs[j, head] = jnp.exp2(sh - m).astype(BF16)
            sink_terms[j, head] = jnp.exp2(sink - m)

    def pv(j):
        rows = slice(j * BLOCK, (j + 2) * BLOCK)
        for g in range(N_KV_HEADS):
            vals = kv_buf[rows, (2 + g) * KV_COLS:(3 + g) * KV_COLS]
            p = jnp.concatenate([probs.pop((j, g * GROUP + i)) for i in range(GROUP)], axis=1)
            o = lax.dot_general(vals, p, TN_DIMS, preferred_element_type=F32)
            out_rows = slice(g * HEAD_DIM, (g + 1) * HEAD_DIM)
            sum_row = (1 - g) * HEAD_DIM
            for i in range(GROUP):
                head = g * GROUP + i
                cols = slice(i * BLOCK, (i + 1) * BLOCK)
                denom = o[sum_row:sum_row + 1, cols] + sink_terms.pop((j, head))
                oh = o[out_rows, cols] * (1.0 / denom)
                attn_buf[head * HEAD_DIM:(head + 1) * HEAD_DIM, j * BLOCK:(j + 1) * BLOCK] = oh.astype(BF16)

    v = {}

    def t_c():
        v["c"] = proj(OFF_C, D_MODEL)

    def t_x():
        cx = v.pop("c") * proj(OFF_X, D_MODEL)
        for c in range(D_MODEL // LANES):
            cx_buf[c, SUBLANES:SUBLANES + tile, :] = cx[:, c * LANES:(c + 1) * LANES]

    def t_conv():
        slabs = []
        for c in range(D_MODEL // LANES):
            cols = slice(c * LANES, (c + 1) * LANES)
            slabs.append(convw_ref[0:1, cols] * cx_buf[c, SUBLANES - 2:SUBLANES - 2 + tile, :]
                         + convw_ref[1:2, cols] * cx_buf[c, SUBLANES - 1:SUBLANES - 1 + tile, :]
                         + convw_ref[2:3, cols] * cx_buf[c, SUBLANES:SUBLANES + tile, :])
        v["conv"] = jnp.concatenate(slabs, axis=1)
        cx_buf[:, 0:SUBLANES, :] = cx_buf[:, tile:tile + SUBLANES, :]

    def t_b():
        v["u"] = (proj(OFF_B, D_MODEL) * v.pop("conv")).astype(BF16)

    def t_gc():
        v["gate_conv"] = jax.nn.sigmoid(proj(OFF_GC, D_MODEL))

    def t_ga():
        v["gate_attn"] = jax.nn.sigmoid(proj(OFF_GA, D_MODEL))

    def t_yconv():
        v["mixed"] = v.pop("gate_conv") * jnp.dot(v.pop("u"), wco_ref[...], preferred_element_type=F32)

    def t_yattn(r):
        rows = slice(r * half, (r + 1) * half)
        y_attn = lax.dot_general(attn_buf[:, rows], wao_ref[...], TN_DIMS, preferred_element_type=F32)
        v["mixed", r] = (v["mixed"][rows] + v["gate_attn"][rows] * y_attn).astype(BF16)

    def t_out(r):
        rows = slice(r * half, (r + 1) * half)
        out_ref[0, rows, :] = x[rows] + jnp.dot(v.pop(("mixed", r)), wo_ref[...], preferred_element_type=F32)

    qk(0)
    t_c()
    wup_out_ref[...] = wup_ref[...].astype(BF16)
    wdown_out_ref[...] = wdown_ref[...].astype(BF16)
    softmax(0, 0)
    t_x()
    softmax(0, 1)
    t_gc()
    t_conv()
    pv(0)
    qk(1)
    t_b()
    softmax(1, 0)
    t_ga()
    softmax(1, 1)
    pv(1)
    qk(2)
    t_yconv()
    softmax(2, 0)
    t_yattn(0)
    softmax(2, 1)
    pv(2)
    qk(3)
    t_out(0)
    softmax(3, 0)
    softmax(3, 1)
    pv(3)
    kv_buf[0:BLOCK, :] = kv_buf[tile:tile + BLOCK, :]
    t_yattn(1)
    t_out(1)


def _mlp_kernel(x_ref, g_ref, wup_ref, wdown_ref, gf_ref, out_ref, *, subtiles):
    normed, act, mixed = {}, {}, {}
    starts = [sum(subtiles[:i]) for i in range(len(subtiles))]

    def rows(i):
        return slice(starts[i], starts[i] + subtiles[i])

    def norm(i):
        normed[i] = _rmsnorm(x_ref[rows(i), :], g_ref[...]).astype(BF16)

    def up(i):
        u = jnp.dot(normed.pop(i), wup_ref[...], preferred_element_type=F32)
        act[i] = jnp.square(jnp.maximum(u, 0.0)).astype(BF16)

    def down(i):
        mixed[i] = x_ref[rows(i), :] + jnp.dot(act.pop(i), wdown_ref[...], preferred_element_type=F32)

    def final(i):
        out_ref[rows(i), :] = _rmsnorm(mixed.pop(i), gf_ref[...])

    n = len(subtiles)
    norm(0)
    up(0)
    for i in range(n):
        if i + 1 < n:
            norm(i + 1)
            up(i + 1)
        if i > 0:
            final(i - 1)
        down(i)
    final(n - 1)


def _resident(shape):
    return pl.BlockSpec(shape, lambda *_: (0,) * len(shape), pipeline_mode=pl.Buffered(1))


def _mixer(x, g, w_in, conv_w, w_conv_out, sinks, rel_bias, w_attn_out, w_o, w_up, w_down):
    batch, seq, _ = x.shape
    tile = TILE_MIX
    assert seq % tile == 0 and tile % BLOCK == 0
    tiles = seq // tile
    steps = batch * tiles
    up_rows, down_rows = D_MODEL // steps, D_FF // steps
    assert up_rows * steps == D_MODEL and up_rows % BF16_SUBLANES == 0
    assert down_rows * steps == D_FF and down_rows % BF16_SUBLANES == 0

    def tile_of(s):
        return jnp.maximum(s - 1, 0)

    def x_map(s):
        return (tile_of(s) // tiles, tile_of(s) % tiles, 0)

    def slice_map(s):
        return (tile_of(s), 0)

    hbm = pl.BlockSpec(memory_space=pl.ANY)
    return pl.pallas_call(
        functools.partial(_mixer_kernel, tile=tile, tiles=tiles),
        grid=(steps + 1,),
        in_specs=[
            pl.BlockSpec((1, tile, D_MODEL), x_map),
            _resident((1, D_MODEL)),
            hbm,
            _resident((CONV_WIDTH, D_MODEL)),
            hbm,
            pl.BlockSpec(memory_space=pltpu.SMEM),
            _resident((2 * BLOCK, BLOCK)),
            pl.BlockSpec(memory_space=pltpu.SMEM),
            hbm,
            hbm,
            pl.BlockSpec((up_rows, D_FF), slice_map),
            pl.BlockSpec((down_rows, D_MODEL), slice_map),
        ],
        out_specs=[
            pl.BlockSpec((1, tile, D_MODEL), x_map),
            pl.BlockSpec((up_rows, D_FF), slice_map),
            pl.BlockSpec((down_rows, D_MODEL), slice_map),
        ],
        out_shape=[
            jax.ShapeDtypeStruct(x.shape, F32),
            jax.ShapeDtypeStruct(w_up.shape, BF16),
            jax.ShapeDtypeStruct(w_down.shape, BF16),
        ],
        scratch_shapes=[
            pltpu.VMEM((D_MODEL, IN_COLS), BF16),
            pltpu.VMEM((D_MODEL, D_MODEL), BF16),
            pltpu.VMEM((D_MODEL, D_MODEL), BF16),
            pltpu.VMEM((D_MODEL, D_MODEL), BF16),
            pltpu.VMEM((STAGE_SLOTS, D_MODEL, STAGE_COLS), F32),
            pltpu.SemaphoreType.DMA((STAGE_SLOTS,)),
            pltpu.VMEM((2, N_HEADS, 2 * BLOCK, BLOCK), F32),
            pltpu.VMEM((D_MODEL // LANES, tile + 2 * SUBLANES, LANES), F32),
            pltpu.VMEM((tile + BLOCK, 4 * KV_COLS), BF16),
            pltpu.VMEM((D_MODEL, tile), BF16),
        ],
        compiler_params=pltpu.CompilerParams(
            dimension_semantics=("arbitrary",), vmem_limit_bytes=VMEM_LIMIT),
        name="token_mixer",
    )(x, g, w_in, conv_w, w_conv_out, sinks, jnp.asarray(_bucket_table()), rel_bias, w_attn_out, w_o, w_up, w_down)


def _mlp(x, g, w_up, w_down, g_final):
    rows = x.shape[0]
    tile = sum(SUBTILES_MLP)
    assert rows % tile == 0 and all(s % BF16_SUBLANES == 0 for s in SUBTILES_MLP)
    return pl.pallas_call(
        functools.partial(_mlp_kernel, subtiles=SUBTILES_MLP),
        grid=(rows // tile,),
        in_specs=[
            pl.BlockSpec((tile, D_MODEL), lambda i: (i, 0)),
            _resident((1, D_MODEL)),
            _resident((D_MODEL, D_FF)),
            _resident((D_FF, D_MODEL)),
            _resident((1, D_MODEL)),
        ],
        out_specs=pl.BlockSpec((tile, D_MODEL), lambda i: (i, 0)),
        out_shape=jax.ShapeDtypeStruct(x.shape, F32),
        compiler_params=pltpu.CompilerParams(
            dimension_semantics=("arbitrary",), vmem_limit_bytes=VMEM_LIMIT),
        name="channel_mixer",
    )(x, g, w_up, w_down, g_final)


def kernel(x, attn_norm_g, w_in, conv_w, w_conv_out, attn_sinks, rel_bias, w_attn_out, w_o, mlp_norm_g, w_up,
           w_down, final_norm_g):
    batch, seq, _ = x.shape
    depth = w_in.shape[0]
    assert depth == 1, "the final norm is fused into the (only) layer's channel mixer"
    x, w_up_bf16, w_down_bf16 = _mixer(
        x, attn_norm_g.reshape(1, D_MODEL), w_in.reshape(D_MODEL, IN_COLS), conv_w.reshape(CONV_WIDTH, D_MODEL),
        w_conv_out.reshape(D_MODEL, D_MODEL), attn_sinks.reshape(N_HEADS).astype(F32), rel_bias.astype(F32),
        w_attn_out.reshape(D_MODEL, D_MODEL), w_o.reshape(D_MODEL, D_MODEL), w_up.reshape(D_MODEL, D_FF),
        w_down.reshape(D_FF, D_MODEL))
    x = _mlp(x.reshape(batch * seq, D_MODEL), mlp_norm_g.reshape(1, D_MODEL), w_up_bf16, w_down_bf16,
             final_norm_g.reshape(1, D_MODEL))
    return x.reshape(batch, seq, D_MODEL)
```

```python
import functools
import math

import jax
import jax.numpy as jnp
import numpy as np
from jax import lax
from jax.experimental import pallas as pl
from jax.experimental.pallas import tpu as pltpu

D_MODEL = 1024
HEAD_DIM = 64
N_HEADS = 16
N_KV_HEADS = 2
GROUP = N_HEADS // N_KV_HEADS
BLOCK = 128
N_BUCKETS = 32
MAX_DISTANCE = 128
CONV_WIDTH = 3
D_FF = 4 * D_MODEL
EPS = 1e-6
NEG_INF = -1e30
LOG2E = math.log2(math.e)

LANES = 128
SUBLANES = 8
BF16_SUBLANES = 16
KV_COLS = N_KV_HEADS * HEAD_DIM
PAIRS_PER_GROUP = GROUP // 2

OFF_B, OFF_C, OFF_X, OFF_Q = 0, D_MODEL, 2 * D_MODEL, 3 * D_MODEL
OFF_K = 4 * D_MODEL
OFF_V = OFF_K + KV_COLS
OFF_GC = OFF_V + KV_COLS
OFF_GA = OFF_GC + D_MODEL
IN_COLS = OFF_GA + D_MODEL

TILE_MIX = 512
SUBTILES_MLP = (512, 512)
VMEM_LIMIT = 56 * 1024 * 1024
STAGE_COLS = 640
STAGE_SLOTS = 3

BF16 = jnp.bfloat16
F32 = jnp.float32

NT_DIMS = (((1,), (1,)), ((), ()))
TN_DIMS = (((0,), (0,)), ((), ()))


def _bucket_table():
    kj = np.arange(2 * BLOCK)[:, None]
    qi = np.arange(BLOCK)[None, :]
    dist = qi + BLOCK - kj
    n = np.maximum(dist, 0)
    max_exact = N_BUCKETS // 2
    ratio = np.log(np.maximum(n, max_exact).astype(np.float32) / max_exact) / np.log(MAX_DISTANCE / max_exact)
    large = np.minimum(max_exact + (ratio * (N_BUCKETS - max_exact)).astype(np.int32), N_BUCKETS - 1)
    bucket = np.where(n < max_exact, n, large).astype(np.int32)
    in_window = (dist >= 0) & (dist < BLOCK)
    return np.where(in_window, bucket, -1).astype(np.int32)


def _bias_table(head, bucket_ref, rel_ref, bias_buf):
    bucket = bucket_ref[...]
    acc = jnp.full(bucket.shape, NEG_INF, F32)
    for b in range(N_BUCKETS):
        acc = jnp.where(bucket == b, rel_ref[b, head], acc)
    acc = acc * LOG2E
    bias_buf[0, head] = acc
    key = lax.broadcasted_iota(jnp.int32, bucket.shape, 0)
    bias_buf[1, head] = jnp.where(key >= BLOCK, acc, NEG_INF)


def _rmsnorm(x, g):
    y = x * lax.rsqrt(jnp.mean(x * x, axis=-1, keepdims=True) + EPS)
    return y * g


def _stage_weights(pairs, stage, sems, fillers):
    chunks = []
    for src, dst in pairs:
        cols = src.shape[1]
        width = max(w for w in range(LANES, STAGE_COLS + 1, LANES) if cols % w == 0)
        chunks += [(src, dst, c, width) for c in range(0, cols, width)]
    slots = stage.shape[0]

    def copy(i):
        src, _, c, width = chunks[i]
        return pltpu.make_async_copy(src.at[:, c:c + width], stage.at[i % slots, :, 0:width], sems.at[i % slots])

    for i in range(min(slots, len(chunks))):
        copy(i).start()
    fillers = list(fillers)
    assert len(fillers) <= len(chunks)
    for i, (_, dst, c, width) in enumerate(chunks):
        if fillers:
            fillers.pop(0)()
        copy(i).wait()
        dst[:, c:c + width] = stage[i % slots, :, 0:width].astype(BF16)
        if i + slots < len(chunks):
            copy(i + slots).start()


def _mixer_kernel(x_ref, g_ref, win_hbm, convw_ref, wco_hbm, sinks_ref, bucket_ref, rel_ref, wao_hbm, wo_hbm,
                  wup_ref, wdown_ref, out_ref, wup_out_ref, wdown_out_ref,
                  win_buf, wco_buf, wao_buf, wo_buf, stage, stage_sems, bias_buf, cx_buf, kv_buf, attn_buf,
                  *, tile, tiles):
    s = pl.program_id(0)

    @pl.when(s == 0)
    def _():
        bias_tasks = [functools.partial(_bias_table, head, bucket_ref, rel_ref, bias_buf)
                      for head in range(N_HEADS)]
        _stage_weights([(win_hbm, win_buf), (wco_hbm, wco_buf), (wao_hbm, wao_buf), (wo_hbm, wo_buf)],
                       stage, stage_sems, bias_tasks)

    @pl.when(s > 0)
    def _():
        _mixer_step((s - 1) % tiles, x_ref, g_ref, win_buf, convw_ref, wco_buf, sinks_ref, bias_buf, wao_buf,
                    wo_buf, wup_ref, wdown_ref, out_ref, wup_out_ref, wdown_out_ref, cx_buf, kv_buf, attn_buf,
                    tile=tile)


def _mixer_step(t, x_ref, g_ref, win_ref, convw_ref, wco_ref, sinks_ref, bias_ref, wao_ref, wo_ref,
                wup_ref, wdown_ref, out_ref, wup_out_ref, wdown_out_ref, cx_buf, kv_buf, attn_buf, *, tile):
    @pl.when(t == 0)
    def _():
        cx_buf[:, 0:SUBLANES, :] = jnp.zeros((D_MODEL // LANES, SUBLANES, LANES), F32)
        kv_buf[0:BLOCK, :] = jnp.zeros((BLOCK, 4 * KV_COLS), BF16)

    x = x_ref[0]
    half = tile // 2
    h_halves = [_rmsnorm(x[r * half:(r + 1) * half], g_ref[...]).astype(BF16) for r in range(2)]
    h = jnp.concatenate(h_halves, axis=0)

    def proj(lo, width):
        return jnp.dot(h, win_ref[:, lo:lo + width], preferred_element_type=F32)

    w_q = win_ref[:, OFF_Q:OFF_Q + D_MODEL]
    w_kv = win_ref[:, OFF_K:OFF_K + 2 * KV_COLS]
    q_0 = jnp.dot(h_halves[0], w_q, preferred_element_type=F32)
    kv = jnp.concatenate([jnp.dot(hr, w_kv, preferred_element_type=F32) for hr in h_halves], axis=0)
    k = kv[:, 0:KV_COLS]
    val = kv[:, KV_COLS:2 * KV_COLS]
    k_swapped = pltpu.roll(k, HEAD_DIM, axis=1)
    low_half_tile = lax.broadcasted_iota(jnp.int32, (tile, LANES), 1) < HEAD_DIM
    new_rows = slice(BLOCK, BLOCK + tile)
    kv_buf[new_rows, 0:KV_COLS] = jnp.where(low_half_tile, k, k_swapped).astype(BF16)
    kv_buf[new_rows, KV_COLS:2 * KV_COLS] = jnp.where(low_half_tile, k_swapped, k).astype(BF16)
    kv_buf[new_rows, 2 * KV_COLS:3 * KV_COLS] = jnp.where(low_half_tile, val, 1.0).astype(BF16)
    kv_buf[new_rows, 3 * KV_COLS:4 * KV_COLS] = jnp.where(low_half_tile, 1.0, val).astype(BF16)
    q_1 = jnp.dot(h_halves[1], w_q, preferred_element_type=F32)
    q = (jnp.concatenate([q_0, q_1], axis=0) * (HEAD_DIM ** -0.5 * LOG2E)).astype(BF16)

    low_half = lax.broadcasted_iota(jnp.int32, (BLOCK, LANES), 1) < HEAD_DIM
    zero = jnp.zeros((BLOCK, LANES), BF16)

    scores, probs, sink_terms = {}, {}, {}

    def qk(j):
        rows = slice(j * BLOCK, (j + 2) * BLOCK)
        qj = q[j * BLOCK:(j + 1) * BLOCK, :]
        for g in range(N_KV_HEADS):
            parts = []
            for p in range(g * PAIRS_PER_GROUP, (g + 1) * PAIRS_PER_GROUP):
                qp = qj[:, p * LANES:(p + 1) * LANES]
                parts.append(jnp.where(low_half, qp, zero))
                parts.append(jnp.where(low_half, zero, qp))
            q_heads = jnp.concatenate(parts, axis=0)
            keys = kv_buf[rows, g * KV_COLS:(g + 1) * KV_COLS]
            scores[j, g] = lax.dot_general(keys, q_heads, NT_DIMS, preferred_element_type=F32)

    def softmax(j, g):
        first = (t == 0).astype(jnp.int32) if j == 0 else 0
        s = scores.pop((j, g))
        for i in range(GROUP):
            head = g * GROUP + i
            sh = s[:, i * BLOCK:(i + 1) * BLOCK] + bias_ref[first, head]
            sink = jnp.full((1, BLOCK), sinks_ref[head], F32) * LOG2E
            m = jnp.maximum(jnp.max(sh, axis=0, keepdims=True), sink)
            probs[j, head] = jnp.exp2(sh - m).astype(BF16)
            sink_terms[j, head] = jnp.exp2(sink - m)

    def pv(j):
        rows = slice(j * BLOCK, (j + 2) * BLOCK)
        for g in range(N_KV_HEADS):
            vals = kv_buf[rows, (2 + g) * KV_COLS:(3 + g) * KV_COLS]
            p = jnp.concatenate([probs.pop((j, g * GROUP + i)) for i in range(GROUP)], axis=1)
            o = lax.dot_general(vals, p, TN_DIMS, preferred_element_type=F32)
            out_rows = slice(g * HEAD_DIM, (g + 1) * HEAD_DIM)
            sum_row = (1 - g) * HEAD_DIM
            for i in range(GROUP):
                head = g * GROUP + i
                cols = slice(i * BLOCK, (i + 1) * BLOCK)
                denom = o[sum_row:sum_row + 1, cols] + sink_terms.pop((j, head))
                oh = o[out_rows, cols] * (1.0 / denom)
                attn_buf[head * HEAD_DIM:(head + 1) * HEAD_DIM, j * BLOCK:(j + 1) * BLOCK] = oh.astype(BF16)

    v = {}

    def t_c():
        v["c"] = proj(OFF_C, D_MODEL)

    def t_x():
        cx = v.pop("c") * proj(OFF_X, D_MODEL)
        for c in range(D_MODEL // LANES):
            cx_buf[c, SUBLANES:SUBLANES + tile, :] = cx[:, c * LANES:(c + 1) * LANES]

    def t_conv():
        slabs = []
        for c in range(D_MODEL // LANES):
            cols = slice(c * LANES, (c + 1) * LANES)
            slabs.append(convw_ref[0:1, cols] * cx_buf[c, SUBLANES - 2:SUBLANES - 2 + tile, :]
                         + convw_ref[1:2, cols] * cx_buf[c, SUBLANES - 1:SUBLANES - 1 + tile, :]
                         + convw_ref[2:3, cols] * cx_buf[c, SUBLANES:SUBLANES + tile, :])
        v["conv"] = jnp.concatenate(slabs, axis=1)
        cx_buf[:, 0:SUBLANES, :] = cx_buf[:, tile:tile + SUBLANES, :]

    def t_b():
        v["u"] = (proj(OFF_B, D_MODEL) * v.pop("conv")).astype(BF16)

    def t_gc():
        v["gate_conv"] = jax.nn.sigmoid(proj(OFF_GC, D_MODEL))

    def t_ga():
        v["gate_attn"] = jax.nn.sigmoid(proj(OFF_GA, D_MODEL))

    def t_yconv():
        v["mixed"] = v.pop("gate_conv") * jnp.dot(v.pop("u"), wco_ref[...], preferred_element_type=F32)

    def t_yattn(r):
        rows = slice(r * half, (r + 1) * half)
        y_attn = lax.dot_general(attn_buf[:, rows], wao_ref[...], TN_DIMS, preferred_element_type=F32)
        v["mixed", r] = (v["mixed"][rows] + v["gate_attn"][rows] * y_attn).astype(BF16)

    def t_out(r):
        rows = slice(r * half, (r + 1) * half)
        out_ref[0, rows, :] = x[rows] + jnp.dot(v.pop(("mixed", r)), wo_ref[...], preferred_element_type=F32)

    qk(0)
    t_c()
    wup_out_ref[...] = wup_ref[...].astype(BF16)
    wdown_out_ref[...] = wdown_ref[...].astype(BF16)
    softmax(0, 0)
    t_x()
    softmax(0, 1)
    t_gc()
    t_conv()
    pv(0)
    qk(1)
    t_b()
    softmax(1, 0)
    t_ga()
    softmax(1, 1)
    pv(1)
    t_yconv()
    qk(2)
    softmax(2, 0)
    t_yattn(0)
    softmax(2, 1)
    pv(2)
    qk(3)
    t_out(0)
    softmax(3, 0)
    softmax(3, 1)
    pv(3)
    kv_buf[0:BLOCK, :] = kv_buf[tile:tile + BLOCK, :]
    t_yattn(1)
    t_out(1)


def _mlp_kernel(x_ref, g_ref, wup_ref, wdown_ref, gf_ref, out_ref, *, subtiles):
    normed, act, mixed = {}, {}, {}
    starts = [sum(subtiles[:i]) for i in range(len(subtiles))]

    def rows(i):
        return slice(starts[i], starts[i] + subtiles[i])

    def norm(i):
        normed[i] = _rmsnorm(x_ref[rows(i), :], g_ref[...]).astype(BF16)

    def up(i):
        u = jnp.dot(normed.pop(i), wup_ref[...], preferred_element_type=F32)
        act[i] = jnp.square(jnp.maximum(u, 0.0)).astype(BF16)

    def down(i):
        mixed[i] = x_ref[rows(i), :] + jnp.dot(act.pop(i), wdown_ref[...], preferred_element_type=F32)

    def final(i):
        out_ref[rows(i), :] = _rmsnorm(mixed.pop(i), gf_ref[...])

    n = len(subtiles)
    norm(0)
    up(0)
    for i in range(n):
        if i + 1 < n:
            norm(i + 1)
            up(i + 1)
        if i > 0:
            final(i - 1)
        down(i)
    final(n - 1)


def _resident(shape):
    return pl.BlockSpec(shape, lambda *_: (0,) * len(shape), pipeline_mode=pl.Buffered(1))


def _mixer(x, g, w_in, conv_w, w_conv_out, sinks, rel_bias, w_attn_out, w_o, w_up, w_down):
    batch, seq, _ = x.shape
    tile = TILE_MIX
    assert seq % tile == 0 and tile % BLOCK == 0
    tiles = seq // tile
    steps = batch * tiles
    up_rows, down_rows = D_MODEL // steps, D_FF // steps
    assert up_rows * steps == D_MODEL and up_rows % BF16_SUBLANES == 0
    assert down_rows * steps == D_FF and down_rows % BF16_SUBLANES == 0

    def tile_of(s):
        return jnp.maximum(s - 1, 0)

    def x_map(s):
        return (tile_of(s) // tiles, tile_of(s) % tiles, 0)

    def slice_map(s):
        return (tile_of(s), 0)

    hbm = pl.BlockSpec(memory_space=pl.ANY)
    return pl.pallas_call(
        functools.partial(_mixer_kernel, tile=tile, tiles=tiles),
        grid=(steps + 1,),
        in_specs=[
            pl.BlockSpec((1, tile, D_MODEL), x_map),
            _resident((1, D_MODEL)),
            hbm,
            _resident((CONV_WIDTH, D_MODEL)),
            hbm,
            pl.BlockSpec(memory_space=pltpu.SMEM),
            _resident((2 * BLOCK, BLOCK)),
            pl.BlockSpec(memory_space=pltpu.SMEM),
            hbm,
            hbm,
            pl.BlockSpec((up_rows, D_FF), slice_map),
            pl.BlockSpec((down_rows, D_MODEL), slice_map),
        ],
        out_specs=[
            pl.BlockSpec((1, tile, D_MODEL), x_map),
            pl.BlockSpec((up_rows, D_FF), slice_map),
            pl.BlockSpec((down_rows, D_MODEL), slice_map),
        ],
        out_shape=[
            jax.ShapeDtypeStruct(x.shape, F32),
            jax.ShapeDtypeStruct(w_up.shape, BF16),
            jax.ShapeDtypeStruct(w_down.shape, BF16),
        ],
        scratch_shapes=[
            pltpu.VMEM((D_MODEL, IN_COLS), BF16),
            pltpu.VMEM((D_MODEL, D_MODEL), BF16),
            pltpu.VMEM((D_MODEL, D_MODEL), BF16),
            pltpu.VMEM((D_MODEL, D_MODEL), BF16),
            pltpu.VMEM((STAGE_SLOTS, D_MODEL, STAGE_COLS), F32),
            pltpu.SemaphoreType.DMA((STAGE_SLOTS,)),
            pltpu.VMEM((2, N_HEADS, 2 * BLOCK, BLOCK), F32),
            pltpu.VMEM((D_MODEL // LANES, tile + 2 * SUBLANES, LANES), F32),
            pltpu.VMEM((tile + BLOCK, 4 * KV_COLS), BF16),
            pltpu.VMEM((D_MODEL, tile), BF16),
        ],
        compiler_params=pltpu.CompilerParams(
            dimension_semantics=("arbitrary",), vmem_limit_bytes=VMEM_LIMIT),
        name="token_mixer",
    )(x, g, w_in, conv_w, w_conv_out, sinks, jnp.asarray(_bucket_table()), rel_bias, w_attn_out, w_o, w_up, w_down)


def _mlp(x, g, w_up, w_down, g_final):
    rows = x.shape[0]
    tile = sum(SUBTILES_MLP)
    assert rows % tile == 0 and all(s % BF16_SUBLANES == 0 for s in SUBTILES_MLP)
    return pl.pallas_call(
        functools.partial(_mlp_kernel, subtiles=SUBTILES_MLP),
        grid=(rows // tile,),
        in_specs=[
            pl.BlockSpec((tile, D_MODEL), lambda i: (i, 0)),
            _resident((1, D_MODEL)),
            _resident((D_MODEL, D_FF)),
            _resident((D_FF, D_MODEL)),
            _resident((1, D_MODEL)),
        ],
        out_specs=pl.BlockSpec((tile, D_MODEL), lambda i: (i, 0)),
        out_shape=jax.ShapeDtypeStruct(x.shape, F32),
        compiler_params=pltpu.CompilerParams(
            dimension_semantics=("arbitrary",), vmem_limit_bytes=VMEM_LIMIT),
        name="channel_mixer",
    )(x, g, w_up, w_down, g_final)


def kernel(x, attn_norm_g, w_in, conv_w, w_conv_out, attn_sinks, rel_bias, w_attn_out, w_o, mlp_norm_g, w_up,
           w_down, final_norm_g):
    batch, seq, _ = x.shape
    depth = w_in.shape[0]
    assert depth == 1, "the final norm is fused into the (only) layer's channel mixer"
    x, w_up_bf16, w_down_bf16 = _mixer(
        x, attn_norm_g.reshape(1, D_MODEL), w_in.reshape(D_MODEL, IN_COLS), conv_w.reshape(CONV_WIDTH, D_MODEL),
        w_conv_out.reshape(D_MODEL, D_MODEL), attn_sinks.reshape(N_HEADS).astype(F32), rel_bias.astype(F32),
        w_attn_out.reshape(D_MODEL, D_MODEL), w_o.reshape(D_MODEL, D_MODEL), w_up.reshape(D_MODEL, D_FF),
        w_down.reshape(D_FF, D_MODEL))
    x = _mlp(x.reshape(batch * seq, D_MODEL), mlp_norm_g.reshape(1, D_MODEL), w_up_bf16, w_down_bf16,
             final_norm_g.reshape(1, D_MODEL))
    return x.reshape(batch, seq, D_MODEL)
```

```python
import functools
import math

import jax
import jax.numpy as jnp
import numpy as np
from jax import lax
from jax.experimental import pallas as pl
from jax.experimental.pallas import tpu as pltpu

D_MODEL = 1024
HEAD_DIM = 64
N_HEADS = 16
N_KV_HEADS = 2
GROUP = N_HEADS // N_KV_HEADS
BLOCK = 128
N_BUCKETS = 32
MAX_DISTANCE = 128
CONV_WIDTH = 3
D_FF = 4 * D_MODEL
EPS = 1e-6
NEG_INF = -1e30
LOG2E = math.log2(math.e)

LANES = 128
SUBLANES = 8
BF16_SUBLANES = 16
KV_COLS = N_KV_HEADS * HEAD_DIM
PAIRS_PER_GROUP = GROUP // 2

OFF_B, OFF_C, OFF_X, OFF_Q = 0, D_MODEL, 2 * D_MODEL, 3 * D_MODEL
OFF_K = 4 * D_MODEL
OFF_V = OFF_K + KV_COLS
OFF_GC = OFF_V + KV_COLS
OFF_GA = OFF_GC + D_MODEL
IN_COLS = OFF_GA + D_MODEL

TILE_MIX = 512
SUBTILES_MLP = (512, 512)
VMEM_LIMIT = 56 * 1024 * 1024
STAGE_COLS = 640
STAGE_SLOTS = 3

BF16 = jnp.bfloat16
F32 = jnp.float32

NT_DIMS = (((1,), (1,)), ((), ()))
TN_DIMS = (((0,), (0,)), ((), ()))


def _bucket_table():
    kj = np.arange(2 * BLOCK)[:, None]
    qi = np.arange(BLOCK)[None, :]
    dist = qi + BLOCK - kj
    n = np.maximum(dist, 0)
    max_exact = N_BUCKETS // 2
    ratio = np.log(np.maximum(n, max_exact).astype(np.float32) / max_exact) / np.log(MAX_DISTANCE / max_exact)
    large = np.minimum(max_exact + (ratio * (N_BUCKETS - max_exact)).astype(np.int32), N_BUCKETS - 1)
    bucket = np.where(n < max_exact, n, large).astype(np.int32)
    in_window = (dist >= 0) & (dist < BLOCK)
    return np.where(in_window, bucket, -1).astype(np.int32)


def _bias_table(head, bucket_ref, rel_ref, bias_buf):
    bucket = bucket_ref[...]
    acc = jnp.full(bucket.shape, NEG_INF, F32)
    for b in range(N_BUCKETS):
        acc = jnp.where(bucket == b, rel_ref[b, head], acc)
    acc = acc * LOG2E
    bias_buf[0, head] = acc
    key = lax.broadcasted_iota(jnp.int32, bucket.shape, 0)
    bias_buf[1, head] = jnp.where(key >= BLOCK, acc, NEG_INF)


def _rmsnorm(x, g):
    y = x * lax.rsqrt(jnp.mean(x * x, axis=-1, keepdims=True) + EPS)
    return y * g


def _stage_weights(pairs, stage, sems, fillers):
    chunks = []
    for src, dst in pairs:
        cols = src.shape[1]
        width = max(w for w in range(LANES, STAGE_COLS + 1, LANES) if cols % w == 0)
        chunks += [(src, dst, c, width) for c in range(0, cols, width)]
    slots = stage.shape[0]

    def copy(i):
        src, _, c, width = chunks[i]
        return pltpu.make_async_copy(src.at[:, c:c + width], stage.at[i % slots, :, 0:width], sems.at[i % slots])

    for i in range(min(slots, len(chunks))):
        copy(i).start()
    fillers = list(fillers)
    assert len(fillers) <= len(chunks)
    for i, (_, dst, c, width) in enumerate(chunks):
        if fillers:
            fillers.pop(0)()
        copy(i).wait()
        dst[:, c:c + width] = stage[i % slots, :, 0:width].astype(BF16)
        if i + slots < len(chunks):
            copy(i + slots).start()


def _mixer_kernel(x_ref, g_ref, win_hbm, convw_ref, wco_hbm, sinks_ref, bucket_ref, rel_ref, wao_hbm, wo_hbm,
                  wup_ref, wdown_ref, out_ref, wup_out_ref, wdown_out_ref,
                  win_buf, wco_buf, wao_buf, wo_buf, stage, stage_sems, bias_buf, cx_buf, kv_buf, attn_buf,
                  *, tile, tiles):
    s = pl.program_id(0)

    @pl.when(s == 0)
    def _():
        bias_tasks = [functools.partial(_bias_table, head, bucket_ref, rel_ref, bias_buf)
                      for head in range(N_HEADS)]
        _stage_weights([(win_hbm, win_buf), (wco_hbm, wco_buf), (wao_hbm, wao_buf), (wo_hbm, wo_buf)],
                       stage, stage_sems, bias_tasks)

    @pl.when(s > 0)
    def _():
        _mixer_step((s - 1) % tiles, x_ref, g_ref, win_buf, convw_ref, wco_buf, sinks_ref, bias_buf, wao_buf,
                    wo_buf, wup_ref, wdown_ref, out_ref, wup_out_ref, wdown_out_ref, cx_buf, kv_buf, attn_buf,
                    tile=tile)


def _mixer_step(t, x_ref, g_ref, win_ref, convw_ref, wco_ref, sinks_ref, bias_ref, wao_ref, wo_ref,
                wup_ref, wdown_ref, out_ref, wup_out_ref, wdown_out_ref, cx_buf, kv_buf, attn_buf, *, tile):
    @pl.when(t == 0)
    def _():
        cx_buf[:, 0:SUBLANES, :] = jnp.zeros((D_MODEL // LANES, SUBLANES, LANES), F32)
        kv_buf[0:BLOCK, :] = jnp.zeros((BLOCK, 4 * KV_COLS), BF16)

    x = x_ref[0]
    half = tile // 2
    h_halves = [_rmsnorm(x[r * half:(r + 1) * half], g_ref[...]).astype(BF16) for r in range(2)]
    h = jnp.concatenate(h_halves, axis=0)

    def proj(lo, width):
        return jnp.dot(h, win_ref[:, lo:lo + width], preferred_element_type=F32)

    w_q = win_ref[:, OFF_Q:OFF_Q + D_MODEL]
    w_kv = win_ref[:, OFF_K:OFF_K + 2 * KV_COLS]
    q_0 = jnp.dot(h_halves[0], w_q, preferred_element_type=F32)
    kv = jnp.concatenate([jnp.dot(hr, w_kv, preferred_element_type=F32) for hr in h_halves], axis=0)
    k = kv[:, 0:KV_COLS]
    val = kv[:, KV_COLS:2 * KV_COLS]
    k_swapped = pltpu.roll(k, HEAD_DIM, axis=1)
    low_half_tile = lax.broadcasted_iota(jnp.int32, (tile, LANES), 1) < HEAD_DIM
    new_rows = slice(BLOCK, BLOCK + tile)
    kv_buf[new_rows, 0:KV_COLS] = jnp.where(low_half_tile, k, k_swapped).astype(BF16)
    kv_buf[new_rows, KV_COLS:2 * KV_COLS] = jnp.where(low_half_tile, k_swapped, k).astype(BF16)
    kv_buf[new_rows, 2 * KV_COLS:3 * KV_COLS] = jnp.where(low_half_tile, val, 1.0).astype(BF16)
    kv_buf[new_rows, 3 * KV_COLS:4 * KV_COLS] = jnp.where(low_half_tile, 1.0, val).astype(BF16)
    q_1 = jnp.dot(h_halves[1], w_q, preferred_element_type=F32)
    q = (jnp.concatenate([q_0, q_1], axis=0) * (HEAD_DIM ** -0.5 * LOG2E)).astype(BF16)

    low_half = lax.broadcasted_iota(jnp.int32, (BLOCK, LANES), 1) < HEAD_DIM
    zero = jnp.zeros((BLOCK, LANES), BF16)

    scores, probs, sink_terms = {}, {}, {}

    def qk(j):
        rows = slice(j * BLOCK, (j + 2) * BLOCK)
        qj = q[j * BLOCK:(j + 1) * BLOCK, :]
        for g in range(N_KV_HEADS):
            parts = []
            for p in range(g * PAIRS_PER_GROUP, (g + 1) * PAIRS_PER_GROUP):
                qp = qj[:, p * LANES:(p + 1) * LANES]
                parts.append(jnp.where(low_half, qp, zero))
                parts.append(jnp.where(low_half, zero, qp))
            q_heads = jnp.concatenate(parts, axis=0)
            keys = kv_buf[rows, g * KV_COLS:(g + 1) * KV_COLS]
            scores[j, g] = lax.dot_general(keys, q_heads, NT_DIMS, preferred_element_type=F32)

    def softmax(j, g):
        first = (t == 0).astype(jnp.int32) if j == 0 else 0
        s = scores.pop((j, g))
        for i in range(GROUP):
            head = g * GROUP + i
            sh = s[:, i * BLOCK:(i + 1) * BLOCK] + bias_ref[first, head]
            sink = jnp.full((1, BLOCK), sinks_ref[head], F32) * LOG2E
            m = jnp.maximum(jnp.max(sh, axis=0, keepdims=True), sink)
            probs[j, head] = jnp.exp2(sh - m).astype(BF16)
            sink_terms[j, head] = jnp.exp2(sink - m)

    def pv(j):
        rows = slice(j * BLOCK, (j + 2) * BLOCK)
        for g in range(N_KV_HEADS):
            vals = kv_buf[rows, (2 + g) * KV_COLS:(3 + g) * KV_COLS]
            p = jnp.concatenate([probs.pop((j, g * GROUP + i)) for i in range(GROUP)], axis=1)
            o = lax.dot_general(vals, p, TN_DIMS, preferred_element_type=F32)
            out_rows = slice(g * HEAD_DIM, (g + 1) * HEAD_DIM)
            sum_row = (1 - g) * HEAD_DIM
            for i in range(GROUP):
                head = g * GROUP + i
                cols = slice(i * BLOCK, (i + 1) * BLOCK)
                denom = o[sum_row:sum_row + 1, cols] + sink_terms.pop((j, head))
                oh = o[out_rows, cols] * (1.0 / denom)
                attn_buf[head * HEAD_DIM:(head + 1) * HEAD_DIM, j * BLOCK:(j + 1) * BLOCK] = oh.astype(BF16)

    v = {}

    def t_c():
        v["c"] = proj(OFF_C, D_MODEL)

    def t_x():
        cx = v.pop("c") * proj(OFF_X, D_MODEL)
        for c in range(D_MODEL // LANES):
            cx_buf[c, SUBLANES:SUBLANES + tile, :] = cx[:, c * LANES:(c + 1) * LANES]

    def t_conv():
        slabs = []
        for c in range(D_MODEL // LANES):
            taps = [convw_ref[0:1, k * D_MODEL + c * LANES:k * D_MODEL + (c + 1) * LANES] for k in range(CONV_WIDTH)]
            slabs.append(taps[0] * cx_buf[c, SUBLANES - 2:SUBLANES - 2 + tile, :]
                         + taps[1] * cx_buf[c, SUBLANES - 1:SUBLANES - 1 + tile, :]
                         + taps[2] * cx_buf[c, SUBLANES:SUBLANES + tile, :])
        v["conv"] = jnp.concatenate(slabs, axis=1)
        cx_buf[:, 0:SUBLANES, :] = cx_buf[:, tile:tile + SUBLANES, :]

    def t_b():
        v["u"] = (proj(OFF_B, D_MODEL) * v.pop("conv")).astype(BF16)

    def t_gc():
        v["gate_conv"] = jax.nn.sigmoid(proj(OFF_GC, D_MODEL))

    def t_ga():
        v["gate_attn"] = jax.nn.sigmoid(proj(OFF_GA, D_MODEL))

    def t_yconv():
        v["mixed"] = v.pop("gate_conv") * jnp.dot(v.pop("u"), wco_ref[...], preferred_element_type=F32)

    def t_yattn(r):
        rows = slice(r * half, (r + 1) * half)
        y_attn = lax.dot_general(attn_buf[:, rows], wao_ref[...], TN_DIMS, preferred_element_type=F32)
        v["mixed", r] = (v["mixed"][rows] + v["gate_attn"][rows] * y_attn).astype(BF16)

    def t_out(r):
        rows = slice(r * half, (r + 1) * half)
        out_ref[0, rows, :] = x[rows] + jnp.dot(v.pop(("mixed", r)), wo_ref[...], preferred_element_type=F32)

    qk(0)
    t_c()
    wup_out_ref[...] = wup_ref[...].astype(BF16)
    wdown_out_ref[...] = wdown_ref[...].astype(BF16)
    softmax(0, 0)
    t_x()
    softmax(0, 1)
    t_gc()
    t_conv()
    pv(0)
    qk(1)
    t_b()
    softmax(1, 0)
    t_ga()
    softmax(1, 1)
    pv(1)
    t_yconv()
    qk(2)
    softmax(2, 0)
    t_yattn(0)
    softmax(2, 1)
    pv(2)
    qk(3)
    t_out(0)
    softmax(3, 0)
    softmax(3, 1)
    pv(3)
    kv_buf[0:BLOCK, :] = kv_buf[tile:tile + BLOCK, :]
    t_yattn(1)
    t_out(1)


def _mlp_kernel(x_ref, g_ref, wup_ref, wdown_ref, gf_ref, out_ref, *, subtiles):
    normed, act, mixed = {}, {}, {}
    starts = [sum(subtiles[:i]) for i in range(len(subtiles))]

    def rows(i):
        return slice(starts[i], starts[i] + subtiles[i])

    def norm(i):
        normed[i] = _rmsnorm(x_ref[rows(i), :], g_ref[...]).astype(BF16)

    def up(i):
        u = jnp.dot(normed.pop(i), wup_ref[...], preferred_element_type=F32)
        act[i] = jnp.square(jnp.maximum(u, 0.0)).astype(BF16)

    def down(i):
        mixed[i] = x_ref[rows(i), :] + jnp.dot(act.pop(i), wdown_ref[...], preferred_element_type=F32)

    def final(i):
        out_ref[rows(i), :] = _rmsnorm(mixed.pop(i), gf_ref[...])

    n = len(subtiles)
    norm(0)
    up(0)
    for i in range(n):
        if i + 1 < n:
            norm(i + 1)
            up(i + 1)
        if i > 0:
            final(i - 1)
        down(i)
    final(n - 1)


def _resident(shape):
    return pl.BlockSpec(shape, lambda *_: (0,) * len(shape), pipeline_mode=pl.Buffered(1))


def _mixer(x, g, w_in, conv_w, w_conv_out, sinks, rel_bias, w_attn_out, w_o, w_up, w_down):
    batch, seq, _ = x.shape
    tile = TILE_MIX
    assert seq % tile == 0 and tile % BLOCK == 0
    tiles = seq // tile
    steps = batch * tiles
    up_rows, down_rows = D_MODEL // steps, D_FF // steps
    assert up_rows * steps == D_MODEL and up_rows % BF16_SUBLANES == 0
    assert down_rows * steps == D_FF and down_rows % BF16_SUBLANES == 0

    def tile_of(s):
        return jnp.maximum(s - 1, 0)

    def x_map(s):
        return (tile_of(s) // tiles, tile_of(s) % tiles, 0)

    def slice_map(s):
        return (tile_of(s), 0)

    hbm = pl.BlockSpec(memory_space=pl.ANY)
    return pl.pallas_call(
        functools.partial(_mixer_kernel, tile=tile, tiles=tiles),
        grid=(steps + 1,),
        in_specs=[
            pl.BlockSpec((1, tile, D_MODEL), x_map),
            _resident((1, D_MODEL)),
            hbm,
            _resident((1, CONV_WIDTH * D_MODEL)),
            hbm,
            pl.BlockSpec(memory_space=pltpu.SMEM),
            _resident((2 * BLOCK, BLOCK)),
            pl.BlockSpec(memory_space=pltpu.SMEM),
            hbm,
            hbm,
            pl.BlockSpec((up_rows, D_FF), slice_map),
            pl.BlockSpec((down_rows, D_MODEL), slice_map),
        ],
        out_specs=[
            pl.BlockSpec((1, tile, D_MODEL), x_map),
            pl.BlockSpec((up_rows, D_FF), slice_map),
            pl.BlockSpec((down_rows, D_MODEL), slice_map),
        ],
        out_shape=[
            jax.ShapeDtypeStruct(x.shape, F32),
            jax.ShapeDtypeStruct(w_up.shape, BF16),
            jax.ShapeDtypeStruct(w_down.shape, BF16),
        ],
        scratch_shapes=[
            pltpu.VMEM((D_MODEL, IN_COLS), BF16),
            pltpu.VMEM((D_MODEL, D_MODEL), BF16),
            pltpu.VMEM((D_MODEL, D_MODEL), BF16),
            pltpu.VMEM((D_MODEL, D_MODEL), BF16),
            pltpu.VMEM((STAGE_SLOTS, D_MODEL, STAGE_COLS), F32),
            pltpu.SemaphoreType.DMA((STAGE_SLOTS,)),
            pltpu.VMEM((2, N_HEADS, 2 * BLOCK, BLOCK), F32),
            pltpu.VMEM((D_MODEL // LANES, tile + 2 * SUBLANES, LANES), F32),
            pltpu.VMEM((tile + BLOCK, 4 * KV_COLS), BF16),
            pltpu.VMEM((D_MODEL, tile), BF16),
        ],
        compiler_params=pltpu.CompilerParams(
            dimension_semantics=("arbitrary",), vmem_limit_bytes=VMEM_LIMIT),
        name="token_mixer",
    )(x, g, w_in, conv_w, w_conv_out, sinks, jnp.asarray(_bucket_table()), rel_bias, w_attn_out, w_o, w_up, w_down)


def _mlp(x, g, w_up, w_down, g_final):
    rows = x.shape[0]
    tile = sum(SUBTILES_MLP)
    assert rows % tile == 0 and all(s % BF16_SUBLANES == 0 for s in SUBTILES_MLP)
    return pl.pallas_call(
        functools.partial(_mlp_kernel, subtiles=SUBTILES_MLP),
        grid=(rows // tile,),
        in_specs=[
            pl.BlockSpec((tile, D_MODEL), lambda i: (i, 0)),
            _resident((1, D_MODEL)),
            _resident((D_MODEL, D_FF)),
            _resident((D_FF, D_MODEL)),
            _resident((1, D_MODEL)),
        ],
        out_specs=pl.BlockSpec((tile, D_MODEL), lambda i: (i, 0)),
        out_shape=jax.ShapeDtypeStruct(x.shape, F32),
        compiler_params=pltpu.CompilerParams(
            dimension_semantics=("arbitrary",), vmem_limit_bytes=VMEM_LIMIT),
        name="channel_mixer",
    )(x, g, w_up, w_down, g_final)


def kernel(x, attn_norm_g, w_in, conv_w, w_conv_out, attn_sinks, rel_bias, w_attn_out, w_o, mlp_norm_g, w_up,
           w_down, final_norm_g):
    batch, seq, _ = x.shape
    depth = w_in.shape[0]
    assert depth == 1, "the final norm is fused into the (only) layer's channel mixer"
    x, w_up_bf16, w_down_bf16 = _mixer(
        x, attn_norm_g.reshape(1, D_MODEL), w_in.reshape(D_MODEL, IN_COLS), conv_w.reshape(1, CONV_WIDTH * D_MODEL),
        w_conv_out.reshape(D_MODEL, D_MODEL), attn_sinks.reshape(N_HEADS).astype(F32), rel_bias.astype(F32),
        w_attn_out.reshape(D_MODEL, D_MODEL), w_o.reshape(D_MODEL, D_MODEL), w_up.reshape(D_MODEL, D_FF),
        w_down.reshape(D_FF, D_MODEL))
    x = _mlp(x.reshape(batch * seq, D_MODEL), mlp_norm_g.reshape(1, D_MODEL), w_up_bf16, w_down_bf16,
             final_norm_g.reshape(1, D_MODEL))
    return x.reshape(batch, seq, D_MODEL)
```

```python
import functools
import math

import jax
import jax.numpy as jnp
import numpy as np
from jax import lax
from jax.experimental import pallas as pl
from jax.experimental.pallas import tpu as pltpu

D_MODEL = 1024
HEAD_DIM = 64
N_HEADS = 16
N_KV_HEADS = 2
GROUP = N_HEADS // N_KV_HEADS
BLOCK = 128
N_BUCKETS = 32
MAX_DISTANCE = 128
CONV_WIDTH = 3
D_FF = 4 * D_MODEL
EPS = 1e-6
NEG_INF = -1e30
LOG2E = math.log2(math.e)

LANES = 128
SUBLANES = 8
BF16_SUBLANES = 16
KV_COLS = N_KV_HEADS * HEAD_DIM
PAIRS_PER_GROUP = GROUP // 2

OFF_B, OFF_C, OFF_X, OFF_Q = 0, D_MODEL, 2 * D_MODEL, 3 * D_MODEL
OFF_K = 4 * D_MODEL
OFF_V = OFF_K + KV_COLS
OFF_GC = OFF_V + KV_COLS
OFF_GA = OFF_GC + D_MODEL
IN_COLS = OFF_GA + D_MODEL

TILE_MIX = 512
SUBTILES_MLP = (512, 512)
VMEM_LIMIT = 56 * 1024 * 1024
STAGE_COLS = 640
STAGE_SLOTS = 3

BF16 = jnp.bfloat16
F32 = jnp.float32

NT_DIMS = (((1,), (1,)), ((), ()))
TN_DIMS = (((0,), (0,)), ((), ()))


def _bucket_starts():
    n = np.arange(BLOCK)
    max_exact = N_BUCKETS // 2
    ratio = np.log(np.maximum(n, max_exact).astype(np.float32) / max_exact) / np.log(MAX_DISTANCE / max_exact)
    large = np.minimum(max_exact + (ratio * (N_BUCKETS - max_exact)).astype(np.int32), N_BUCKETS - 1)
    bucket = np.where(n < max_exact, n, large).astype(np.int32)
    assert np.all(np.diff(bucket) >= 0)
    return [int(np.searchsorted(bucket, b, side="left")) for b in range(N_BUCKETS)]


def _bias_table(head, rel_ref, bias_buf):
    shape = (2 * BLOCK, BLOCK)
    key = lax.broadcasted_iota(jnp.int32, shape, 0)
    dist = lax.broadcasted_iota(jnp.int32, shape, 1) + BLOCK - key
    acc = jnp.full(shape, NEG_INF, F32)
    for b, start in enumerate(_bucket_starts()):
        if start < BLOCK:
            acc = jnp.where(dist >= start, rel_ref[b, head], acc)
    acc = jnp.where((dist >= 0) & (dist < BLOCK), acc, NEG_INF) * LOG2E
    bias_buf[0, head] = acc
    bias_buf[1, head] = jnp.where(key >= BLOCK, acc, NEG_INF)


def _rmsnorm(x, g):
    y = x * lax.rsqrt(jnp.mean(x * x, axis=-1, keepdims=True) + EPS)
    return y * g


def _stage_weights(pairs, stage, sems, fillers):
    chunks = []
    for src, dst in pairs:
        cols = src.shape[1]
        width = max(w for w in range(LANES, STAGE_COLS + 1, LANES) if cols % w == 0)
        chunks += [(src, dst, c, width) for c in range(0, cols, width)]
    slots = stage.shape[0]

    def copy(i):
        src, _, c, width = chunks[i]
        return pltpu.make_async_copy(src.at[:, c:c + width], stage.at[i % slots, :, 0:width], sems.at[i % slots])

    for i in range(min(slots, len(chunks))):
        copy(i).start()
    fillers = list(fillers)
    assert len(fillers) <= len(chunks)
    for i, (_, dst, c, width) in enumerate(chunks):
        if fillers:
            fillers.pop(0)()
        copy(i).wait()
        dst[:, c:c + width] = stage[i % slots, :, 0:width].astype(BF16)
        if i + slots < len(chunks):
            copy(i + slots).start()


def _mixer_kernel(x_ref, g_ref, win_hbm, convw_ref, wco_hbm, sinks_ref, rel_ref, wao_hbm, wo_hbm,
                  wup_ref, wdown_ref, out_ref, wup_out_ref, wdown_out_ref,
                  win_buf, wco_buf, wao_buf, wo_buf, stage, stage_sems, bias_buf, cx_buf, kv_buf, attn_buf,
                  *, tile, tiles):
    s = pl.program_id(0)

    @pl.when(s == 0)
    def _():
        bias_tasks = [functools.partial(_bias_table, head, rel_ref, bias_buf)
                      for head in range(N_HEADS)]
        _stage_weights([(win_hbm, win_buf), (wco_hbm, wco_buf), (wao_hbm, wao_buf), (wo_hbm, wo_buf)],
                       stage, stage_sems, bias_tasks)

    @pl.when(s > 0)
    def _():
        _mixer_step((s - 1) % tiles, x_ref, g_ref, win_buf, convw_ref, wco_buf, sinks_ref, bias_buf, wao_buf,
                    wo_buf, wup_ref, wdown_ref, out_ref, wup_out_ref, wdown_out_ref, cx_buf, kv_buf, attn_buf,
                    tile=tile)


def _mixer_step(t, x_ref, g_ref, win_ref, convw_ref, wco_ref, sinks_ref, bias_ref, wao_ref, wo_ref,
                wup_ref, wdown_ref, out_ref, wup_out_ref, wdown_out_ref, cx_buf, kv_buf, attn_buf, *, tile):
    @pl.when(t == 0)
    def _():
        cx_buf[:, 0:SUBLANES, :] = jnp.zeros((D_MODEL // LANES, SUBLANES, LANES), F32)
        kv_buf[0:BLOCK, :] = jnp.zeros((BLOCK, 4 * KV_COLS), BF16)

    x = x_ref[0]
    half = tile // 2
    h_halves = [_rmsnorm(x[r * half:(r + 1) * half], g_ref[...]).astype(BF16) for r in range(2)]
    h = jnp.concatenate(h_halves, axis=0)

    def proj(lo, width):
        return jnp.dot(h, win_ref[:, lo:lo + width], preferred_element_type=F32)

    w_q = win_ref[:, OFF_Q:OFF_Q + D_MODEL]
    w_kv = win_ref[:, OFF_K:OFF_K + 2 * KV_COLS]
    q_0 = jnp.dot(h_halves[0], w_q, preferred_element_type=F32)
    kv = jnp.concatenate([jnp.dot(hr, w_kv, preferred_element_type=F32) for hr in h_halves], axis=0)
    k = kv[:, 0:KV_COLS]
    val = kv[:, KV_COLS:2 * KV_COLS]
    k_swapped = pltpu.roll(k, HEAD_DIM, axis=1)
    low_half_tile = lax.broadcasted_iota(jnp.int32, (tile, LANES), 1) < HEAD_DIM
    new_rows = slice(BLOCK, BLOCK + tile)
    kv_buf[new_rows, 0:KV_COLS] = jnp.where(low_half_tile, k, k_swapped).astype(BF16)
    kv_buf[new_rows, KV_COLS:2 * KV_COLS] = jnp.where(low_half_tile, k_swapped, k).astype(BF16)
    kv_buf[new_rows, 2 * KV_COLS:3 * KV_COLS] = jnp.where(low_half_tile, val, 1.0).astype(BF16)
    kv_buf[new_rows, 3 * KV_COLS:4 * KV_COLS] = jnp.where(low_half_tile, 1.0, val).astype(BF16)
    q_1 = jnp.dot(h_halves[1], w_q, preferred_element_type=F32)
    q = (jnp.concatenate([q_0, q_1], axis=0) * (HEAD_DIM ** -0.5 * LOG2E)).astype(BF16)

    low_half = lax.broadcasted_iota(jnp.int32, (BLOCK, LANES), 1) < HEAD_DIM
    zero = jnp.zeros((BLOCK, LANES), BF16)

    scores, probs, sink_terms = {}, {}, {}

    def qk(j):
        rows = slice(j * BLOCK, (j + 2) * BLOCK)
        qj = q[j * BLOCK:(j + 1) * BLOCK, :]
        for g in range(N_KV_HEADS):
            parts = []
            for p in range(g * PAIRS_PER_GROUP, (g + 1) * PAIRS_PER_GROUP):
                qp = qj[:, p * LANES:(p + 1) * LANES]
                parts.append(jnp.where(low_half, qp, zero))
                parts.append(jnp.where(low_half, zero, qp))
            q_heads = jnp.concatenate(parts, axis=0)
            keys = kv_buf[rows, g * KV_COLS:(g + 1) * KV_COLS]
            scores[j, g] = lax.dot_general(keys, q_heads, NT_DIMS, preferred_element_type=F32)

    def softmax(j, g):
        first = (t == 0).astype(jnp.int32) if j == 0 else 0
        s = scores.pop((j, g))
        for i in range(GROUP):
            head = g * GROUP + i
            sh = s[:, i * BLOCK:(i + 1) * BLOCK] + bias_ref[first, head]
            sink = jnp.full((1, BLOCK), sinks_ref[head], F32) * LOG2E
            m = jnp.maximum(jnp.max(sh, axis=0, keepdims=True), sink)
            probs[j, head] = jnp.exp2(sh - m).astype(BF16)
            sink_terms[j, head] = jnp.exp2(sink - m)

    def pv(j):
        rows = slice(j * BLOCK, (j + 2) * BLOCK)
        for g in range(N_KV_HEADS):
            vals = kv_buf[rows, (2 + g) * KV_COLS:(3 + g) * KV_COLS]
            p = jnp.concatenate([probs.pop((j, g * GROUP + i)) for i in range(GROUP)], axis=1)
            o = lax.dot_general(vals, p, TN_DIMS, preferred_element_type=F32)
            out_rows = slice(g * HEAD_DIM, (g + 1) * HEAD_DIM)
            sum_row = (1 - g) * HEAD_DIM
            for i in range(GROUP):
                head = g * GROUP + i
                cols = slice(i * BLOCK, (i + 1) * BLOCK)
                denom = o[sum_row:sum_row + 1, cols] + sink_terms.pop((j, head))
                oh = o[out_rows, cols] * (1.0 / denom)
                attn_buf[head * HEAD_DIM:(head + 1) * HEAD_DIM, j * BLOCK:(j + 1) * BLOCK] = oh.astype(BF16)

    v = {}

    def t_c():
        v["c"] = proj(OFF_C, D_MODEL)

    def t_x():
        cx = v.pop("c") * proj(OFF_X, D_MODEL)
        for c in range(D_MODEL // LANES):
            cx_buf[c, SUBLANES:SUBLANES + tile, :] = cx[:, c * LANES:(c + 1) * LANES]

    def t_conv():
        slabs = []
        for c in range(D_MODEL // LANES):
            taps = [convw_ref[0:1, k * D_MODEL + c * LANES:k * D_MODEL + (c + 1) * LANES] for k in range(CONV_WIDTH)]
            slabs.append(taps[0] * cx_buf[c, SUBLANES - 2:SUBLANES - 2 + tile, :]
                         + taps[1] * cx_buf[c, SUBLANES - 1:SUBLANES - 1 + tile, :]
                         + taps[2] * cx_buf[c, SUBLANES:SUBLANES + tile, :])
        v["conv"] = jnp.concatenate(slabs, axis=1)
        cx_buf[:, 0:SUBLANES, :] = cx_buf[:, tile:tile + SUBLANES, :]

    def t_b():
        v["u"] = (proj(OFF_B, D_MODEL) * v.pop("conv")).astype(BF16)

    def t_gc():
        v["gate_conv"] = jax.nn.sigmoid(proj(OFF_GC, D_MODEL))

    def t_ga():
        v["gate_attn"] = jax.nn.sigmoid(proj(OFF_GA, D_MODEL))

    def t_yconv():
        v["mixed"] = v.pop("gate_conv") * jnp.dot(v.pop("u"), wco_ref[...], preferred_element_type=F32)

    def t_yattn(r):
        rows = slice(r * half, (r + 1) * half)
        y_attn = lax.dot_general(attn_buf[:, rows], wao_ref[...], TN_DIMS, preferred_element_type=F32)
        v["mixed", r] = (v["mixed"][rows] + v["gate_attn"][rows] * y_attn).astype(BF16)

    def t_out(r):
        rows = slice(r * half, (r + 1) * half)
        out_ref[0, rows, :] = x[rows] + jnp.dot(v.pop(("mixed", r)), wo_ref[...], preferred_element_type=F32)

    qk(0)
    t_c()
    wup_out_ref[...] = wup_ref[...].astype(BF16)
    wdown_out_ref[...] = wdown_ref[...].astype(BF16)
    softmax(0, 0)
    t_x()
    softmax(0, 1)
    t_gc()
    t_conv()
    pv(0)
    qk(1)
    t_b()
    softmax(1, 0)
    t_ga()
    softmax(1, 1)
    pv(1)
    t_yconv()
    qk(2)
    softmax(2, 0)
    t_yattn(0)
    softmax(2, 1)
    pv(2)
    qk(3)
    t_out(0)
    softmax(3, 0)
    softmax(3, 1)
    pv(3)
    kv_buf[0:BLOCK, :] = kv_buf[tile:tile + BLOCK, :]
    t_yattn(1)
    t_out(1)


def _mlp_kernel(x_ref, g_ref, wup_ref, wdown_ref, gf_ref, out_ref, *, subtiles):
    normed, act, mixed = {}, {}, {}
    starts = [sum(subtiles[:i]) for i in range(len(subtiles))]

    def rows(i):
        return slice(starts[i], starts[i] + subtiles[i])

    def norm(i):
        normed[i] = _rmsnorm(x_ref[rows(i), :], g_ref[...]).astype(BF16)

    def up(i):
        u = jnp.dot(normed.pop(i), wup_ref[...], preferred_element_type=F32)
        act[i] = jnp.square(jnp.maximum(u, 0.0)).astype(BF16)

    def down(i):
        mixed[i] = x_ref[rows(i), :] + jnp.dot(act.pop(i), wdown_ref[...], preferred_element_type=F32)

    def final(i):
        out_ref[rows(i), :] = _rmsnorm(mixed.pop(i), gf_ref[...])

    n = len(subtiles)
    norm(0)
    up(0)
    for i in range(n):
        if i + 1 < n:
            norm(i + 1)
            up(i + 1)
        if i > 0:
            final(i - 1)
        down(i)
    final(n - 1)


def _resident(shape):
    return pl.BlockSpec(shape, lambda *_: (0,) * len(shape), pipeline_mode=pl.Buffered(1))


def _mixer(x, g, w_in, conv_w, w_conv_out, sinks, rel_bias, w_attn_out, w_o, w_up, w_down):
    batch, seq, _ = x.shape
    tile = TILE_MIX
    assert seq % tile == 0 and tile % BLOCK == 0
    tiles = seq // tile
    steps = batch * tiles
    up_rows, down_rows = D_MODEL // steps, D_FF // steps
    assert up_rows * steps == D_MODEL and up_rows % BF16_SUBLANES == 0
    assert down_rows * steps == D_FF and down_rows % BF16_SUBLANES == 0

    def tile_of(s):
        return jnp.maximum(s - 1, 0)

    def x_map(s):
        return (tile_of(s) // tiles, tile_of(s) % tiles, 0)

    def slice_map(s):
        return (tile_of(s), 0)

    hbm = pl.BlockSpec(memory_space=pl.ANY)
    return pl.pallas_call(
        functools.partial(_mixer_kernel, tile=tile, tiles=tiles),
        grid=(steps + 1,),
        in_specs=[
            pl.BlockSpec((1, tile, D_MODEL), x_map),
            _resident((1, D_MODEL)),
            hbm,
            _resident((1, CONV_WIDTH * D_MODEL)),
            hbm,
            pl.BlockSpec(memory_space=pltpu.SMEM),
            pl.BlockSpec(memory_space=pltpu.SMEM),
            hbm,
            hbm,
            pl.BlockSpec((up_rows, D_FF), slice_map),
            pl.BlockSpec((down_rows, D_MODEL), slice_map),
        ],
        out_specs=[
            pl.BlockSpec((1, tile, D_MODEL), x_map),
            pl.BlockSpec((up_rows, D_FF), slice_map),
            pl.BlockSpec((down_rows, D_MODEL), slice_map),
        ],
        out_shape=[
            jax.ShapeDtypeStruct(x.shape, F32),
            jax.ShapeDtypeStruct(w_up.shape, BF16),
            jax.ShapeDtypeStruct(w_down.shape, BF16),
        ],
        scratch_shapes=[
            pltpu.VMEM((D_MODEL, IN_COLS), BF16),
            pltpu.VMEM((D_MODEL, D_MODEL), BF16),
            pltpu.VMEM((D_MODEL, D_MODEL), BF16),
            pltpu.VMEM((D_MODEL, D_MODEL), BF16),
            pltpu.VMEM((STAGE_SLOTS, D_MODEL, STAGE_COLS), F32),
            pltpu.SemaphoreType.DMA((STAGE_SLOTS,)),
            pltpu.VMEM((2, N_HEADS, 2 * BLOCK, BLOCK), F32),
            pltpu.VMEM((D_MODEL // LANES, tile + 2 * SUBLANES, LANES), F32),
            pltpu.VMEM((tile + BLOCK, 4 * KV_COLS), BF16),
            pltpu.VMEM((D_MODEL, tile), BF16),
        ],
        compiler_params=pltpu.CompilerParams(
            dimension_semantics=("arbitrary",), vmem_limit_bytes=VMEM_LIMIT),
        name="token_mixer",
    )(x, g, w_in, conv_w, w_conv_out, sinks, rel_bias, w_attn_out, w_o, w_up, w_down)


def _mlp(x, g, w_up, w_down, g_final):
    rows = x.shape[0]
    tile = sum(SUBTILES_MLP)
    assert rows % tile == 0 and all(s % BF16_SUBLANES == 0 for s in SUBTILES_MLP)
    return pl.pallas_call(
        functools.partial(_mlp_kernel, subtiles=SUBTILES_MLP),
        grid=(rows // tile,),
        in_specs=[
            pl.BlockSpec((tile, D_MODEL), lambda i: (i, 0)),
            _resident((1, D_MODEL)),
            _resident((D_MODEL, D_FF)),
            _resident((D_FF, D_MODEL)),
            _resident((1, D_MODEL)),
        ],
        out_specs=pl.BlockSpec((tile, D_MODEL), lambda i: (i, 0)),
        out_shape=jax.ShapeDtypeStruct(x.shape, F32),
        compiler_params=pltpu.CompilerParams(
            dimension_semantics=("arbitrary",), vmem_limit_bytes=VMEM_LIMIT),
        name="channel_mixer",
    )(x, g, w_up, w_down, g_final)


def kernel(x, attn_norm_g, w_in, conv_w, w_conv_out, attn_sinks, rel_bias, w_attn_out, w_o, mlp_norm_g, w_up,
           w_down, final_norm_g):
    batch, seq, _ = x.shape
    depth = w_in.shape[0]
    assert depth == 1, "the final norm is fused into the (only) layer's channel mixer"
    x, w_up_bf16, w_down_bf16 = _mixer(
        x, attn_norm_g.reshape(1, D_MODEL), w_in.reshape(D_MODEL, IN_COLS), conv_w.reshape(1, CONV_WIDTH * D_MODEL),
        w_conv_out.reshape(D_MODEL, D_MODEL), attn_sinks.reshape(N_HEADS).astype(F32), rel_bias.astype(F32),
        w_attn_out.reshape(D_MODEL, D_MODEL), w_o.reshape(D_MODEL, D_MODEL), w_up.reshape(D_MODEL, D_FF),
        w_down.reshape(D_FF, D_MODEL))
    x = _mlp(x.reshape(batch * seq, D_MODEL), mlp_norm_g.reshape(1, D_MODEL), w_up_bf16, w_down_bf16,
             final_norm_g.reshape(1, D_MODEL))
    return x.reshape(batch, seq, D_MODEL)
```

```python
import functools
import math

import jax
import jax.numpy as jnp
import numpy as np
from jax import lax
from jax.experimental import pallas as pl
from jax.experimental.pallas import tpu as pltpu

D_MODEL = 1024
HEAD_DIM = 64
N_HEADS = 16
N_KV_HEADS = 2
GROUP = N_HEADS // N_KV_HEADS
BLOCK = 128
N_BUCKETS = 32
MAX_DISTANCE = 128
CONV_WIDTH = 3
D_FF = 4 * D_MODEL
EPS = 1e-6
NEG_INF = -1e30
LOG2E = math.log2(math.e)

LANES = 128
SUBLANES = 8
BF16_SUBLANES = 16
KV_COLS = N_KV_HEADS * HEAD_DIM
PAIRS_PER_GROUP = GROUP // 2

OFF_B, OFF_C, OFF_X, OFF_Q = 0, D_MODEL, 2 * D_MODEL, 3 * D_MODEL
OFF_K = 4 * D_MODEL
OFF_V = OFF_K + KV_COLS
OFF_GC = OFF_V + KV_COLS
OFF_GA = OFF_GC + D_MODEL
IN_COLS = OFF_GA + D_MODEL

TILE_MIX = 512
SUBTILES_MLP = (512, 512)
VMEM_LIMIT = 56 * 1024 * 1024
STAGE_COLS = 640
STAGE_SLOTS = 3

BF16 = jnp.bfloat16
F32 = jnp.float32

NT_DIMS = (((1,), (1,)), ((), ()))
TN_DIMS = (((0,), (0,)), ((), ()))


def _bucket_table():
    kj = np.arange(2 * BLOCK)[:, None]
    qi = np.arange(BLOCK)[None, :]
    dist = qi + BLOCK - kj
    n = np.maximum(dist, 0)
    max_exact = N_BUCKETS // 2
    ratio = np.log(np.maximum(n, max_exact).astype(np.float32) / max_exact) / np.log(MAX_DISTANCE / max_exact)
    large = np.minimum(max_exact + (ratio * (N_BUCKETS - max_exact)).astype(np.int32), N_BUCKETS - 1)
    bucket = np.where(n < max_exact, n, large).astype(np.int32)
    in_window = (dist >= 0) & (dist < BLOCK)
    return np.where(in_window, bucket, -1).astype(np.int32)


def _bias_table(head, bucket_ref, rel_ref, bias_buf):
    bucket = bucket_ref[...]
    acc = jnp.full(bucket.shape, NEG_INF, F32)
    for b in range(N_BUCKETS):
        acc = jnp.where(bucket == b, rel_ref[b, head], acc)
    acc = acc * LOG2E
    bias_buf[0, head] = acc
    key = lax.broadcasted_iota(jnp.int32, bucket.shape, 0)
    bias_buf[1, head] = jnp.where(key >= BLOCK, acc, NEG_INF)


def _rmsnorm(x, g):
    y = x * lax.rsqrt(jnp.mean(x * x, axis=-1, keepdims=True) + EPS)
    return y * g


def _stage_weights(pairs, stage, sems, fillers):
    chunks = []
    for src, dst in pairs:
        cols = src.shape[1]
        width = max(w for w in range(LANES, STAGE_COLS + 1, LANES) if cols % w == 0)
        chunks += [(src, dst, c, width) for c in range(0, cols, width)]
    slots = stage.shape[0]

    def copy(i):
        src, _, c, width = chunks[i]
        return pltpu.make_async_copy(src.at[:, c:c + width], stage.at[i % slots, :, 0:width], sems.at[i % slots])

    for i in range(min(slots, len(chunks))):
        copy(i).start()
    fillers = list(fillers)
    assert len(fillers) <= len(chunks)
    for i, (_, dst, c, width) in enumerate(chunks):
        if fillers:
            fillers.pop(0)()
        copy(i).wait()
        dst[:, c:c + width] = stage[i % slots, :, 0:width].astype(BF16)
        if i + slots < len(chunks):
            copy(i + slots).start()


def _mixer_kernel(x_ref, g_ref, win_hbm, convw_ref, wco_hbm, sinks_ref, bucket_ref, rel_ref, wao_hbm, wo_hbm,
                  wup_ref, wdown_ref, out_ref, wup_out_ref, wdown_out_ref,
                  win_buf, wco_buf, wao_buf, wo_buf, stage, stage_sems, bias_buf, cx_buf, kv_buf, attn_buf,
                  *, tile, tiles):
    s = pl.program_id(0)

    @pl.when(s == 0)
    def _():
        cx_buf[:, 0:SUBLANES, :] = jnp.zeros((D_MODEL // LANES, SUBLANES, LANES), F32)
        kv_buf[0:BLOCK, :] = jnp.zeros((BLOCK, 4 * KV_COLS), BF16)
        bias_tasks = [functools.partial(_bias_table, head, bucket_ref, rel_ref, bias_buf)
                      for head in range(N_HEADS)]
        _stage_weights([(win_hbm, win_buf), (wco_hbm, wco_buf), (wao_hbm, wao_buf), (wo_hbm, wo_buf)],
                       stage, stage_sems, bias_tasks)

    @pl.when(s > 0)
    def _():
        _mixer_step((s - 1) % tiles, x_ref, g_ref, win_buf, convw_ref, wco_buf, sinks_ref, bias_buf, wao_buf,
                    wo_buf, wup_ref, wdown_ref, out_ref, wup_out_ref, wdown_out_ref, cx_buf, kv_buf, attn_buf,
                    tile=tile, tiles=tiles)


def _mixer_step(t, x_ref, g_ref, win_ref, convw_ref, wco_ref, sinks_ref, bias_ref, wao_ref, wo_ref,
                wup_ref, wdown_ref, out_ref, wup_out_ref, wdown_out_ref, cx_buf, kv_buf, attn_buf,
                *, tile, tiles):
    last_tile = t == tiles - 1
    x = x_ref[0]
    half = tile // 2
    h_halves = [_rmsnorm(x[r * half:(r + 1) * half], g_ref[...]).astype(BF16) for r in range(2)]
    h = jnp.concatenate(h_halves, axis=0)

    def proj(lo, width):
        return jnp.dot(h, win_ref[:, lo:lo + width], preferred_element_type=F32)

    w_q = win_ref[:, OFF_Q:OFF_Q + D_MODEL]
    w_kv = win_ref[:, OFF_K:OFF_K + 2 * KV_COLS]
    q_0 = jnp.dot(h_halves[0], w_q, preferred_element_type=F32)
    kv = jnp.concatenate([jnp.dot(hr, w_kv, preferred_element_type=F32) for hr in h_halves], axis=0)
    k = kv[:, 0:KV_COLS]
    val = kv[:, KV_COLS:2 * KV_COLS]
    k_swapped = pltpu.roll(k, HEAD_DIM, axis=1)
    low_half_tile = lax.broadcasted_iota(jnp.int32, (tile, LANES), 1) < HEAD_DIM
    new_rows = slice(BLOCK, BLOCK + tile)
    kv_buf[new_rows, 0:KV_COLS] = jnp.where(low_half_tile, k, k_swapped).astype(BF16)
    kv_buf[new_rows, KV_COLS:2 * KV_COLS] = jnp.where(low_half_tile, k_swapped, k).astype(BF16)
    kv_buf[new_rows, 2 * KV_COLS:3 * KV_COLS] = jnp.where(low_half_tile, val, 1.0).astype(BF16)
    kv_buf[new_rows, 3 * KV_COLS:4 * KV_COLS] = jnp.where(low_half_tile, 1.0, val).astype(BF16)
    q_1 = jnp.dot(h_halves[1], w_q, preferred_element_type=F32)
    q = (jnp.concatenate([q_0, q_1], axis=0) * (HEAD_DIM ** -0.5 * LOG2E)).astype(BF16)

    low_half = lax.broadcasted_iota(jnp.int32, (BLOCK, LANES), 1) < HEAD_DIM
    zero = jnp.zeros((BLOCK, LANES), BF16)

    scores, probs, sink_terms = {}, {}, {}

    def qk(j):
        rows = slice(j * BLOCK, (j + 2) * BLOCK)
        qj = q[j * BLOCK:(j + 1) * BLOCK, :]
        for g in range(N_KV_HEADS):
            parts = []
            for p in range(g * PAIRS_PER_GROUP, (g + 1) * PAIRS_PER_GROUP):
                qp = qj[:, p * LANES:(p + 1) * LANES]
                parts.append(jnp.where(low_half, qp, zero))
                parts.append(jnp.where(low_half, zero, qp))
            q_heads = jnp.concatenate(parts, axis=0)
            keys = kv_buf[rows, g * KV_COLS:(g + 1) * KV_COLS]
            scores[j, g] = lax.dot_general(keys, q_heads, NT_DIMS, preferred_element_type=F32)

    def softmax(j, g):
        first = (t == 0).astype(jnp.int32) if j == 0 else 0
        s = scores.pop((j, g))
        for i in range(GROUP):
            head = g * GROUP + i
            sh = s[:, i * BLOCK:(i + 1) * BLOCK] + bias_ref[first, head]
            sink = jnp.full((1, BLOCK), sinks_ref[head], F32) * LOG2E
            m = jnp.maximum(jnp.max(sh, axis=0, keepdims=True), sink)
            probs[j, head] = jnp.exp2(sh - m).astype(BF16)
            sink_terms[j, head] = jnp.exp2(sink - m)

    def pv(j):
        rows = slice(j * BLOCK, (j + 2) * BLOCK)
        for g in range(N_KV_HEADS):
            vals = kv_buf[rows, (2 + g) * KV_COLS:(3 + g) * KV_COLS]
            p = jnp.concatenate([probs.pop((j, g * GROUP + i)) for i in range(GROUP)], axis=1)
            o = lax.dot_general(vals, p, TN_DIMS, preferred_element_type=F32)
            out_rows = slice(g * HEAD_DIM, (g + 1) * HEAD_DIM)
            sum_row = (1 - g) * HEAD_DIM
            for i in range(GROUP):
                head = g * GROUP + i
                cols = slice(i * BLOCK, (i + 1) * BLOCK)
                denom = o[sum_row:sum_row + 1, cols] + sink_terms.pop((j, head))
                oh = o[out_rows, cols] * (1.0 / denom)
                attn_buf[head * HEAD_DIM:(head + 1) * HEAD_DIM, j * BLOCK:(j + 1) * BLOCK] = oh.astype(BF16)

    v = {}

    def t_c():
        v["c"] = proj(OFF_C, D_MODEL)

    def t_x():
        cx = v.pop("c") * proj(OFF_X, D_MODEL)
        for c in range(D_MODEL // LANES):
            cx_buf[c, SUBLANES:SUBLANES + tile, :] = cx[:, c * LANES:(c + 1) * LANES]

    def t_conv():
        slabs = []
        for c in range(D_MODEL // LANES):
            taps = [convw_ref[0:1, k * D_MODEL + c * LANES:k * D_MODEL + (c + 1) * LANES] for k in range(CONV_WIDTH)]
            slabs.append(taps[0] * cx_buf[c, SUBLANES - 2:SUBLANES - 2 + tile, :]
                         + taps[1] * cx_buf[c, SUBLANES - 1:SUBLANES - 1 + tile, :]
                         + taps[2] * cx_buf[c, SUBLANES:SUBLANES + tile, :])
        v["conv"] = jnp.concatenate(slabs, axis=1)
        cx_buf[:, 0:SUBLANES, :] = jnp.where(last_tile, 0.0, cx_buf[:, tile:tile + SUBLANES, :])

    def t_b():
        v["u"] = (proj(OFF_B, D_MODEL) * v.pop("conv")).astype(BF16)

    def t_gc():
        v["gate_conv"] = jax.nn.sigmoid(proj(OFF_GC, D_MODEL))

    def t_ga():
        v["gate_attn"] = jax.nn.sigmoid(proj(OFF_GA, D_MODEL))

    def t_yconv():
        v["mixed"] = v.pop("gate_conv") * jnp.dot(v.pop("u"), wco_ref[...], preferred_element_type=F32)

    def t_yattn(r):
        rows = slice(r * half, (r + 1) * half)
        y_attn = lax.dot_general(attn_buf[:, rows], wao_ref[...], TN_DIMS, preferred_element_type=F32)
        v["mixed", r] = (v["mixed"][rows] + v["gate_attn"][rows] * y_attn).astype(BF16)

    def t_out(r):
        rows = slice(r * half, (r + 1) * half)
        out_ref[0, rows, :] = x[rows] + jnp.dot(v.pop(("mixed", r)), wo_ref[...], preferred_element_type=F32)

    qk(0)
    t_c()
    wup_out_ref[...] = wup_ref[...].astype(BF16)
    wdown_out_ref[...] = wdown_ref[...].astype(BF16)
    softmax(0, 0)
    t_x()
    softmax(0, 1)
    t_gc()
    t_conv()
    pv(0)
    qk(1)
    t_b()
    softmax(1, 0)
    t_ga()
    softmax(1, 1)
    pv(1)
    t_yconv()
    qk(2)
    softmax(2, 0)
    t_yattn(0)
    softmax(2, 1)
    pv(2)
    qk(3)
    t_out(0)
    softmax(3, 0)
    softmax(3, 1)
    pv(3)
    kv_buf[0:BLOCK, :] = jnp.where(last_tile, jnp.zeros((), BF16), kv_buf[tile:tile + BLOCK, :])
    t_yattn(1)
    t_out(1)


def _mlp_kernel(x_ref, g_ref, wup_ref, wdown_ref, gf_ref, out_ref, *, subtiles):
    normed, act, mixed = {}, {}, {}
    starts = [sum(subtiles[:i]) for i in range(len(subtiles))]

    def rows(i):
        return slice(starts[i], starts[i] + subtiles[i])

    def norm(i):
        normed[i] = _rmsnorm(x_ref[rows(i), :], g_ref[...]).astype(BF16)

    def up(i):
        u = jnp.dot(normed.pop(i), wup_ref[...], preferred_element_type=F32)
        act[i] = jnp.square(jnp.maximum(u, 0.0)).astype(BF16)

    def down(i):
        mixed[i] = x_ref[rows(i), :] + jnp.dot(act.pop(i), wdown_ref[...], preferred_element_type=F32)

    def final(i):
        out_ref[rows(i), :] = _rmsnorm(mixed.pop(i), gf_ref[...])

    n = len(subtiles)
    norm(0)
    up(0)
    for i in range(n):
        if i + 1 < n:
            norm(i + 1)
            up(i + 1)
        if i > 0:
            final(i - 1)
        down(i)
    final(n - 1)


def _resident(shape):
    return pl.BlockSpec(shape, lambda *_: (0,) * len(shape), pipeline_mode=pl.Buffered(1))


def _mixer(x, g, w_in, conv_w, w_conv_out, sinks, rel_bias, w_attn_out, w_o, w_up, w_down):
    batch, seq, _ = x.shape
    tile = TILE_MIX
    assert seq % tile == 0 and tile % BLOCK == 0
    tiles = seq // tile
    steps = batch * tiles
    up_rows, down_rows = D_MODEL // steps, D_FF // steps
    assert up_rows * steps == D_MODEL and up_rows % BF16_SUBLANES == 0
    assert down_rows * steps == D_FF and down_rows % BF16_SUBLANES == 0

    def tile_of(s):
        return jnp.maximum(s - 1, 0)

    def x_map(s):
        return (tile_of(s) // tiles, tile_of(s) % tiles, 0)

    def slice_map(s):
        return (tile_of(s), 0)

    hbm = pl.BlockSpec(memory_space=pl.ANY)
    return pl.pallas_call(
        functools.partial(_mixer_kernel, tile=tile, tiles=tiles),
        grid=(steps + 1,),
        in_specs=[
            pl.BlockSpec((1, tile, D_MODEL), x_map),
            _resident((1, D_MODEL)),
            hbm,
            _resident((1, CONV_WIDTH * D_MODEL)),
            hbm,
            pl.BlockSpec(memory_space=pltpu.SMEM),
            _resident((2 * BLOCK, BLOCK)),
            pl.BlockSpec(memory_space=pltpu.SMEM),
            hbm,
            hbm,
            pl.BlockSpec((up_rows, D_FF), slice_map),
            pl.BlockSpec((down_rows, D_MODEL), slice_map),
        ],
        out_specs=[
            pl.BlockSpec((1, tile, D_MODEL), x_map),
            pl.BlockSpec((up_rows, D_FF), slice_map),
            pl.BlockSpec((down_rows, D_MODEL), slice_map),
        ],
        out_shape=[
            jax.ShapeDtypeStruct(x.shape, F32),
            jax.ShapeDtypeStruct(w_up.shape, BF16),
            jax.ShapeDtypeStruct(w_down.shape, BF16),
        ],
        scratch_shapes=[
            pltpu.VMEM((D_MODEL, IN_COLS), BF16),
            pltpu.VMEM((D_MODEL, D_MODEL), BF16),
            pltpu.VMEM((D_MODEL, D_MODEL), BF16),
            pltpu.VMEM((D_MODEL, D_MODEL), BF16),
            pltpu.VMEM((STAGE_SLOTS, D_MODEL, STAGE_COLS), F32),
            pltpu.SemaphoreType.DMA((STAGE_SLOTS,)),
            pltpu.VMEM((2, N_HEADS, 2 * BLOCK, BLOCK), F32),
            pltpu.VMEM((D_MODEL // LANES, tile + 2 * SUBLANES, LANES), F32),
            pltpu.VMEM((tile + BLOCK, 4 * KV_COLS), BF16),
            pltpu.VMEM((D_MODEL, tile), BF16),
        ],
        compiler_params=pltpu.CompilerParams(
            dimension_semantics=("arbitrary",), vmem_limit_bytes=VMEM_LIMIT),
        name="token_mixer",
    )(x, g, w_in, conv_w, w_conv_out, sinks, jnp.asarray(_bucket_table()), rel_bias, w_attn_out, w_o, w_up, w_down)


def _mlp(x, g, w_up, w_down, g_final):
    rows = x.shape[0]
    tile = sum(SUBTILES_MLP)
    assert rows % tile == 0 and all(s % BF16_SUBLANES == 0 for s in SUBTILES_MLP)
    return pl.pallas_call(
        functools.partial(_mlp_kernel, subtiles=SUBTILES_MLP),
        grid=(rows // tile,),
        in_specs=[
            pl.BlockSpec((tile, D_MODEL), lambda i: (i, 0)),
            _resident((1, D_MODEL)),
            _resident((D_MODEL, D_FF)),
            _resident((D_FF, D_MODEL)),
            _resident((1, D_MODEL)),
        ],
        out_specs=pl.BlockSpec((tile, D_MODEL), lambda i: (i, 0)),
        out_shape=jax.ShapeDtypeStruct(x.shape, F32),
        compiler_params=pltpu.CompilerParams(
            dimension_semantics=("arbitrary",), vmem_limit_bytes=VMEM_LIMIT),
        name="channel_mixer",
    )(x, g, w_up, w_down, g_final)


def kernel(x, attn_norm_g, w_in, conv_w, w_conv_out, attn_sinks, rel_bias, w_attn_out, w_o, mlp_norm_g, w_up,
           w_down, final_norm_g):
    batch, seq, _ = x.shape
    depth = w_in.shape[0]
    assert depth == 1, "the final norm is fused into the (only) layer's channel mixer"
    x, w_up_bf16, w_down_bf16 = _mixer(
        x, attn_norm_g.reshape(1, D_MODEL), w_in.reshape(D_MODEL, IN_COLS), conv_w.reshape(1, CONV_WIDTH * D_MODEL),
        w_conv_out.reshape(D_MODEL, D_MODEL), attn_sinks.reshape(N_HEADS).astype(F32), rel_bias.astype(F32),
        w_attn_out.reshape(D_MODEL, D_MODEL), w_o.reshape(D_MODEL, D_MODEL), w_up.reshape(D_MODEL, D_FF),
        w_down.reshape(D_FF, D_MODEL))
    x = _mlp(x.reshape(batch * seq, D_MODEL), mlp_norm_g.reshape(1, D_MODEL), w_up_bf16, w_down_bf16,
             final_norm_g.reshape(1, D_MODEL))
    return x.reshape(batch, seq, D_MODEL)
```

```python
import functools
import math

import jax
import jax.numpy as jnp
import numpy as np
from jax import lax
from jax.experimental import pallas as pl
from jax.experimental.pallas import tpu as pltpu

D_MODEL = 1024
HEAD_DIM = 64
N_HEADS = 16
N_KV_HEADS = 2
GROUP = N_HEADS // N_KV_HEADS
BLOCK = 128
N_BUCKETS = 32
MAX_DISTANCE = 128
CONV_WIDTH = 3
D_FF = 4 * D_MODEL
EPS = 1e-6
NEG_INF = -1e30
LOG2E = math.log2(math.e)

LANES = 128
SUBLANES = 8
BF16_SUBLANES = 16
KV_COLS = N_KV_HEADS * HEAD_DIM
PAIRS_PER_GROUP = GROUP // 2

OFF_B, OFF_C, OFF_X, OFF_Q = 0, D_MODEL, 2 * D_MODEL, 3 * D_MODEL
OFF_K = 4 * D_MODEL
OFF_V = OFF_K + KV_COLS
OFF_GC = OFF_V + KV_COLS
OFF_GA = OFF_GC + D_MODEL
IN_COLS = OFF_GA + D_MODEL

TILE_MIX = 512
SUBTILES_MLP = (512, 512)
VMEM_LIMIT = 56 * 1024 * 1024
STAGE_COLS = 640
STAGE_SLOTS = 3

BF16 = jnp.bfloat16
F32 = jnp.float32

NT_DIMS = (((1,), (1,)), ((), ()))
TN_DIMS = (((0,), (0,)), ((), ()))


def _bucket_table():
    kj = np.arange(2 * BLOCK)[:, None]
    qi = np.arange(BLOCK)[None, :]
    dist = qi + BLOCK - kj
    n = np.maximum(dist, 0)
    max_exact = N_BUCKETS // 2
    ratio = np.log(np.maximum(n, max_exact).astype(np.float32) / max_exact) / np.log(MAX_DISTANCE / max_exact)
    large = np.minimum(max_exact + (ratio * (N_BUCKETS - max_exact)).astype(np.int32), N_BUCKETS - 1)
    bucket = np.where(n < max_exact, n, large).astype(np.int32)
    in_window = (dist >= 0) & (dist < BLOCK)
    return np.where(in_window, bucket, -1).astype(np.int32)


def _bias_table(head, bucket_ref, rel_ref, bias_buf):
    bucket = bucket_ref[...]
    acc = jnp.full(bucket.shape, NEG_INF, F32)
    for b in range(N_BUCKETS):
        acc = jnp.where(bucket == b, rel_ref[b, head], acc)
    bias_buf[head] = acc * LOG2E


def _rmsnorm(x, g):
    y = x * lax.rsqrt(jnp.mean(x * x, axis=-1, keepdims=True) + EPS)
    return y * g


def _stage_weights(pairs, stage, sems, fillers):
    chunks = []
    for src, dst in pairs:
        cols = src.shape[1]
        width = max(w for w in range(LANES, STAGE_COLS + 1, LANES) if cols % w == 0)
        chunks += [(src, dst, c, width) for c in range(0, cols, width)]
    slots = stage.shape[0]

    def copy(i):
        src, _, c, width = chunks[i]
        return pltpu.make_async_copy(src.at[:, c:c + width], stage.at[i % slots, :, 0:width], sems.at[i % slots])

    for i in range(min(slots, len(chunks))):
        copy(i).start()
    fillers = list(fillers)
    assert len(fillers) <= len(chunks)
    for i, (_, dst, c, width) in enumerate(chunks):
        if fillers:
            fillers.pop(0)()
        copy(i).wait()
        dst[:, c:c + width] = stage[i % slots, :, 0:width].astype(BF16)
        if i + slots < len(chunks):
            copy(i + slots).start()


def _mixer_kernel(x_ref, g_ref, win_hbm, convw_ref, wco_hbm, sinks_ref, bucket_ref, rel_ref, wao_hbm, wo_hbm,
                  wup_ref, wdown_ref, out_ref, wup_out_ref, wdown_out_ref,
                  win_buf, wco_buf, wao_buf, wo_buf, stage, stage_sems, bias_buf, cx_buf, kv_buf, attn_buf,
                  *, tile, tiles):
    s = pl.program_id(0)

    @pl.when(s == 0)
    def _():
        cx_buf[:, 0:SUBLANES, :] = jnp.zeros((D_MODEL // LANES, SUBLANES, LANES), F32)
        kv_buf[0:BLOCK, :] = jnp.zeros((BLOCK, 4 * KV_COLS), BF16)
        bias_tasks = [functools.partial(_bias_table, head, bucket_ref, rel_ref, bias_buf)
                      for head in range(N_HEADS)]
        _stage_weights([(win_hbm, win_buf), (wco_hbm, wco_buf), (wao_hbm, wao_buf), (wo_hbm, wo_buf)],
                       stage, stage_sems, bias_tasks)

    @pl.when(s > 0)
    def _():
        _mixer_step((s - 1) % tiles, x_ref, g_ref, win_buf, convw_ref, wco_buf, sinks_ref, bias_buf, wao_buf,
                    wo_buf, wup_ref, wdown_ref, out_ref, wup_out_ref, wdown_out_ref, cx_buf, kv_buf, attn_buf,
                    tile=tile, tiles=tiles)


def _mixer_step(t, x_ref, g_ref, win_ref, convw_ref, wco_ref, sinks_ref, bias_ref, wao_ref, wo_ref,
                wup_ref, wdown_ref, out_ref, wup_out_ref, wdown_out_ref, cx_buf, kv_buf, attn_buf,
                *, tile, tiles):
    last_tile = t == tiles - 1
    x = x_ref[0]
    half = tile // 2
    h_halves = [_rmsnorm(x[r * half:(r + 1) * half], g_ref[...]).astype(BF16) for r in range(2)]
    h = jnp.concatenate(h_halves, axis=0)

    def proj(lo, width):
        return jnp.dot(h, win_ref[:, lo:lo + width], preferred_element_type=F32)

    w_q = win_ref[:, OFF_Q:OFF_Q + D_MODEL]
    w_kv = win_ref[:, OFF_K:OFF_K + 2 * KV_COLS]
    q_0 = jnp.dot(h_halves[0], w_q, preferred_element_type=F32)
    kv = jnp.concatenate([jnp.dot(hr, w_kv, preferred_element_type=F32) for hr in h_halves], axis=0)
    k = kv[:, 0:KV_COLS]
    val = kv[:, KV_COLS:2 * KV_COLS]
    k_swapped = pltpu.roll(k, HEAD_DIM, axis=1)
    low_half_tile = lax.broadcasted_iota(jnp.int32, (tile, LANES), 1) < HEAD_DIM
    new_rows = slice(BLOCK, BLOCK + tile)
    kv_buf[new_rows, 0:KV_COLS] = jnp.where(low_half_tile, k, k_swapped).astype(BF16)
    kv_buf[new_rows, KV_COLS:2 * KV_COLS] = jnp.where(low_half_tile, k_swapped, k).astype(BF16)
    kv_buf[new_rows, 2 * KV_COLS:3 * KV_COLS] = jnp.where(low_half_tile, val, 1.0).astype(BF16)
    kv_buf[new_rows, 3 * KV_COLS:4 * KV_COLS] = jnp.where(low_half_tile, 1.0, val).astype(BF16)
    q_1 = jnp.dot(h_halves[1], w_q, preferred_element_type=F32)
    q = (jnp.concatenate([q_0, q_1], axis=0) * (HEAD_DIM ** -0.5 * LOG2E)).astype(BF16)

    low_half = lax.broadcasted_iota(jnp.int32, (BLOCK, LANES), 1) < HEAD_DIM
    zero = jnp.zeros((BLOCK, LANES), BF16)

    scores, probs, sink_terms = {}, {}, {}

    def qk(j):
        rows = slice(j * BLOCK, (j + 2) * BLOCK)
        qj = q[j * BLOCK:(j + 1) * BLOCK, :]
        for g in range(N_KV_HEADS):
            parts = []
            for p in range(g * PAIRS_PER_GROUP, (g + 1) * PAIRS_PER_GROUP):
                qp = qj[:, p * LANES:(p + 1) * LANES]
                parts.append(jnp.where(low_half, qp, zero))
                parts.append(jnp.where(low_half, zero, qp))
            q_heads = jnp.concatenate(parts, axis=0)
            keys = kv_buf[rows, g * KV_COLS:(g + 1) * KV_COLS]
            scores[j, g] = lax.dot_general(keys, q_heads, NT_DIMS, preferred_element_type=F32)

    def softmax(j, g):
        s = scores.pop((j, g))
        if j == 0:
            key = lax.broadcasted_iota(jnp.int32, (2 * BLOCK, BLOCK), 0)
            no_prev = (key < BLOCK) & (t == 0)
        for i in range(GROUP):
            head = g * GROUP + i
            bias = bias_ref[head]
            if j == 0:
                bias = jnp.where(no_prev, NEG_INF, bias)
            sh = s[:, i * BLOCK:(i + 1) * BLOCK] + bias
            sink = jnp.full((1, BLOCK), sinks_ref[head], F32) * LOG2E
            m = jnp.maximum(jnp.max(sh, axis=0, keepdims=True), sink)
            probs[j, head] = jnp.exp2(sh - m).astype(BF16)
            sink_terms[j, head] = jnp.exp2(sink - m)

    def pv(j):
        rows = slice(j * BLOCK, (j + 2) * BLOCK)
        for g in range(N_KV_HEADS):
            vals = kv_buf[rows, (2 + g) * KV_COLS:(3 + g) * KV_COLS]
            p = jnp.concatenate([probs.pop((j, g * GROUP + i)) for i in range(GROUP)], axis=1)
            o = lax.dot_general(vals, p, TN_DIMS, preferred_element_type=F32)
            out_rows = slice(g * HEAD_DIM, (g + 1) * HEAD_DIM)
            sum_row = (1 - g) * HEAD_DIM
            for i in range(GROUP):
                head = g * GROUP + i
                cols = slice(i * BLOCK, (i + 1) * BLOCK)
                denom = o[sum_row:sum_row + 1, cols] + sink_terms.pop((j, head))
                oh = o[out_rows, cols] * (1.0 / denom)
                attn_buf[head * HEAD_DIM:(head + 1) * HEAD_DIM, j * BLOCK:(j + 1) * BLOCK] = oh.astype(BF16)

    v = {}

    def t_c():
        v["c"] = proj(OFF_C, D_MODEL)

    def t_x():
        cx = v.pop("c") * proj(OFF_X, D_MODEL)
        for c in range(D_MODEL // LANES):
            cx_buf[c, SUBLANES:SUBLANES + tile, :] = cx[:, c * LANES:(c + 1) * LANES]

    def t_conv():
        slabs = []
        for c in range(D_MODEL // LANES):
            taps = [convw_ref[0:1, k * D_MODEL + c * LANES:k * D_MODEL + (c + 1) * LANES] for k in range(CONV_WIDTH)]
            slabs.append(taps[0] * cx_buf[c, SUBLANES - 2:SUBLANES - 2 + tile, :]
                         + taps[1] * cx_buf[c, SUBLANES - 1:SUBLANES - 1 + tile, :]
                         + taps[2] * cx_buf[c, SUBLANES:SUBLANES + tile, :])
        v["conv"] = jnp.concatenate(slabs, axis=1)
        cx_buf[:, 0:SUBLANES, :] = jnp.where(last_tile, 0.0, cx_buf[:, tile:tile + SUBLANES, :])

    def t_b():
        v["u"] = (proj(OFF_B, D_MODEL) * v.pop("conv")).astype(BF16)

    def t_gc():
        v["gate_conv"] = jax.nn.sigmoid(proj(OFF_GC, D_MODEL))

    def t_ga():
        v["gate_attn"] = jax.nn.sigmoid(proj(OFF_GA, D_MODEL))

    def t_yconv():
        v["mixed"] = v.pop("gate_conv") * jnp.dot(v.pop("u"), wco_ref[...], preferred_element_type=F32)

    def t_yattn(r):
        rows = slice(r * half, (r + 1) * half)
        y_attn = lax.dot_general(attn_buf[:, rows], wao_ref[...], TN_DIMS, preferred_element_type=F32)
        v["mixed", r] = (v["mixed"][rows] + v["gate_attn"][rows] * y_attn).astype(BF16)

    def t_out(r):
        rows = slice(r * half, (r + 1) * half)
        out_ref[0, rows, :] = x[rows] + jnp.dot(v.pop(("mixed", r)), wo_ref[...], preferred_element_type=F32)

    qk(0)
    t_c()
    wup_out_ref[...] = wup_ref[...].astype(BF16)
    wdown_out_ref[...] = wdown_ref[...].astype(BF16)
    softmax(0, 0)
    t_x()
    softmax(0, 1)
    t_gc()
    t_conv()
    pv(0)
    qk(1)
    t_b()
    softmax(1, 0)
    t_ga()
    softmax(1, 1)
    pv(1)
    t_yconv()
    qk(2)
    softmax(2, 0)
    t_yattn(0)
    softmax(2, 1)
    pv(2)
    qk(3)
    t_out(0)
    softmax(3, 0)
    softmax(3, 1)
    pv(3)
    kv_buf[0:BLOCK, :] = jnp.where(last_tile, jnp.zeros((), BF16), kv_buf[tile:tile + BLOCK, :])
    t_yattn(1)
    t_out(1)


def _mlp_kernel(x_ref, g_ref, wup_ref, wdown_ref, gf_ref, out_ref, *, subtiles):
    normed, act, mixed = {}, {}, {}
    starts = [sum(subtiles[:i]) for i in range(len(subtiles))]

    def rows(i):
        return slice(starts[i], starts[i] + subtiles[i])

    def norm(i):
        normed[i] = _rmsnorm(x_ref[rows(i), :], g_ref[...]).astype(BF16)

    def up(i):
        u = jnp.dot(normed.pop(i), wup_ref[...], preferred_element_type=F32)
        act[i] = jnp.square(jnp.maximum(u, 0.0)).astype(BF16)

    def down(i):
        mixed[i] = x_ref[rows(i), :] + jnp.dot(act.pop(i), wdown_ref[...], preferred_element_type=F32)

    def final(i):
        out_ref[rows(i), :] = _rmsnorm(mixed.pop(i), gf_ref[...])

    n = len(subtiles)
    norm(0)
    up(0)
    for i in range(n):
        if i + 1 < n:
            norm(i + 1)
            up(i + 1)
        if i > 0:
            final(i - 1)
        down(i)
    final(n - 1)


def _resident(shape):
    return pl.BlockSpec(shape, lambda *_: (0,) * len(shape), pipeline_mode=pl.Buffered(1))


def _mixer(x, g, w_in, conv_w, w_conv_out, sinks, rel_bias, w_attn_out, w_o, w_up, w_down):
    batch, seq, _ = x.shape
    tile = TILE_MIX
    assert seq % tile == 0 and tile % BLOCK == 0
    tiles = seq // tile
    steps = batch * tiles
    up_rows, down_rows = D_MODEL // steps, D_FF // steps
    assert up_rows * steps == D_MODEL and up_rows % BF16_SUBLANES == 0
    assert down_rows * steps == D_FF and down_rows % BF16_SUBLANES == 0

    def tile_of(s):
        return jnp.maximum(s - 1, 0)

    def x_map(s):
        return (tile_of(s) // tiles, tile_of(s) % tiles, 0)

    def slice_map(s):
        return (tile_of(s), 0)

    hbm = pl.BlockSpec(memory_space=pl.ANY)
    return pl.pallas_call(
        functools.partial(_mixer_kernel, tile=tile, tiles=tiles),
        grid=(steps + 1,),
        in_specs=[
            pl.BlockSpec((1, tile, D_MODEL), x_map),
            _resident((1, D_MODEL)),
            hbm,
            _resident((1, CONV_WIDTH * D_MODEL)),
            hbm,
            pl.BlockSpec(memory_space=pltpu.SMEM),
            _resident((2 * BLOCK, BLOCK)),
            pl.BlockSpec(memory_space=pltpu.SMEM),
            hbm,
            hbm,
            pl.BlockSpec((up_rows, D_FF), slice_map),
            pl.BlockSpec((down_rows, D_MODEL), slice_map),
        ],
        out_specs=[
            pl.BlockSpec((1, tile, D_MODEL), x_map),
            pl.BlockSpec((up_rows, D_FF), slice_map),
            pl.BlockSpec((down_rows, D_MODEL), slice_map),
        ],
        out_shape=[
            jax.ShapeDtypeStruct(x.shape, F32),
            jax.ShapeDtypeStruct(w_up.shape, BF16),
            jax.ShapeDtypeStruct(w_down.shape, BF16),
        ],
        scratch_shapes=[
            pltpu.VMEM((D_MODEL, IN_COLS), BF16),
            pltpu.VMEM((D_MODEL, D_MODEL), BF16),
            pltpu.VMEM((D_MODEL, D_MODEL), BF16),
            pltpu.VMEM((D_MODEL, D_MODEL), BF16),
            pltpu.VMEM((STAGE_SLOTS, D_MODEL, STAGE_COLS), F32),
            pltpu.SemaphoreType.DMA((STAGE_SLOTS,)),
            pltpu.VMEM((N_HEADS, 2 * BLOCK, BLOCK), F32),
            pltpu.VMEM((D_MODEL // LANES, tile + 2 * SUBLANES, LANES), F32),
            pltpu.VMEM((tile + BLOCK, 4 * KV_COLS), BF16),
            pltpu.VMEM((D_MODEL, tile), BF16),
        ],
        compiler_params=pltpu.CompilerParams(
            dimension_semantics=("arbitrary",), vmem_limit_bytes=VMEM_LIMIT),
        name="token_mixer",
    )(x, g, w_in, conv_w, w_conv_out, sinks, jnp.asarray(_bucket_table()), rel_bias, w_attn_out, w_o, w_up, w_down)


def _mlp(x, g, w_up, w_down, g_final):
    rows = x.shape[0]
    tile = sum(SUBTILES_MLP)
    assert rows % tile == 0 and all(s % BF16_SUBLANES == 0 for s in SUBTILES_MLP)
    return pl.pallas_call(
        functools.partial(_mlp_kernel, subtiles=SUBTILES_MLP),
        grid=(rows // tile,),
        in_specs=[
            pl.BlockSpec((tile, D_MODEL), lambda i: (i, 0)),
            _resident((1, D_MODEL)),
            _resident((D_MODEL, D_FF)),
            _resident((D_FF, D_MODEL)),
            _resident((1, D_MODEL)),
        ],
        out_specs=pl.BlockSpec((tile, D_MODEL), lambda i: (i, 0)),
        out_shape=jax.ShapeDtypeStruct(x.shape, F32),
        compiler_params=pltpu.CompilerParams(
            dimension_semantics=("arbitrary",), vmem_limit_bytes=VMEM_LIMIT),
        name="channel_mixer",
    )(x, g, w_up, w_down, g_final)


def kernel(x, attn_norm_g, w_in, conv_w, w_conv_out, attn_sinks, rel_bias, w_attn_out, w_o, mlp_norm_g, w_up,
           w_down, final_norm_g):
    batch, seq, _ = x.shape
    depth = w_in.shape[0]
    assert depth == 1, "the final norm is fused into the (only) layer's channel mixer"
    x, w_up_bf16, w_down_bf16 = _mixer(
        x, attn_norm_g.reshape(1, D_MODEL), w_in.reshape(D_MODEL, IN_COLS), conv_w.reshape(1, CONV_WIDTH * D_MODEL),
        w_conv_out.reshape(D_MODEL, D_MODEL), attn_sinks.reshape(N_HEADS).astype(F32), rel_bias.astype(F32),
        w_attn_out.reshape(D_MODEL, D_MODEL), w_o.reshape(D_MODEL, D_MODEL), w_up.reshape(D_MODEL, D_FF),
        w_down.reshape(D_FF, D_MODEL))
    x = _mlp(x.reshape(batch * seq, D_MODEL), mlp_norm_g.reshape(1, D_MODEL), w_up_bf16, w_down_bf16,
             final_norm_g.reshape(1, D_MODEL))
    return x.reshape(batch, seq, D_MODEL)
```
